```python
import jax, jax.numpy as jnp
from jax import lax
import numpy as np

D_MODEL = 2048
BATCH = 4
SEQ = 2048
DEPTH = 2
DEC_BATCH = 128
DEC_SEQ = 1
PAST_LEN = 16384
PAGE_SIZE = 128

MIX_W = D_MODEL
HG_W = MIX_W // 2
HG_HEADS = 8
HG_VDIM = HG_W // HG_HEADS
HG_EXPAND = 128
HG_FDIM = HG_HEADS * HG_EXPAND
SC_W = MIX_W // 4
SC_WIDTH = 3
CC_W = MIX_W - HG_W - SC_W
CC_WIDTH = 31
D_FF = 4 * D_MODEL
CHUNK = 64
EPS = 1e-6
F_FLOOR = 1e-30
IN_SPLITS = (HG_FDIM, HG_FDIM, HG_W, HG_W, SC_W, SC_W, SC_W, CC_W, CC_W)
IN_COLS = sum(IN_SPLITS)

kernel_name = "hymba_style_hgrn2_shortconv_conformer_decoder_step"


def rmsnorm(x, g):
    xf = x.astype(jnp.float32)
    y = xf * lax.rsqrt(jnp.mean(xf * xf, axis=-1, keepdims=True) + EPS)
    return (y * g.astype(jnp.float32)).astype(x.dtype)


def layernorm(x, g, b):
    xf = x.astype(jnp.float32)
    mu = jnp.mean(xf, axis=-1, keepdims=True)
    var = jnp.mean(jnp.square(xf - mu), axis=-1, keepdims=True)
    y = (xf - mu) * lax.rsqrt(var + EPS)
    return (y * g.astype(jnp.float32) + b.astype(jnp.float32)).astype(x.dtype)


def causal_dwconv(u, buf, w):
    width = w.shape[0]
    full = jnp.concatenate([buf.astype(u.dtype), u], axis=1)
    out = lax.conv_general_dilated(full, w[:, None, :].astype(u.dtype), window_strides=(1,),
                                   padding='VALID', dimension_numbers=('NWC', 'WIO', 'NWC'),
                                   feature_group_count=u.shape[-1])
    return out, full[:, full.shape[1] - (width - 1):]


def hgrn2_chunked(q, k, v, log_f, s0):
    n, l = q.shape[0], q.shape[1]
    c = min(CHUNK, l)
    pad = (-l) % c
    padw = ((0, 0), (0, pad), (0, 0), (0, 0))
    q, k, v, log_f = (jnp.pad(a, padw) for a in (q, k, v, log_f))
    nc = (l + pad) // c

    def to_chunks(a):
        return a.reshape(n, nc, c, a.shape[2], a.shape[3]).transpose(1, 0, 3, 2, 4)

    causal = jnp.tril(jnp.ones((c, c), dtype=bool))[:, :, None]

    def step(s, inp):
        qc, kc, vc, gc = inp
        b = jnp.cumsum(gc, axis=2)
        o_inter = jnp.einsum('nhck,nhkv->nhcv', qc * jnp.exp(b), s)
        diff = b[:, :, :, None, :] - b[:, :, None, :, :]
        decay = jnp.where(causal, jnp.exp(jnp.minimum(diff, 0.0)), 0.0)
        att = jnp.einsum('nhtk,nhtsk,nhsk->nhts', qc, decay, kc)
        o = o_inter + jnp.einsum('nhts,nhsv->nhtv', att, vc)
        b_last = b[:, :, -1:, :]
        s = jnp.exp(b_last[:, :, 0, :])[..., None] * s + jnp.einsum(
            'nhsk,nhsv->nhkv', kc * jnp.exp(b_last - b), vc)
        return s, o

    s_fin, o = lax.scan(step, s0, (to_chunks(q), to_chunks(k), to_chunks(v), to_chunks(log_f)))
    o = o.transpose(1, 0, 3, 2, 4).reshape(n, nc * c, q.shape[2], v.shape[3])[:, :l]
    return o, s_fin


def trunk_layer(x, st_h, st_s, st_c, lb, g_mix, w_in, hg_g, sc_w, cc_w, cc_b, cc_lg, cc_lb,
                w_out, g_mlp, w_up, w_down):
    n, l, _ = x.shape
    f32 = jnp.float32
    h = rmsnorm(x, g_mix)
    z = h @ w_in.astype(h.dtype)
    idx = list(np.cumsum(IN_SPLITS)[:-1])
    q, f_pre, i_v, og, sc_b, sc_c, sc_h, cc_v, cc_g = jnp.split(z, idx, axis=-1)

    q = jax.nn.silu(q.astype(f32)).reshape(n, l, HG_HEADS, HG_EXPAND)
    f_pre = f_pre.astype(f32).reshape(n, l, HG_HEADS, HG_EXPAND)
    lbh = lb.astype(f32).reshape(HG_HEADS, HG_EXPAND)
    sig = jax.nn.sigmoid(f_pre)
    f = lbh + (1.0 - lbh) * sig
    log_f = jnp.log(jnp.maximum(f, F_FLOOR))
    k = (1.0 - lbh) * (1.0 - sig)
    v = i_v.astype(f32).reshape(n, l, HG_HEADS, HG_VDIM)
    o, new_h = hgrn2_chunked(q, k, v, log_f, st_h.astype(f32))
    o = o * lax.rsqrt(jnp.mean(o * o, axis=-1, keepdims=True) + EPS)
    o = o * hg_g.astype(f32).reshape(HG_HEADS, HG_VDIM)
    y_h = (o.reshape(n, l, HG_W) * jax.nn.silu(og.astype(f32))).astype(x.dtype)

    u = sc_c * sc_h
    conv_s, new_s = causal_dwconv(u, st_s, sc_w)
    y_s = sc_b * conv_s

    a = cc_v * jax.nn.sigmoid(cc_g)
    conv_c, new_c = causal_dwconv(a, st_c, cc_w)
    conv_c = conv_c + cc_b.astype(conv_c.dtype)
    y_c = jax.nn.silu(layernorm(conv_c, cc_lg, cc_lb))

    mix = jnp.concatenate([y_h, y_s.astype(x.dtype), y_c.astype(x.dtype)], axis=-1)
    x = x + mix @ w_out.astype(x.dtype)

    h2 = rmsnorm(x, g_mlp)
    x = x + jnp.square(jax.nn.relu(h2 @ w_up.astype(h2.dtype))) @ w_down.astype(h2.dtype)
    return x, new_h, new_s, new_c


def run_trunk(x, st_h, st_s, st_c, lb_all, g_mix, w_in, hgrn_norm_g, sconv_w, cconv_w, cconv_b,
              cconv_ln_g, cconv_ln_b, w_out, g_mlp, w_up, w_down, g_final):
    nh, ns, nc = [], [], []
    for li in range(DEPTH):
        x, sh, ss, sc = trunk_layer(x, st_h[li], st_s[li], st_c[li], lb_all[li], g_mix[li], w_in[li],
                                    hgrn_norm_g[li], sconv_w[li], cconv_w[li], cconv_b[li],
                                    cconv_ln_g[li], cconv_ln_b[li], w_out[li], g_mlp[li],
                                    w_up[li], w_down[li])
        nh.append(sh)
        ns.append(ss)
        nc.append(sc)
    return rmsnorm(x, g_final), jnp.stack(nh), jnp.stack(ns), jnp.stack(nc)


def setup_inputs(seed: int = 0) -> dict:
    key = jax.random.key(seed)
    ks = jax.random.split(key, 20)

    def nrm(k, shape, s):
        return jax.random.normal(k, shape, jnp.float32) * s

    return {
        'x_prompt': nrm(ks[0], (BATCH, SEQ, D_MODEL), 1.0),
        'x_sample': nrm(ks[1], (DEC_BATCH, DEC_SEQ, D_MODEL), 1.0),
        'state_hgrn': nrm(ks[2], (DEPTH, DEC_BATCH, HG_HEADS, HG_EXPAND, HG_VDIM), 0.5),
        'state_sconv': nrm(ks[3], (DEPTH, DEC_BATCH, SC_WIDTH - 1, SC_W), 1.0),
        'state_cconv': nrm(ks[4], (DEPTH, DEC_BATCH, CC_WIDTH - 1, CC_W), 0.7),
        'g_mix': 1.0 + nrm(ks[5], (DEPTH, D_MODEL), 0.02),
        'w_in': nrm(ks[6], (DEPTH, D_MODEL, IN_COLS), D_MODEL ** -0.5),
        'hgrn_lb': nrm(ks[7], (DEPTH, HG_FDIM), 1.0),
        'hgrn_norm_g': 1.0 + nrm(ks[8], (DEPTH, HG_W), 0.02),
        'sconv_w': nrm(ks[9], (DEPTH, SC_WIDTH, SC_W), SC_WIDTH ** -0.5),
        'cconv_w': nrm(ks[10], (DEPTH, CC_WIDTH, CC_W), CC_WIDTH ** -0.5),
        'cconv_b': nrm(ks[11], (DEPTH, CC_W), 0.02),
        'cconv_ln_g': 1.0 + nrm(ks[12], (DEPTH, CC_W), 0.02),
        'cconv_ln_b': nrm(ks[13], (DEPTH, CC_W), 0.02),
        'w_out': nrm(ks[14], (DEPTH, MIX_W, D_MODEL), MIX_W ** -0.5),
        'g_mlp': 1.0 + nrm(ks[15], (DEPTH, D_MODEL), 0.02),
        'w_up': nrm(ks[16], (DEPTH, D_MODEL, D_FF), D_MODEL ** -0.5),
        'w_down': nrm(ks[17], (DEPTH, D_FF, D_MODEL), D_FF ** -0.5),
        'g_final': 1.0 + nrm(ks[18], (D_MODEL,), 0.02),
    }


def reference(x_prompt, x_sample, state_hgrn, state_sconv, state_cconv, g_mix, w_in, hgrn_lb,
              hgrn_norm_g, sconv_w, cconv_w, cconv_b, cconv_ln_g, cconv_ln_b, w_out, g_mlp,
              w_up, w_down, g_final):
    p = jax.nn.softmax(hgrn_lb.astype(jnp.float32), axis=0)
    lb_all = jnp.cumsum(p, axis=0) - p[0:1]

    zh = jnp.zeros((DEPTH, x_prompt.shape[0], HG_HEADS, HG_EXPAND, HG_VDIM), jnp.float32)
    zs = jnp.zeros((DEPTH, x_prompt.shape[0], SC_WIDTH - 1, SC_W), x_prompt.dtype)
    zc = jnp.zeros((DEPTH, x_prompt.shape[0], CC_WIDTH - 1, CC_W), x_prompt.dtype)
    y_prompt, ph, ps, pc = run_trunk(x_prompt, zh, zs, zc, lb_all, g_mix, w_in, hgrn_norm_g,
                                     sconv_w, cconv_w, cconv_b, cconv_ln_g, cconv_ln_b, w_out,
                                     g_mlp, w_up, w_down, g_final)
    y_sample, sh, ss, sc = run_trunk(x_sample, state_hgrn, state_sconv, state_cconv, lb_all, g_mix,
                                     w_in, hgrn_norm_g, sconv_w, cconv_w, cconv_b, cconv_ln_g,
                                     cconv_ln_b, w_out, g_mlp, w_up, w_down, g_final)
    return (y_prompt, y_sample, ph, ps, pc, sh, ss, sc)
```

```python
import functools

import numpy as np
import jax
import jax.numpy as jnp
from jax import lax
from jax.experimental import pallas as pl
from jax.experimental.pallas import tpu as pltpu

F32 = jnp.float32
BF16 = jnp.bfloat16

EPS = 1e-6
F_FLOOR = 1e-30

LANES = 128
SUBLANES = 8
VMEM_LIMIT_BYTES = 56 * 1024 * 1024

HG_HEADS = 8
HG_DIM = 128
SC_TAPS = 3
CC_TAPS = 31
CHUNK = 128
N_LEVELS = 7
SAMPLE_TOKENS_PER_STEP = 8
CONV_TIME_BLOCK = 512
CONV_ROWS = 32


def _params(*sem):
    return pltpu.CompilerParams(dimension_semantics=sem, vmem_limit_bytes=VMEM_LIMIT_BYTES)


def _rmsnorm(x, g):
    return x * lax.rsqrt(jnp.mean(x * x, axis=-1, keepdims=True) + EPS) * g


def _norm_matmul_kernel(x_ref, g_ref, w_ref, o_ref, h_ref):
    @pl.when(pl.program_id(1) == 0)
    def _():
        h_ref[...] = _rmsnorm(x_ref[...], g_ref[...]).astype(BF16)

    o_ref[...] = jnp.dot(h_ref[...], w_ref[...], preferred_element_type=F32)


def _norm_matmul(x, g, w, *, tm, tn):
    m, k = x.shape
    n = w.shape[1]
    return pl.pallas_call(
        _norm_matmul_kernel,
        grid=(m // tm, n // tn),
        in_specs=[pl.BlockSpec((tm, k), lambda i, j: (i, 0)),
                  pl.BlockSpec((1, k), lambda i, j: (0, 0)),
                  pl.BlockSpec((k, tn), lambda i, j: (0, j))],
        out_specs=pl.BlockSpec((tm, tn), lambda i, j: (i, j)),
        out_shape=jax.ShapeDtypeStruct((m, n), F32),
        scratch_shapes=[pltpu.VMEM((tm, k), BF16)],
        compiler_params=_params("parallel", "arbitrary"),
        name="in_proj",
    )(x, g, w)


def _out_proj_kernel(x_ref, mh_ref, ms_ref, mc_ref, wh_ref, ws_ref, wc_ref, o_ref):
    acc = x_ref[...]
    acc += jnp.dot(mh_ref[...], wh_ref[...], preferred_element_type=F32)
    acc += jnp.dot(ms_ref[...], ws_ref[...], preferred_element_type=F32)
    acc += jnp.dot(mc_ref[...], wc_ref[...], preferred_element_type=F32)
    o_ref[...] = acc


def _out_proj(x, mh, ms, mc, wh, ws, wc, *, tm):
    m, d = x.shape
    row = lambda a: pl.BlockSpec((tm, a.shape[1]), lambda i: (i, 0))
    whole = lambda a: pl.BlockSpec(a.shape, lambda i: (0, 0))
    return pl.pallas_call(
        _out_proj_kernel,
        grid=(m // tm,),
        in_specs=[row(x), row(mh), row(ms), row(mc), whole(wh), whole(ws), whole(wc)],
        out_specs=row(x),
        out_shape=jax.ShapeDtypeStruct((m, d), F32),
        compiler_params=_params("parallel"),
        name="out_proj",
    )(x, mh, ms, mc, wh, ws, wc)


def _ffn_kernel(x_ref, g_ref, wu_ref, wd_ref, gf_ref, o_ref, h_ref, *, final_norm):
    j = pl.program_id(1)

    @pl.when(j == 0)
    def _():
        x = x_ref[...]
        h_ref[...] = _rmsnorm(x, g_ref[...]).astype(BF16)
        o_ref[...] = x

    u = jnp.dot(h_ref[...], wu_ref[...], preferred_element_type=F32)
    r = jnp.maximum(u, 0.0)
    o_ref[...] += jnp.dot((r * r).astype(BF16), wd_ref[...], preferred_element_type=F32)

    if final_norm:
        @pl.when(j == pl.num_programs(1) - 1)
        def _():
            o_ref[...] = _rmsnorm(o_ref[...], gf_ref[...])


def _ffn(x, g, wu, wd, gf, *, tm, tf, final_norm):
    m, d = x.shape
    dff = wu.shape[1]
    return pl.pallas_call(
        functools.partial(_ffn_kernel, final_norm=final_norm),
        grid=(m // tm, dff // tf),
        in_specs=[pl.BlockSpec((tm, d), lambda i, j: (i, 0)),
                  pl.BlockSpec((1, d), lambda i, j: (0, 0)),
                  pl.BlockSpec((d, tf), lambda i, j: (0, j)),
                  pl.BlockSpec((tf, d), lambda i, j: (j, 0)),
                  pl.BlockSpec((1, d), lambda i, j: (0, 0))],
        out_specs=pl.BlockSpec((tm, d), lambda i, j: (i, 0)),
        out_shape=jax.ShapeDtypeStruct((m, d), F32),
        scratch_shapes=[pltpu.VMEM((tm, d), BF16)],
        compiler_params=_params("parallel", "arbitrary"),
        name="ffn",
    )(x, g, wu, wd, gf)


def _decay_sum_matrix():
    c = CHUNK
    t = np.arange(c)[:, None]
    r = np.arange(c)[None, :]
    blocks = [r <= t, r > t]
    for lvl in range(N_LEVELS):
        h = 1 << lvl
        ref = (t // (2 * h)) * (2 * h) + h - 1
        upper = ((t >> lvl) & 1) == 1
        blocks.append(np.where(upper, (r > ref) & (r <= t), (r > t) & (r <= ref)))
    return np.concatenate(blocks, axis=0).astype(np.float32)


def _pair_level_matrix():
    c = CHUNK
    t = np.arange(c)[:, None]
    s = np.arange(c)[None, :]
    x = np.maximum(t ^ s, 1)
    lvl = np.floor(np.log2(x)).astype(np.int32)
    return np.where(s < t, lvl, np.where(s == t, N_LEVELS, -1)).astype(np.int32)


def _hgrn_prompt_kernel(zq_ref, zf_ref, zi_ref, zo_ref, lb_ref, gn_ref, dsm_ref, lvl_ref,
                        y_ref, s_ref):
    n_chunks = zq_ref.shape[0] // CHUNK
    lb = lb_ref[...]
    gn = gn_ref[...]
    dsm = dsm_ref[...]
    lvl = lvl_ref[...]
    row = lax.broadcasted_iota(jnp.int32, (CHUNK, HG_DIM), 0)

    def chunk(c, st):
        rows = pl.ds(pl.multiple_of(c * CHUNK, CHUNK), CHUNK)
        zq = zq_ref[rows, :]
        q = zq * jax.nn.sigmoid(zq)
        sig = jax.nn.sigmoid(zf_ref[rows, :])
        f = lb + (1.0 - lb) * sig
        g = jnp.log(jnp.maximum(f, F_FLOOR))
        kk = (1.0 - lb) * (1.0 - sig)
        v = zi_ref[rows, :].astype(BF16)

        g_hi = g.astype(BF16)
        g_lo = (g - g_hi.astype(F32)).astype(BF16)
        sums = jnp.dot(dsm, jnp.concatenate([g_hi, g_lo], axis=1), preferred_element_type=F32)
        sums = jnp.minimum(sums[:, :HG_DIM] + sums[:, HG_DIM:], 0.0)

        b = sums[0:CHUNK]
        qd = (q * jnp.exp(b)).astype(BF16)
        kd = (kk * jnp.exp(sums[CHUNK:2 * CHUNK])).astype(BF16)

        att = jnp.where(lvl == N_LEVELS, jnp.sum(q * kk, axis=1, keepdims=True), 0.0)
        for l in range(N_LEVELS):
            dec = jnp.exp(sums[(2 + l) * CHUNK:(3 + l) * CHUNK])
            upper = ((row >> l) & 1) == 1
            x = (dec * jnp.where(upper, q, kk)).astype(BF16)
            att_l = lax.dot_general(x, x, (((1,), (1,)), ((), ())), preferred_element_type=F32)
            att = jnp.where(lvl == l, att_l, att)

        o = jnp.dot(att.astype(BF16), v, preferred_element_type=F32)
        o += lax.dot_general(qd, st.astype(BF16), (((1,), (1,)), ((), ())),
                             preferred_element_type=F32)
        st = st * jnp.exp(b[CHUNK - 1:CHUNK, :]) + lax.dot_general(
            v, kd, (((0,), (0,)), ((), ())), preferred_element_type=F32)

        o = o * lax.rsqrt(jnp.mean(o * o, axis=-1, keepdims=True) + EPS) * gn
        zo = zo_ref[rows, :]
        y_ref[rows, :] = (o * (zo * jax.nn.sigmoid(zo))).astype(y_ref.dtype)
        return st

    st = lax.fori_loop(0, n_chunks, chunk, jnp.zeros((HG_DIM, HG_DIM), F32))
    s_ref[0, 0] = st.T


def _hgrn_prompt(z, lb, gn, *, batch, seq):
    hw = HG_HEADS * HG_DIM
    col = lambda off: pl.BlockSpec((seq, HG_DIM), lambda n, h, off=off: (n, off + h))
    per_head = pl.BlockSpec((1, HG_DIM), lambda n, h: (0, h))
    dsm = jnp.asarray(_decay_sum_matrix(), BF16)
    lvl = jnp.asarray(_pair_level_matrix())
    return pl.pallas_call(
        _hgrn_prompt_kernel,
        grid=(batch, HG_HEADS),
        in_specs=[col(0), col(HG_HEADS), col(2 * HG_HEADS), col(3 * HG_HEADS), per_head, per_head,
                  pl.BlockSpec(dsm.shape, lambda n, h: (0, 0)),
                  pl.BlockSpec(lvl.shape, lambda n, h: (0, 0))],
        out_specs=[pl.BlockSpec((seq, HG_DIM), lambda n, h: (n, h)),
                   pl.BlockSpec((1, 1, HG_DIM, HG_DIM), lambda n, h: (n, h, 0, 0))],
        out_shape=[jax.ShapeDtypeStruct((batch * seq, hw), BF16),
                   jax.ShapeDtypeStruct((batch, HG_HEADS, HG_DIM, HG_DIM), F32)],
        compiler_params=_params("parallel", "parallel"),
        name="hgrn_prompt",
    )(z, z, z, z, lb, gn, dsm, lvl)


def _conv_prompt_kernel(zb_ref, zc_ref, zh_ref, zv_ref, zg_ref, scw_ref, ccw_ref, ccb_ref,
                        lng_ref, lnb_ref, ys_ref, yc_ref, ns_ref, nc_ref, u_ref, a_ref):
    tb = pl.program_id(1)
    t_blk = zb_ref.shape[0]
    u_halo = SUBLANES
    a_halo = CONV_ROWS

    @pl.when(tb == 0)
    def _():
        u_ref[0:u_halo, :] = jnp.zeros((u_halo, u_ref.shape[1]), F32)
        a_ref[0:a_halo, :] = jnp.zeros((a_halo, a_ref.shape[1]), F32)

    @pl.when(tb > 0)
    def _():
        u_ref[0:u_halo, :] = u_ref[t_blk:t_blk + u_halo, :]
        a_ref[0:a_halo, :] = a_ref[t_blk:t_blk + a_halo, :]

    scw = scw_ref[...]
    ccw = ccw_ref[...]
    ccb = ccb_ref[...]
    lng = lng_ref[...]
    lnb = lnb_ref[...]

    def fill(i, carry):
        rows = pl.ds(pl.multiple_of(i * CONV_ROWS, CONV_ROWS), CONV_ROWS)
        u_ref[pl.ds(pl.multiple_of(u_halo + i * CONV_ROWS, SUBLANES), CONV_ROWS), :] = (
            zc_ref[rows, :] * zh_ref[rows, :])
        a_ref[pl.ds(pl.multiple_of(a_halo + i * CONV_ROWS, CONV_ROWS), CONV_ROWS), :] = (
            zv_ref[rows, :] * jax.nn.sigmoid(zg_ref[rows, :]))
        return carry

    lax.fori_loop(0, t_blk // CONV_ROWS, fill, 0)

    def conv(i, carry):
        base = i * CONV_ROWS
        rows = pl.ds(pl.multiple_of(base, CONV_ROWS), CONV_ROWS)
        uw = u_ref[pl.ds(pl.multiple_of(base, CONV_ROWS), u_halo + CONV_ROWS), :]
        acc = scw[SC_TAPS - 1:SC_TAPS, :] * uw[u_halo:, :]
        for j in range(SC_TAPS - 1):
            off = u_halo - (SC_TAPS - 1) + j
            acc += scw[j:j + 1, :] * uw[off:off + CONV_ROWS, :]
        ys_ref[rows, :] = (zb_ref[rows, :] * acc).astype(ys_ref.dtype)

        aw = a_ref[pl.ds(pl.multiple_of(base, CONV_ROWS), a_halo + CONV_ROWS), :]
        acc = ccb + ccw[CC_TAPS - 1:CC_TAPS, :] * aw[a_halo:, :]
        for j in range(CC_TAPS - 1):
            off = a_halo - (CC_TAPS - 1) + j
            acc += ccw[j:j + 1, :] * aw[off:off + CONV_ROWS, :]
        mu = jnp.mean(acc, axis=-1, keepdims=True)
        d = acc - mu
        var = jnp.mean(d * d, axis=-1, keepdims=True)
        y = d * lax.rsqrt(var + EPS) * lng + lnb
        yc_ref[rows, :] = (y * jax.nn.sigmoid(y)).astype(yc_ref.dtype)
        return carry

    lax.fori_loop(0, t_blk // CONV_ROWS, conv, 0)

    @pl.when(tb == pl.num_programs(1) - 1)
    def _():
        ns_ref[0] = u_ref[u_halo + t_blk - (SC_TAPS - 1):u_halo + t_blk, :]
        nc_ref[0] = a_ref[a_halo + t_blk - (CC_TAPS - 1):a_halo + t_blk, :]


def _conv_prompt(z, scw, ccw, ccb, lng, lnb, *, batch, seq, sc_off, cc_off):
    w = scw.shape[1]
    t_blk = CONV_TIME_BLOCK
    nt = seq // t_blk
    col = lambda off: pl.BlockSpec((t_blk, w), lambda n, t, off=off: (n * nt + t, off))
    whole = lambda a: pl.BlockSpec(a.shape, lambda n, t: (0, 0))
    out_rows = pl.BlockSpec((t_blk, w), lambda n, t: (n * nt + t, 0))
    return pl.pallas_call(
        _conv_prompt_kernel,
        grid=(batch, nt),
        in_specs=[col(sc_off), col(sc_off + 1), col(sc_off + 2), col(cc_off), col(cc_off + 1),
                  whole(scw), whole(ccw), whole(ccb), whole(lng), whole(lnb)],
        out_specs=[out_rows, out_rows,
                   pl.BlockSpec((1, SC_TAPS - 1, w), lambda n, t: (n, 0, 0)),
                   pl.BlockSpec((1, CC_TAPS - 1, w), lambda n, t: (n, 0, 0))],
        out_shape=[jax.ShapeDtypeStruct((batch * seq, w), BF16),
                   jax.ShapeDtypeStruct((batch * seq, w), BF16),
                   jax.ShapeDtypeStruct((batch, SC_TAPS - 1, w), F32),
                   jax.ShapeDtypeStruct((batch, CC_TAPS - 1, w), F32)],
        scratch_shapes=[pltpu.VMEM((SUBLANES + t_blk, w), F32),
                        pltpu.VMEM((CONV_ROWS + t_blk, w), F32)],
        compiler_params=_params("parallel", "arbitrary"),
        name="conv_prompt",
    )(z, z, z, z, z, scw, ccw, ccb, lng, lnb)


def _mix_sample_kernel(z_ref, st_ref, ss_ref, sc_ref, lb_ref, gn_ref, scw_ref, ccw_ref, ccb_ref,
                       lng_ref, lnb_ref, mh_ref, ms_ref, mc_ref, so_ref, ns_ref, nc_ref, oh_ref):
    hw = HG_HEADS * HG_DIM
    w = scw_ref.shape[1]
    eye = (lax.broadcasted_iota(jnp.int32, (HG_DIM, HG_DIM), 0)
           == lax.broadcasted_iota(jnp.int32, (HG_DIM, HG_DIM), 1))

    def column(x_row):
        return jnp.sum(jnp.where(eye, x_row, 0.0), axis=1, keepdims=True)

    for t in range(z_ref.shape[0]):
        tr = slice(t, t + 1)
        for h in range(HG_HEADS):
            cols = slice(h * HG_DIM, (h + 1) * HG_DIM)
            lb = lb_ref[:, cols]
            zq = z_ref[tr, cols]
            q = zq * jax.nn.sigmoid(zq)
            sig = jax.nn.sigmoid(z_ref[tr, hw + h * HG_DIM:hw + (h + 1) * HG_DIM])
            f = lb + (1.0 - lb) * sig
            g = jnp.log(jnp.maximum(f, F_FLOOR))
            kk = (1.0 - lb) * (1.0 - sig)
            v = z_ref[tr, 2 * hw + h * HG_DIM:2 * hw + (h + 1) * HG_DIM]
            zo = z_ref[tr, 3 * hw + h * HG_DIM:3 * hw + (h + 1) * HG_DIM]
            s_new = column(jnp.exp(g)) * st_ref[t, h] + column(kk) * v
            so_ref[t, h] = s_new
            o = jnp.sum(column(q) * s_new, axis=0, keepdims=True)
            o = o * lax.rsqrt(jnp.mean(o * o, axis=-1, keepdims=True) + EPS) * gn_ref[:, cols]
            oh_ref[tr, cols] = o * (zo * jax.nn.sigmoid(zo))
    mh_ref[...] = oh_ref[...].astype(mh_ref.dtype)

    off = 4 * hw
    zb = z_ref[:, off:off + w]
    u = z_ref[:, off + w:off + 2 * w] * z_ref[:, off + 2 * w:off + 3 * w]
    conv = scw_ref[SC_TAPS - 1:SC_TAPS, :] * u
    for j in range(SC_TAPS - 1):
        conv += scw_ref[j:j + 1, :] * ss_ref[:, j * w:(j + 1) * w]
    ms_ref[...] = (zb * conv).astype(ms_ref.dtype)
    for j in range(SC_TAPS - 2):
        ns_ref[:, j * w:(j + 1) * w] = ss_ref[:, (j + 1) * w:(j + 2) * w]
    ns_ref[:, (SC_TAPS - 2) * w:] = u

    off = 4 * hw + 3 * w
    a = z_ref[:, off:off + w] * jax.nn.sigmoid(z_ref[:, off + w:off + 2 * w])
    conv = ccb_ref[...] + ccw_ref[CC_TAPS - 1:CC_TAPS, :] * a
    for j in range(CC_TAPS - 1):
        conv += ccw_ref[j:j + 1, :] * sc_ref[:, j * w:(j + 1) * w]
    mu = jnp.mean(conv, axis=-1, keepdims=True)
    d = conv - mu
    var = jnp.mean(d * d, axis=-1, keepdims=True)
    y = d * lax.rsqrt(var + EPS) * lng_ref[...] + lnb_ref[...]
    mc_ref[...] = (y * jax.nn.sigmoid(y)).astype(mc_ref.dtype)
    nc_ref[:, :(CC_TAPS - 2) * w] = sc_ref[:, w:]
    nc_ref[:, (CC_TAPS - 2) * w:] = a


def _mix_sample(z, st_h, st_s, st_c, lb, gn, scw, ccw, ccb, lng, lnb, *, layer):
    nb = z.shape[0]
    tb = SAMPLE_TOKENS_PER_STEP
    hw = HG_HEADS * HG_DIM
    w = scw.shape[1]
    rows = lambda width: pl.BlockSpec((tb, width), lambda i: (i, 0))
    layer_rows = lambda width: pl.BlockSpec((None, tb, width), lambda i: (layer, i, 0))
    whole = lambda a: pl.BlockSpec(a.shape, lambda i: (0, 0))
    return pl.pallas_call(
        _mix_sample_kernel,
        grid=(nb // tb,),
        in_specs=[rows(z.shape[1]),
                  pl.BlockSpec((None, tb, HG_HEADS, HG_DIM, HG_DIM), lambda i: (layer, i, 0, 0, 0)),
                  layer_rows(st_s.shape[2]), layer_rows(st_c.shape[2]),
                  whole(lb), whole(gn), whole(scw), whole(ccw), whole(ccb), whole(lng), whole(lnb)],
        out_specs=[rows(hw), rows(w), rows(w),
                   pl.BlockSpec((tb, HG_HEADS, HG_DIM, HG_DIM), lambda i: (i, 0, 0, 0)),
                   rows(st_s.shape[2]), rows(st_c.shape[2])],
        out_shape=[jax.ShapeDtypeStruct((nb, hw), BF16),
                   jax.ShapeDtypeStruct((nb, w), BF16),
                   jax.ShapeDtypeStruct((nb, w), BF16),
                   jax.ShapeDtypeStruct((nb, HG_HEADS, HG_DIM, HG_DIM), F32),
                   jax.ShapeDtypeStruct((nb, st_s.shape[2]), F32),
                   jax.ShapeDtypeStruct((nb, st_c.shape[2]), F32)],
        scratch_shapes=[pltpu.VMEM((tb, hw), F32)],
        compiler_params=_params("parallel"),
        name="mix_sample",
    )(z, st_h, st_s, st_c, lb, gn, scw, ccw, ccb, lng, lnb)


def kernel(x_prompt, x_sample, state_hgrn, state_sconv, state_cconv, g_mix, w_in, hgrn_lb,
           hgrn_norm_g, sconv_w, cconv_w, cconv_b, cconv_ln_g, cconv_ln_b, w_out, g_mlp,
           w_up, w_down, g_final):
    batch, seq, d = x_prompt.shape
    nb = x_sample.shape[0]
    depth = w_in.shape[0]
    hw = HG_HEADS * HG_DIM
    w = sconv_w.shape[2]
    assert state_hgrn.shape[2:] == (HG_HEADS, HG_DIM, HG_DIM)
    assert w_in.shape[2] == 4 * hw + 5 * w and w_out.shape[1] == hw + 2 * w
    assert sconv_w.shape[1] == SC_TAPS and cconv_w.shape[1] == CC_TAPS
    assert seq % CONV_TIME_BLOCK == 0 and seq % CHUNK == 0 and nb % SAMPLE_TOKENS_PER_STEP == 0

    p = jax.nn.softmax(hgrn_lb.astype(F32), axis=0)
    lb_all = jnp.cumsum(p, axis=0) - p[0:1]

    w_in_b = w_in.astype(BF16)
    w_out_b = w_out.astype(BF16)
    w_up_b = w_up.astype(BF16)
    w_down_b = w_down.astype(BF16)
    row = lambda a: a.reshape(1, -1)

    xp = x_prompt.reshape(batch * seq, d)
    xs = x_sample.reshape(nb, d)
    st_s = state_sconv.reshape(depth, nb, (SC_TAPS - 1) * w)
    st_c = state_cconv.reshape(depth, nb, (CC_TAPS - 1) * w)
    sc_off = 4 * hw // w
    cc_off = sc_off + 3

    ph, ps, pc, sh, ss, sc = [], [], [], [], [], []
    for li in range(depth):
        last = li == depth - 1
        g1, g2, gf = row(g_mix[li]), row(g_mlp[li]), row(g_final)
        lb, gn = row(lb_all[li]), row(hgrn_norm_g[li])
        ccb, lng, lnb = row(cconv_b[li]), row(cconv_ln_g[li]), row(cconv_ln_b[li])
        wo = w_out_b[li]
        wh, ws, wc = wo[:hw], wo[hw:hw + w], wo[hw + w:]

        z = _norm_matmul(xp, g1, w_in_b[li], tm=1024, tn=512)
        mh, new_h = _hgrn_prompt(z, lb, gn, batch=batch, seq=seq)
        ms, mc, new_s, new_c = _conv_prompt(z, sconv_w[li], cconv_w[li], ccb, lng, lnb,
                                            batch=batch, seq=seq, sc_off=sc_off, cc_off=cc_off)
        xp = _out_proj(xp, mh, ms, mc, wh, ws, wc, tm=512)
        xp = _ffn(xp, g2, w_up_b[li], w_down_b[li], gf, tm=1024, tf=512, final_norm=last)
        ph.append(new_h)
        ps.append(new_s)
        pc.append(new_c)

        z = _norm_matmul(xs, g1, w_in_b[li], tm=nb, tn=512)
        mh, ms, mc, new_h, new_s, new_c = _mix_sample(
            z, state_hgrn, st_s, st_c, lb, gn, sconv_w[li], cconv_w[li], ccb, lng, lnb, layer=li)
        xs = _out_proj(xs, mh, ms, mc, wh, ws, wc, tm=nb)
        xs = _ffn(xs, g2, w_up_b[li], w_down_b[li], gf, tm=nb, tf=1024, final_norm=last)
        sh.append(new_h)
        ss.append(new_s.reshape(nb, SC_TAPS - 1, w))
        sc.append(new_c.reshape(nb, CC_TAPS - 1, w))

    return (xp.reshape(batch, seq, d), xs.reshape(nb, 1, d), jnp.stack(ph), jnp.stack(ps),
            jnp.stack(pc), jnp.stack(sh), jnp.stack(ss), jnp.stack(sc))
```

```python
import functools

import numpy as np
import jax
import jax.numpy as jnp
from jax import lax
from jax.experimental import pallas as pl
from jax.experimental.pallas import tpu as pltpu

F32 = jnp.float32
BF16 = jnp.bfloat16

EPS = 1e-6
F_FLOOR = 1e-30

LANES = 128
SUBLANES = 8
VMEM_LIMIT_BYTES = 56 * 1024 * 1024

HG_HEADS = 8
HG_DIM = 128
SC_TAPS = 3
CC_TAPS = 31
CHUNK = 128
N_LEVELS = 7
SAMPLE_TOKENS_PER_STEP = 8
CONV_TIME_BLOCK = 512
CONV_ROWS = 32


def _params(*sem):
    return pltpu.CompilerParams(dimension_semantics=sem, vmem_limit_bytes=VMEM_LIMIT_BYTES)


def _layer_block(a, layer):
    zeros = (0,) * (a.ndim - 1)
    return pl.BlockSpec((None,) + a.shape[1:], lambda *_: (layer,) + zeros)


def _rmsnorm(x, g):
    return x * lax.rsqrt(jnp.mean(x * x, axis=-1, keepdims=True) + EPS) * g


def _sigmoid(x):
    return 0.5 * jnp.tanh(0.5 * x) + 0.5


def _silu(x):
    h = 0.5 * x
    return h + h * jnp.tanh(h)


def _gate_constants(lb):
    f_scale = 0.5 * (1.0 - lb)
    return f_scale, lb + f_scale


def _forget_and_key(z, f_scale, f_shift):
    w = f_scale * jnp.tanh(0.5 * z)
    return f_shift + w, f_scale - w


def _norm_matmul_kernel(x_ref, g_ref, w_ref, o_ref, h_ref):
    @pl.when(pl.program_id(1) == 0)
    def _():
        h_ref[...] = _rmsnorm(x_ref[...], g_ref[...]).astype(BF16)

    o_ref[...] = jnp.dot(h_ref[...], w_ref[...], preferred_element_type=F32)


def _norm_matmul(x, g, w, *, layer, tm, tn):
    m, k = x.shape
    n = w.shape[2]
    return pl.pallas_call(
        _norm_matmul_kernel,
        grid=(m // tm, n // tn),
        in_specs=[pl.BlockSpec((tm, k), lambda i, j: (i, 0)),
                  _layer_block(g, layer),
                  pl.BlockSpec((None, k, tn), lambda i, j: (layer, 0, j))],
        out_specs=pl.BlockSpec((tm, tn), lambda i, j: (i, j)),
        out_shape=jax.ShapeDtypeStruct((m, n), F32),
        scratch_shapes=[pltpu.VMEM((tm, k), BF16)],
        compiler_params=_params("parallel", "arbitrary"),
        name="in_proj",
    )(x, g, w)


def _out_proj_kernel(x_ref, mh_ref, ms_ref, mc_ref, wh_ref, ws_ref, wc_ref, o_ref):
    acc = x_ref[...]
    acc += jnp.dot(mh_ref[...], wh_ref[...], preferred_element_type=F32)
    acc += jnp.dot(ms_ref[...], ws_ref[...], preferred_element_type=F32)
    acc += jnp.dot(mc_ref[...], wc_ref[...], preferred_element_type=F32)
    o_ref[...] = acc


def _out_proj(x, mh, ms, mc, w_out, *, layer, tm):
    m, d = x.shape
    hw, w = mh.shape[1], ms.shape[1]
    row = lambda a: pl.BlockSpec((tm, a.shape[1]), lambda i: (i, 0))
    w_rows = lambda rows, blk: pl.BlockSpec((None, rows, d), lambda i: (layer, blk, 0))
    return pl.pallas_call(
        _out_proj_kernel,
        grid=(m // tm,),
        in_specs=[row(x), row(mh), row(ms), row(mc),
                  w_rows(hw, 0), w_rows(w, hw // w), w_rows(w, hw // w + 1)],
        out_specs=row(x),
        out_shape=jax.ShapeDtypeStruct((m, d), F32),
        compiler_params=_params("parallel"),
        name="out_proj",
    )(x, mh, ms, mc, w_out, w_out, w_out)


def _ffn_kernel(x_ref, g_ref, wu_ref, wd_ref, gf_ref, o_ref, h_ref, *, final_norm):
    j = pl.program_id(1)

    @pl.when(j == 0)
    def _():
        x = x_ref[...]
        h_ref[...] = _rmsnorm(x, g_ref[...]).astype(BF16)
        o_ref[...] = x

    u = jnp.dot(h_ref[...], wu_ref[...], preferred_element_type=F32)
    r = jnp.maximum(u, 0.0)
    o_ref[...] += jnp.dot((r * r).astype(BF16), wd_ref[...], preferred_element_type=F32)

    if final_norm:
        @pl.when(j == pl.num_programs(1) - 1)
        def _():
            o_ref[...] = _rmsnorm(o_ref[...], gf_ref[...])


def _ffn(x, g, wu, wd, gf, *, layer, tm, tf, final_norm):
    m, d = x.shape
    dff = wu.shape[2]
    return pl.pallas_call(
        functools.partial(_ffn_kernel, final_norm=final_norm),
        grid=(m // tm, dff // tf),
        in_specs=[pl.BlockSpec((tm, d), lambda i, j: (i, 0)),
                  _layer_block(g, layer),
                  pl.BlockSpec((None, d, tf), lambda i, j: (layer, 0, j)),
                  pl.BlockSpec((None, tf, d), lambda i, j: (layer, j, 0)),
                  pl.BlockSpec((1, d), lambda i, j: (0, 0))],
        out_specs=pl.BlockSpec((tm, d), lambda i, j: (i, 0)),
        out_shape=jax.ShapeDtypeStruct((m, d), F32),
        scratch_shapes=[pltpu.VMEM((tm, d), BF16)],
        compiler_params=_params("parallel", "arbitrary"),
        name="ffn",
    )(x, g, wu, wd, gf)


LOG2E = 1.4426950408889634
LOW_LEVELS = 3


def _prefix_sum_matrix():
    t = np.arange(CHUNK)[:, None]
    r = np.arange(CHUNK)[None, :]
    return (r <= t).astype(np.float32)


def _hgrn_prompt_kernel(zq_ref, zf_ref, zi_ref, zo_ref, lb_ref, gn_ref, tri_ref, y_ref, s_ref):
    n_chunks = zq_ref.shape[0] // CHUNK
    n_tiles = CHUNK // SUBLANES
    lb = lb_ref[...]
    gn = gn_ref[...]
    tri = tri_ref[...]
    f_scale, f_shift = _gate_constants(lb)
    sub = lax.broadcasted_iota(jnp.int32, (SUBLANES, HG_DIM), 0)
    lane = lax.broadcasted_iota(jnp.int32, (SUBLANES, HG_DIM), 1)
    col = lane & (SUBLANES - 1)
    below = col < sub
    pat_low = [below & (((sub ^ col) >> l) == 1) for l in range(LOW_LEVELS)]
    pat_diag = col == sub
    upper_low = [((sub >> l) & 1) == 1 for l in range(LOW_LEVELS)]
    nt = (((1,), (1,)), ((), ()))
    tiled = lambda x: x.reshape(n_tiles, SUBLANES, HG_DIM)
    flat = lambda x: x.reshape(CHUNK, HG_DIM)

    def tile_row(x3, s):
        return jnp.broadcast_to(x3[:, s:s + 1, :], x3.shape)

    def chunk(c, st):
        rows = pl.ds(pl.multiple_of(c * CHUNK, CHUNK), CHUNK)
        q = _silu(zq_ref[rows, :])
        f, kk = _forget_and_key(zf_ref[rows, :], f_scale, f_shift)
        g = jnp.log(jnp.maximum(f, F_FLOOR)) * LOG2E
        v = zi_ref[rows, :].astype(BF16)

        g_hi = g.astype(BF16)
        g_lo = (g - g_hi.astype(F32)).astype(BF16)
        b2 = jnp.dot(tri, jnp.concatenate([g_hi, g_lo], axis=1), preferred_element_type=F32)
        b = b2[:, :HG_DIM] + b2[:, HG_DIM:]
        b_last = b[CHUNK - 1:CHUNK, :]
        qd = (q * jnp.exp2(jnp.minimum(b, 0.0))).astype(BF16)
        kd = (kk * jnp.exp2(jnp.minimum(b_last - b, 0.0))).astype(BF16)

        half = SUBLANES // 2
        b3, g3, q3, k3 = tiled(b), tiled(g), tiled(q), tiled(kk)
        ref_low = [None,
                   jnp.where(sub < half, tile_row(b3, 1), tile_row(b3, half + 1)),
                   tile_row(b3, half - 1)]
        a_low = []
        for l in range(LOW_LEVELS):
            upper = upper_low[l]
            nd = jnp.where(upper, g3, 0.0) if l == 0 else -jnp.abs(b3 - ref_low[l])
            x = flat(jnp.exp2(nd) * jnp.where(upper, q3, k3)).astype(BF16)
            a_low.append(lax.dot_general(x, x, nt, preferred_element_type=F32))

        a_up = {}
        for l in range(LOW_LEVELS, N_LEVELS):
            h = 1 << l
            pairs = CHUNK // (2 * h)
            bg = b.reshape(pairs, 2, h, HG_DIM)
            ref = bg[:, 0, h - 1:h, :]
            x_lo = kk.reshape(pairs, 2, h, HG_DIM)[:, 0] * jnp.exp2(jnp.minimum(ref - bg[:, 0], 0.0))
            x_up = q.reshape(pairs, 2, h, HG_DIM)[:, 1] * jnp.exp2(jnp.minimum(bg[:, 1] - ref, 0.0))
            x_all = jnp.stack([x_lo, x_up], axis=1).reshape(CHUNK, HG_DIM).astype(BF16)
            a_up[l] = lax.dot_general(x_up.reshape(CHUNK // 2, HG_DIM).astype(BF16), x_all, nt,
                                      preferred_element_type=F32)

        dqk = jnp.sum(q * kk, axis=1, keepdims=True)
        tiles = []
        for j in range(n_tiles):
            r = slice(j * SUBLANES, (j + 1) * SUBLANES)
            blk = jnp.where(pat_diag, dqk[r], 0.0)
            for l in range(LOW_LEVELS):
                blk = jnp.where(pat_low[l], a_low[l][r], blk)
            blk = jnp.where((lane >> 3) == j, blk, 0.0)
            for l in range(LOW_LEVELS, N_LEVELS):
                h = 1 << l
                t0 = j * SUBLANES
                if (t0 // h) % 2 == 1:
                    pair = t0 // (2 * h)
                    r_up = pair * h + (t0 - pair * 2 * h - h)
                    blk = jnp.where(lane < pair * 2 * h + h, a_up[l][r_up:r_up + SUBLANES], blk)
            tiles.append(blk)
        att = jnp.concatenate(tiles, axis=0)

        o = jnp.dot(att.astype(BF16), v, preferred_element_type=F32)
        o += lax.dot_general(qd, st.astype(BF16), nt, preferred_element_type=F32)
        st = st * jnp.exp2(jnp.minimum(b_last, 0.0)) + lax.dot_general(
            v, kd, (((0,), (0,)), ((), ())), preferred_element_type=F32)

        o = o * lax.rsqrt(jnp.mean(o * o, axis=-1, keepdims=True) + EPS) * gn
        y_ref[rows, :] = (o * _silu(zo_ref[rows, :])).astype(y_ref.dtype)
        return st

    st = lax.fori_loop(0, n_chunks, chunk, jnp.zeros((HG_DIM, HG_DIM), F32), unroll=8)
    s_ref[0, 0] = st.T


def _hgrn_prompt(z, lb, gn, *, layer, batch, seq):
    hw = HG_HEADS * HG_DIM
    col = lambda off: pl.BlockSpec((seq, HG_DIM), lambda n, h, off=off: (n, off + h))
    per_head = pl.BlockSpec((None, 1, HG_DIM), lambda n, h: (layer, 0, h))
    tri = jnp.asarray(_prefix_sum_matrix(), BF16)
    return pl.pallas_call(
        _hgrn_prompt_kernel,
        grid=(batch, HG_HEADS),
        in_specs=[col(0), col(HG_HEADS), col(2 * HG_HEADS), col(3 * HG_HEADS), per_head, per_head,
                  pl.BlockSpec(tri.shape, lambda n, h: (0, 0))],
        out_specs=[pl.BlockSpec((seq, HG_DIM), lambda n, h: (n, h)),
                   pl.BlockSpec((1, 1, HG_DIM, HG_DIM), lambda n, h: (n, h, 0, 0))],
        out_shape=[jax.ShapeDtypeStruct((batch * seq, hw), BF16),
                   jax.ShapeDtypeStruct((batch, HG_HEADS, HG_DIM, HG_DIM), F32)],
        compiler_params=_params("parallel", "parallel"),
        name="hgrn_prompt",
    )(z, z, z, z, lb, gn, tri)


def _conv_prompt_kernel(zb_ref, zc_ref, zh_ref, zv_ref, zg_ref, scw_ref, ccw_ref, ccb_ref,
                        lng_ref, lnb_ref, ys_ref, yc_ref, ns_ref, nc_ref, u_ref, a_ref):
    tb = pl.program_id(1)
    t_blk = zb_ref.shape[0]
    u_halo = SUBLANES
    a_halo = CONV_ROWS

    @pl.when(tb == 0)
    def _():
        u_ref[0:u_halo, :] = jnp.zeros((u_halo, u_ref.shape[1]), F32)
        a_ref[0:a_halo, :] = jnp.zeros((a_halo, a_ref.shape[1]), F32)

    @pl.when(tb > 0)
    def _():
        u_ref[0:u_halo, :] = u_ref[t_blk:t_blk + u_halo, :]
        a_ref[0:a_halo, :] = a_ref[t_blk:t_blk + a_halo, :]

    scw = scw_ref
    ccw = ccw_ref
    ccb = ccb_ref[...]
    lng = lng_ref[...]
    lnb = lnb_ref[...]

    def fill(i, carry):
        rows = pl.ds(pl.multiple_of(i * CONV_ROWS, CONV_ROWS), CONV_ROWS)
        u_ref[pl.ds(pl.multiple_of(u_halo + i * CONV_ROWS, SUBLANES), CONV_ROWS), :] = (
            zc_ref[rows, :] * zh_ref[rows, :])
        a_ref[pl.ds(pl.multiple_of(a_halo + i * CONV_ROWS, CONV_ROWS), CONV_ROWS), :] = (
            zv_ref[rows, :] * _sigmoid(zg_ref[rows, :]))
        return carry

    lax.fori_loop(0, t_blk // CONV_ROWS, fill, 0)

    def conv(i, carry):
        base = i * CONV_ROWS
        rows = pl.ds(pl.multiple_of(base, CONV_ROWS), CONV_ROWS)
        uw = u_ref[pl.ds(pl.multiple_of(base, CONV_ROWS), u_halo + CONV_ROWS), :]
        acc = scw[SC_TAPS - 1:SC_TAPS, :] * uw[u_halo:, :]
        for j in range(SC_TAPS - 1):
            off = u_halo - (SC_TAPS - 1) + j
            acc += scw[j:j + 1, :] * uw[off:off + CONV_ROWS, :]
        ys_ref[rows, :] = (zb_ref[rows, :] * acc).astype(ys_ref.dtype)

        strips = []
        for s in range(a_ref.shape[1] // LANES):
            lanes = slice(s * LANES, (s + 1) * LANES)
            aw = a_ref[pl.ds(pl.multiple_of(base, CONV_ROWS), a_halo + CONV_ROWS), lanes]
            acc = ccb[:, lanes]
            for res in range(SUBLANES):
                shifted = aw if res == 0 else pltpu.roll(aw, a_halo + CONV_ROWS - res, axis=0)
                for j in range(CC_TAPS):
                    off = a_halo - (CC_TAPS - 1) + j
                    if off % SUBLANES == res:
                        acc += ccw[j:j + 1, lanes] * shifted[off - res:off - res + CONV_ROWS, :]
            strips.append(acc)
        acc = jnp.concatenate(strips, axis=1)
        mu = jnp.mean(acc, axis=-1, keepdims=True)
        d = acc - mu
        var = jnp.mean(d * d, axis=-1, keepdims=True)
        y = d * lax.rsqrt(var + EPS) * lng + lnb
        yc_ref[rows, :] = _silu(y).astype(yc_ref.dtype)
        return carry

    lax.fori_loop(0, t_blk // CONV_ROWS, conv, 0)

    @pl.when(tb == pl.num_programs(1) - 1)
    def _():
        ns_ref[0] = u_ref[u_halo + t_blk - (SC_TAPS - 1):u_halo + t_blk, :]
        nc_ref[0] = a_ref[a_halo + t_blk - (CC_TAPS - 1):a_halo + t_blk, :]


def _conv_prompt(z, scw, ccw, ccb, lng, lnb, *, layer, batch, seq, sc_off, cc_off):
    w = scw.shape[2]
    t_blk = CONV_TIME_BLOCK
    nt = seq // t_blk
    col = lambda off: pl.BlockSpec((t_blk, w), lambda n, t, off=off: (n * nt + t, off))
    whole = lambda a: _layer_block(a, layer)
    out_rows = pl.BlockSpec((t_blk, w), lambda n, t: (n * nt + t, 0))
    return pl.pallas_call(
        _conv_prompt_kernel,
        grid=(batch, nt),
        in_specs=[col(sc_off), col(sc_off + 1), col(sc_off + 2), col(cc_off), col(cc_off + 1),
                  whole(scw), whole(ccw), whole(ccb), whole(lng), whole(lnb)],
        out_specs=[out_rows, out_rows,
                   pl.BlockSpec((1, SC_TAPS - 1, w), lambda n, t: (n, 0, 0)),
                   pl.BlockSpec((1, CC_TAPS - 1, w), lambda n, t: (n, 0, 0))],
        out_shape=[jax.ShapeDtypeStruct((batch * seq, w), BF16),
                   jax.ShapeDtypeStruct((batch * seq, w), BF16),
                   jax.ShapeDtypeStruct((batch, SC_TAPS - 1, w), F32),
                   jax.ShapeDtypeStruct((batch, CC_TAPS - 1, w), F32)],
        scratch_shapes=[pltpu.VMEM((SUBLANES + t_blk, w), F32),
                        pltpu.VMEM((CONV_ROWS + t_blk, w), F32)],
        compiler_params=_params("parallel", "arbitrary"),
        name="conv_prompt",
    )(z, z, z, z, z, scw, ccw, ccb, lng, lnb)


def _mix_sample_kernel(z_ref, st_ref, ss_ref, sc_ref, lb_ref, gn_ref, scw_ref, ccw_ref, ccb_ref,
                       lng_ref, lnb_ref, mh_ref, ms_ref, mc_ref, so_ref, ns_ref, nc_ref, oh_ref):
    hw = HG_HEADS * HG_DIM
    w = scw_ref.shape[1]
    eye = (lax.broadcasted_iota(jnp.int32, (HG_DIM, HG_DIM), 0)
           == lax.broadcasted_iota(jnp.int32, (HG_DIM, HG_DIM), 1))

    def column(x_row):
        return jnp.sum(jnp.where(eye, x_row, 0.0), axis=1, keepdims=True)

    q_all = _silu(z_ref[:, 0:hw])
    f_all, k_all = _forget_and_key(z_ref[:, hw:2 * hw], *_gate_constants(lb_ref[...]))
    decay_all = jnp.exp(jnp.log(jnp.maximum(f_all, F_FLOOR)))
    v_all = z_ref[:, 2 * hw:3 * hw]
    for t in range(z_ref.shape[0]):
        tr = slice(t, t + 1)
        for h in range(HG_HEADS):
            cols = slice(h * HG_DIM, (h + 1) * HG_DIM)
            s_new = (column(decay_all[tr, cols]) * st_ref[t, h]
                     + column(k_all[tr, cols]) * v_all[tr, cols])
            so_ref[t, h] = s_new
            oh_ref[tr, cols] = jnp.dot(q_all[tr, cols].astype(BF16), s_new.astype(BF16),
                                       preferred_element_type=F32)
    gate_all = _silu(z_ref[:, 3 * hw:4 * hw])
    for h in range(HG_HEADS):
        cols = slice(h * HG_DIM, (h + 1) * HG_DIM)
        o = oh_ref[:, cols]
        o = o * lax.rsqrt(jnp.mean(o * o, axis=-1, keepdims=True) + EPS) * gn_ref[:, cols]
        oh_ref[:, cols] = o * gate_all[:, cols]
    mh_ref[...] = oh_ref[...].astype(mh_ref.dtype)

    off = 4 * hw
    zb = z_ref[:, off:off + w]
    u = z_ref[:, off + w:off + 2 * w] * z_ref[:, off + 2 * w:off + 3 * w]
    conv = scw_ref[SC_TAPS - 1:SC_TAPS, :] * u
    for j in range(SC_TAPS - 1):
        conv += scw_ref[j:j + 1, :] * ss_ref[:, j * w:(j + 1) * w]
    ms_ref[...] = (zb * conv).astype(ms_ref.dtype)
    for j in range(SC_TAPS - 2):
        ns_ref[:, j * w:(j + 1) * w] = ss_ref[:, (j + 1) * w:(j + 2) * w]
    ns_ref[:, (SC_TAPS - 2) * w:] = u

    off = 4 * hw + 3 * w
    a = z_ref[:, off:off + w] * _sigmoid(z_ref[:, off + w:off + 2 * w])
    conv = ccb_ref[...] + ccw_ref[CC_TAPS - 1:CC_TAPS, :] * a
    for j in range(CC_TAPS - 1):
        conv += ccw_ref[j:j + 1, :] * sc_ref[:, j * w:(j + 1) * w]
    mu = jnp.mean(conv, axis=-1, keepdims=True)
    d = conv - mu
    var = jnp.mean(d * d, axis=-1, keepdims=True)
    y = d * lax.rsqrt(var + EPS) * lng_ref[...] + lnb_ref[...]
    mc_ref[...] = _silu(y).astype(mc_ref.dtype)
    nc_ref[:, :(CC_TAPS - 2) * w] = sc_ref[:, w:]
    nc_ref[:, (CC_TAPS - 2) * w:] = a


def _mix_sample(z, st_h, st_s, st_c, lb, gn, scw, ccw, ccb, lng, lnb, *, layer):
    nb = z.shape[0]
    tb = SAMPLE_TOKENS_PER_STEP
    hw = HG_HEADS * HG_DIM
    w = scw.shape[2]
    rows = lambda width: pl.BlockSpec((tb, width), lambda i: (i, 0))
    layer_rows = lambda width: pl.BlockSpec((None, tb, width), lambda i: (layer, i, 0))
    whole = lambda a: _layer_block(a, layer)
    return pl.pallas_call(
        _mix_sample_kernel,
        grid=(nb // tb,),
        in_specs=[rows(z.shape[1]),
                  pl.BlockSpec((None, tb, HG_HEADS, HG_DIM, HG_DIM), lambda i: (layer, i, 0, 0, 0)),
                  layer_rows(st_s.shape[2]), layer_rows(st_c.shape[2]),
                  whole(lb), whole(gn), whole(scw), whole(ccw), whole(ccb), whole(lng), whole(lnb)],
        out_specs=[rows(hw), rows(w), rows(w),
                   pl.BlockSpec((tb, HG_HEADS, HG_DIM, HG_DIM), lambda i: (i, 0, 0, 0)),
                   rows(st_s.shape[2]), rows(st_c.shape[2])],
        out_shape=[jax.ShapeDtypeStruct((nb, hw), BF16),
                   jax.ShapeDtypeStruct((nb, w), BF16),
                   jax.ShapeDtypeStruct((nb, w), BF16),
                   jax.ShapeDtypeStruct((nb, HG_HEADS, HG_DIM, HG_DIM), F32),
                   jax.ShapeDtypeStruct((nb, st_s.shape[2]), F32),
                   jax.ShapeDtypeStruct((nb, st_c.shape[2]), F32)],
        scratch_shapes=[pltpu.VMEM((tb, hw), F32)],
        compiler_params=_params("parallel"),
        name="mix_sample",
    )(z, st_h, st_s, st_c, lb, gn, scw, ccw, ccb, lng, lnb)


def kernel(x_prompt, x_sample, state_hgrn, state_sconv, state_cconv, g_mix, w_in, hgrn_lb,
           hgrn_norm_g, sconv_w, cconv_w, cconv_b, cconv_ln_g, cconv_ln_b, w_out, g_mlp,
           w_up, w_down, g_final):
    batch, seq, d = x_prompt.shape
    nb = x_sample.shape[0]
    depth = w_in.shape[0]
    hw = HG_HEADS * HG_DIM
    w = sconv_w.shape[2]
    assert state_hgrn.shape[2:] == (HG_HEADS, HG_DIM, HG_DIM)
    assert w_in.shape[2] == 4 * hw + 5 * w and w_out.shape[1] == hw + 2 * w
    assert sconv_w.shape[1] == SC_TAPS and cconv_w.shape[1] == CC_TAPS
    assert seq % CONV_TIME_BLOCK == 0 and seq % CHUNK == 0 and nb % SAMPLE_TOKENS_PER_STEP == 0

    p = jax.nn.softmax(hgrn_lb.astype(F32), axis=0)
    lb_all = jnp.cumsum(p, axis=0) - p[0:1]

    w_in_b = w_in.astype(BF16)
    w_out_b = w_out.astype(BF16)
    w_up_b = w_up.astype(BF16)
    w_down_b = w_down.astype(BF16)
    rows = lambda a: a.reshape(depth, 1, -1)
    g1, g2, gf = rows(g_mix), rows(g_mlp), g_final.reshape(1, -1)
    lb, gn = rows(lb_all), rows(hgrn_norm_g)
    ccb, lng, lnb = rows(cconv_b), rows(cconv_ln_g), rows(cconv_ln_b)

    xp = x_prompt.reshape(batch * seq, d)
    xs = x_sample.reshape(nb, d)
    st_s = state_sconv.reshape(depth, nb, (SC_TAPS - 1) * w)
    st_c = state_cconv.reshape(depth, nb, (CC_TAPS - 1) * w)
    sc_off = 4 * hw // w
    cc_off = sc_off + 3

    ph, ps, pc, sh, ss, sc = [], [], [], [], [], []
    for li in range(depth):
        last = li == depth - 1

        z = _norm_matmul(xp, g1, w_in_b, layer=li, tm=1024, tn=512)
        mh, new_h = _hgrn_prompt(z, lb, gn, layer=li, batch=batch, seq=seq)
        ms, mc, new_s, new_c = _conv_prompt(z, sconv_w, cconv_w, ccb, lng, lnb, layer=li,
                                            batch=batch, seq=seq, sc_off=sc_off, cc_off=cc_off)
        xp = _out_proj(xp, mh, ms, mc, w_out_b, layer=li, tm=512)
        xp = _ffn(xp, g2, w_up_b, w_down_b, gf, layer=li, tm=1024, tf=512, final_norm=last)
        ph.append(new_h)
        ps.append(new_s)
        pc.append(new_c)

        z = _norm_matmul(xs, g1, w_in_b, layer=li, tm=nb, tn=512)
        mh, ms, mc, new_h, new_s, new_c = _mix_sample(
            z, state_hgrn, st_s, st_c, lb, gn, sconv_w, cconv_w, ccb, lng, lnb, layer=li)
        xs = _out_proj(xs, mh, ms, mc, w_out_b, layer=li, tm=nb)
        xs = _ffn(xs, g2, w_up_b, w_down_b, gf, layer=li, tm=nb, tf=1024, final_norm=last)
        sh.append(new_h)
        ss.append(new_s.reshape(nb, SC_TAPS - 1, w))
        sc.append(new_c.reshape(nb, CC_TAPS - 1, w))

    return (xp.reshape(batch, seq, d), xs.reshape(nb, 1, d), jnp.stack(ph), jnp.stack(ps),
            jnp.stack(pc), jnp.stack(sh), jnp.stack(ss), jnp.stack(sc))
```

```python
import functools

import numpy as np
import jax
import jax.numpy as jnp
from jax import lax
from jax.experimental import pallas as pl
from jax.experimental.pallas import tpu as pltpu

F32 = jnp.float32
BF16 = jnp.bfloat16

EPS = 1e-6
F_FLOOR = 1e-30

LANES = 128
SUBLANES = 8
VMEM_LIMIT_BYTES = 56 * 1024 * 1024

HG_HEADS = 8
HG_DIM = 128
SC_TAPS = 3
CC_TAPS = 31
CHUNK = 128
N_LEVELS = 7
SAMPLE_TOKENS_PER_STEP = 8
WEIGHT_BLOCK = 512
CONV_TIME_BLOCK = 512
CONV_ROWS = 32


def _params(*sem):
    return pltpu.CompilerParams(dimension_semantics=sem, vmem_limit_bytes=VMEM_LIMIT_BYTES)


def _layer_block(a, layer):
    zeros = (0,) * (a.ndim - 1)
    return pl.BlockSpec((None,) + a.shape[1:], lambda *_: (layer,) + zeros)


def _rmsnorm(x, g):
    return x * lax.rsqrt(jnp.mean(x * x, axis=-1, keepdims=True) + EPS) * g


def _sigmoid(x):
    return 0.5 * jnp.tanh(0.5 * x) + 0.5


def _silu(x):
    h = 0.5 * x
    return h + h * jnp.tanh(h)


def _gate_constants(lb):
    f_scale = 0.5 * (1.0 - lb)
    return f_scale, lb + f_scale


def _forget_and_key(z, f_scale, f_shift):
    w = f_scale * jnp.tanh(0.5 * z)
    return f_shift + w, f_scale - w


def _norm_matmul_kernel(x_ref, g_ref, w_ref, o_ref, h_ref):
    @pl.when(pl.program_id(1) == 0)
    def _():
        h_ref[...] = _rmsnorm(x_ref[...], g_ref[...]).astype(BF16)

    o_ref[...] = jnp.dot(h_ref[...], w_ref[...], preferred_element_type=F32)


def _norm_matmul(x, g, wb, *, layer, tm):
    m, k = x.shape
    nblk, _, tn = wb.shape
    return pl.pallas_call(
        _norm_matmul_kernel,
        grid=(m // tm, nblk),
        in_specs=[pl.BlockSpec((tm, k), lambda i, j: (i, 0)),
                  _layer_block(g, layer),
                  pl.BlockSpec((None, k, tn), lambda i, j: (j, 0, 0))],
        out_specs=pl.BlockSpec((tm, tn), lambda i, j: (i, j)),
        out_shape=jax.ShapeDtypeStruct((m, nblk * tn), F32),
        scratch_shapes=[pltpu.VMEM((tm, k), BF16)],
        compiler_params=_params("parallel", "arbitrary"),
        name="in_proj",
    )(x, g, wb)


def _norm_matmul_cast_kernel(x_ref, g_ref, w_ref, o_ref, wb_ref, h_ref):
    @pl.when(pl.program_id(0) == 0)
    def _():
        h_ref[...] = _rmsnorm(x_ref[...], g_ref[...]).astype(BF16)

    wb = w_ref[...].astype(BF16)
    wb_ref[...] = wb
    o_ref[...] = jnp.dot(h_ref[...], wb, preferred_element_type=F32)


def _norm_matmul_cast(x, g, w, *, layer, tn):
    m, k = x.shape
    n = w.shape[2]
    return pl.pallas_call(
        _norm_matmul_cast_kernel,
        grid=(n // tn,),
        in_specs=[pl.BlockSpec((m, k), lambda j: (0, 0)),
                  _layer_block(g, layer),
                  pl.BlockSpec((None, k, tn), lambda j: (layer, 0, j))],
        out_specs=[pl.BlockSpec((m, tn), lambda j: (0, j)),
                   pl.BlockSpec((None, k, tn), lambda j: (j, 0, 0))],
        out_shape=[jax.ShapeDtypeStruct((m, n), F32),
                   jax.ShapeDtypeStruct((n // tn, k, tn), BF16)],
        scratch_shapes=[pltpu.VMEM((m, k), BF16)],
        compiler_params=_params("arbitrary"),
        name="in_proj_cast",
    )(x, g, w)


def _out_proj_kernel(x_ref, mh_ref, ms_ref, mc_ref, wh_ref, ws_ref, wc_ref, o_ref):
    acc = x_ref[...]
    acc += jnp.dot(mh_ref[...], wh_ref[...], preferred_element_type=F32)
    acc += jnp.dot(ms_ref[...], ws_ref[...], preferred_element_type=F32)
    acc += jnp.dot(mc_ref[...], wc_ref[...], preferred_element_type=F32)
    o_ref[...] = acc


def _out_proj_cast_kernel(x_ref, m_ref, w_ref, o_ref, wb_ref):
    @pl.when(pl.program_id(0) == 0)
    def _():
        o_ref[...] = x_ref[...]

    wb = w_ref[...].astype(BF16)
    wb_ref[...] = wb
    o_ref[...] += jnp.dot(m_ref[...], wb, preferred_element_type=F32)


def _out_proj_cast(x, mix, w_out, *, layer, tk):
    m, d = x.shape
    kdim = mix.shape[1]
    return pl.pallas_call(
        _out_proj_cast_kernel,
        grid=(kdim // tk,),
        in_specs=[pl.BlockSpec((m, d), lambda r: (0, 0)),
                  pl.BlockSpec((m, tk), lambda r: (0, r)),
                  pl.BlockSpec((None, tk, d), lambda r: (layer, r, 0))],
        out_specs=[pl.BlockSpec((m, d), lambda r: (0, 0)),
                   pl.BlockSpec((tk, d), lambda r: (r, 0))],
        out_shape=[jax.ShapeDtypeStruct((m, d), F32),
                   jax.ShapeDtypeStruct((kdim, d), BF16)],
        compiler_params=_params("arbitrary"),
        name="out_proj_cast",
    )(x, mix, w_out)


def _out_proj(x, mh, ms, mc, w_out, *, tm):
    m, d = x.shape
    hw, w = mh.shape[1], ms.shape[1]
    row = lambda a: pl.BlockSpec((tm, a.shape[1]), lambda i: (i, 0))
    w_rows = lambda rows, blk: pl.BlockSpec((rows, d), lambda i: (blk, 0))
    return pl.pallas_call(
        _out_proj_kernel,
        grid=(m // tm,),
        in_specs=[row(x), row(mh), row(ms), row(mc),
                  w_rows(hw, 0), w_rows(w, hw // w), w_rows(w, hw // w + 1)],
        out_specs=row(x),
        out_shape=jax.ShapeDtypeStruct((m, d), F32),
        compiler_params=_params("parallel"),
        name="out_proj",
    )(x, mh, ms, mc, w_out, w_out, w_out)


def _ffn_kernel(x_ref, g_ref, wu_ref, wd_ref, gf_ref, o_ref, h_ref, *, final_norm):
    j = pl.program_id(1)

    @pl.when(j == 0)
    def _():
        x = x_ref[...]
        h_ref[...] = _rmsnorm(x, g_ref[...]).astype(BF16)
        o_ref[...] = x

    u = jnp.dot(h_ref[...], wu_ref[...], preferred_element_type=F32)
    r = jnp.maximum(u, 0.0)
    o_ref[...] += jnp.dot((r * r).astype(BF16), wd_ref[...], preferred_element_type=F32)

    if final_norm:
        @pl.when(j == pl.num_programs(1) - 1)
        def _():
            o_ref[...] = _rmsnorm(o_ref[...], gf_ref[...])


def _ffn_cast_kernel(x_ref, g_ref, wu_ref, wd_ref, gf_ref, o_ref, wub_ref, wdb_ref, h_ref, *,
                     final_norm):
    j = pl.program_id(0)

    @pl.when(j == 0)
    def _():
        x = x_ref[...]
        h_ref[...] = _rmsnorm(x, g_ref[...]).astype(BF16)
        o_ref[...] = x

    wu = wu_ref[...].astype(BF16)
    wd = wd_ref[...].astype(BF16)
    wub_ref[...] = wu
    wdb_ref[...] = wd
    u = jnp.dot(h_ref[...], wu, preferred_element_type=F32)
    r = jnp.maximum(u, 0.0)
    o_ref[...] += jnp.dot((r * r).astype(BF16), wd, preferred_element_type=F32)

    if final_norm:
        @pl.when(j == pl.num_programs(0) - 1)
        def _():
            o_ref[...] = _rmsnorm(o_ref[...], gf_ref[...])


def _ffn_cast(x, g, wu, wd, gf, *, layer, tf, final_norm):
    m, d = x.shape
    dff = wu.shape[2]
    return pl.pallas_call(
        functools.partial(_ffn_cast_kernel, final_norm=final_norm),
        grid=(dff // tf,),
        in_specs=[pl.BlockSpec((m, d), lambda j: (0, 0)),
                  _layer_block(g, layer),
                  pl.BlockSpec((None, d, tf), lambda j: (layer, 0, j)),
                  pl.BlockSpec((None, tf, d), lambda j: (layer, j, 0)),
                  pl.BlockSpec((1, d), lambda j: (0, 0))],
        out_specs=[pl.BlockSpec((m, d), lambda j: (0, 0)),
                   pl.BlockSpec((None, d, tf), lambda j: (j, 0, 0)),
                   pl.BlockSpec((tf, d), lambda j: (j, 0))],
        out_shape=[jax.ShapeDtypeStruct((m, d), F32),
                   jax.ShapeDtypeStruct((dff // tf, d, tf), BF16),
                   jax.ShapeDtypeStruct((dff, d), BF16)],
        scratch_shapes=[pltpu.VMEM((m, d), BF16)],
        compiler_params=_params("arbitrary"),
        name="ffn_cast",
    )(x, g, wu, wd, gf)


def _ffn(x, g, wub, wdb, gf, *, layer, tm, final_norm):
    m, d = x.shape
    nblk, _, tf = wub.shape
    return pl.pallas_call(
        functools.partial(_ffn_kernel, final_norm=final_norm),
        grid=(m // tm, nblk),
        in_specs=[pl.BlockSpec((tm, d), lambda i, j: (i, 0)),
                  _layer_block(g, layer),
                  pl.BlockSpec((None, d, tf), lambda i, j: (j, 0, 0)),
                  pl.BlockSpec((tf, d), lambda i, j: (j, 0)),
                  pl.BlockSpec((1, d), lambda i, j: (0, 0))],
        out_specs=pl.BlockSpec((tm, d), lambda i, j: (i, 0)),
        out_shape=jax.ShapeDtypeStruct((m, d), F32),
        scratch_shapes=[pltpu.VMEM((tm, d), BF16)],
        compiler_params=_params("parallel", "arbitrary"),
        name="ffn",
    )(x, g, wub, wdb, gf)


LOG2E = 1.4426950408889634
LOW_LEVELS = 3


def _prefix_sum_matrix():
    t = np.arange(CHUNK)[:, None]
    r = np.arange(CHUNK)[None, :]
    return (r <= t).astype(np.float32)


def _hgrn_prompt_kernel(zq_ref, zf_ref, zi_ref, zo_ref, lb_ref, gn_ref, tri_ref, y_ref, s_ref):
    n_chunks = zq_ref.shape[0] // CHUNK
    n_tiles = CHUNK // SUBLANES
    lb = lb_ref[...]
    gn = gn_ref[...]
    tri = tri_ref[...]
    f_scale, f_shift = _gate_constants(lb)
    sub = lax.broadcasted_iota(jnp.int32, (SUBLANES, HG_DIM), 0)
    lane = lax.broadcasted_iota(jnp.int32, (SUBLANES, HG_DIM), 1)
    col = lane & (SUBLANES - 1)
    below = col < sub
    pat_low = [below & (((sub ^ col) >> l) == 1) for l in range(LOW_LEVELS)]
    pat_diag = col == sub
    upper_low = [((sub >> l) & 1) == 1 for l in range(LOW_LEVELS)]
    nt = (((1,), (1,)), ((), ()))
    tiled = lambda x: x.reshape(n_tiles, SUBLANES, HG_DIM)
    flat = lambda x: x.reshape(CHUNK, HG_DIM)

    def tile_row(x3, s):
        return jnp.broadcast_to(x3[:, s:s + 1, :], x3.shape)

    def chunk(c, st):
        rows = pl.ds(pl.multiple_of(c * CHUNK, CHUNK), CHUNK)
        q = _silu(zq_ref[rows, :])
        f, kk = _forget_and_key(zf_ref[rows, :], f_scale, f_shift)
        g = jnp.log(jnp.maximum(f, F_FLOOR)) * LOG2E
        v = zi_ref[rows, :].astype(BF16)

        g_hi = g.astype(BF16)
        g_lo = (g - g_hi.astype(F32)).astype(BF16)
        b2 = jnp.dot(tri, jnp.concatenate([g_hi, g_lo], axis=1), preferred_element_type=F32)
        b = b2[:, :HG_DIM] + b2[:, HG_DIM:]
        b_last = b[CHUNK - 1:CHUNK, :]
        qd = (q * jnp.exp2(jnp.minimum(b, 0.0))).astype(BF16)
        kd = (kk * jnp.exp2(jnp.minimum(b_last - b, 0.0))).astype(BF16)

        half = SUBLANES // 2
        b3, g3, q3, k3 = tiled(b), tiled(g), tiled(q), tiled(kk)
        ref_low = [None,
                   jnp.where(sub < half, tile_row(b3, 1), tile_row(b3, half + 1)),
                   tile_row(b3, half - 1)]
        a_low = []
        for l in range(LOW_LEVELS):
            upper = upper_low[l]
            nd = jnp.where(upper, g3, 0.0) if l == 0 else -jnp.abs(b3 - ref_low[l])
            x = flat(jnp.exp2(nd) * jnp.where(upper, q3, k3)).astype(BF16)
            a_low.append(lax.dot_general(x, x, nt, preferred_element_type=F32))

        a_up = {}
        for l in range(LOW_LEVELS, N_LEVELS):
            h = 1 << l
            pairs = CHUNK // (2 * h)
            bg = b.reshape(pairs, 2, h, HG_DIM)
            ref = bg[:, 0, h - 1:h, :]
            x_lo = kk.reshape(pairs, 2, h, HG_DIM)[:, 0] * jnp.exp2(jnp.minimum(ref - bg[:, 0], 0.0))
            x_up = q.reshape(pairs, 2, h, HG_DIM)[:, 1] * jnp.exp2(jnp.minimum(bg[:, 1] - ref, 0.0))
            x_all = jnp.stack([x_lo, x_up], axis=1).reshape(CHUNK, HG_DIM).astype(BF16)
            a_up[l] = lax.dot_general(x_up.reshape(CHUNK // 2, HG_DIM).astype(BF16), x_all, nt,
                                      preferred_element_type=F32)

        dqk = jnp.sum(q * kk, axis=1, keepdims=True)
        tiles = []
        for j in range(n_tiles):
            r = slice(j * SUBLANES, (j + 1) * SUBLANES)
            blk = jnp.where(pat_diag, dqk[r], 0.0)
            for l in range(LOW_LEVELS):
                blk = jnp.where(pat_low[l], a_low[l][r], blk)
            blk = jnp.where((lane >> 3) == j, blk, 0.0)
            for l in range(LOW_LEVELS, N_LEVELS):
                h = 1 << l
                t0 = j * SUBLANES
                if (t0 // h) % 2 == 1:
                    pair = t0 // (2 * h)
                    r_up = pair * h + (t0 - pair * 2 * h - h)
                    blk = jnp.where(lane < pair * 2 * h + h, a_up[l][r_up:r_up + SUBLANES], blk)
            tiles.append(blk)
        att = jnp.concatenate(tiles, axis=0)

        o = jnp.dot(att.astype(BF16), v, preferred_element_type=F32)
        o += lax.dot_general(qd, st.astype(BF16), nt, preferred_element_type=F32)
        st = st * jnp.exp2(jnp.minimum(b_last, 0.0)) + lax.dot_general(
            v, kd, (((0,), (0,)), ((), ())), preferred_element_type=F32)

        o = o * lax.rsqrt(jnp.mean(o * o, axis=-1, keepdims=True) + EPS) * gn
        y_ref[rows, :] = (o * _silu(zo_ref[rows, :])).astype(y_ref.dtype)
        return st

    st = lax.fori_loop(0, n_chunks, chunk, jnp.zeros((HG_DIM, HG_DIM), F32), unroll=8)
    s_ref[0, 0] = st.T


def _hgrn_prompt(z, lb, gn, *, layer, batch, seq):
    hw = HG_HEADS * HG_DIM
    col = lambda off: pl.BlockSpec((seq, HG_DIM), lambda n, h, off=off: (n, off + h))
    per_head = pl.BlockSpec((None, 1, HG_DIM), lambda n, h: (layer, 0, h))
    tri = jnp.asarray(_prefix_sum_matrix(), BF16)
    return pl.pallas_call(
        _hgrn_prompt_kernel,
        grid=(batch, HG_HEADS),
        in_specs=[col(0), col(HG_HEADS), col(2 * HG_HEADS), col(3 * HG_HEADS), per_head, per_head,
                  pl.BlockSpec(tri.shape, lambda n, h: (0, 0))],
        out_specs=[pl.BlockSpec((seq, HG_DIM), lambda n, h: (n, h)),
                   pl.BlockSpec((1, 1, HG_DIM, HG_DIM), lambda n, h: (n, h, 0, 0))],
        out_shape=[jax.ShapeDtypeStruct((batch * seq, hw), BF16),
                   jax.ShapeDtypeStruct((batch, HG_HEADS, HG_DIM, HG_DIM), F32)],
        compiler_params=_params("parallel", "parallel"),
        name="hgrn_prompt",
    )(z, z, z, z, lb, gn, tri)


def _conv_prompt_kernel(zb_ref, zc_ref, zh_ref, zv_ref, zg_ref, scw_ref, ccw_ref, ccb_ref,
                        lng_ref, lnb_ref, ys_ref, yc_ref, ns_ref, nc_ref, u_ref, a_ref):
    tb = pl.program_id(1)
    t_blk = zb_ref.shape[0]
    u_halo = SUBLANES
    a_halo = CONV_ROWS

    @pl.when(tb == 0)
    def _():
        u_ref[0:u_halo, :] = jnp.zeros((u_halo, u_ref.shape[1]), F32)
        a_ref[0:a_halo, :] = jnp.zeros((a_halo, a_ref.shape[1]), F32)

    @pl.when(tb > 0)
    def _():
        u_ref[0:u_halo, :] = u_ref[t_blk:t_blk + u_halo, :]
        a_ref[0:a_halo, :] = a_ref[t_blk:t_blk + a_halo, :]

    scw = scw_ref
    ccw = ccw_ref
    ccb = ccb_ref[...]
    lng = lng_ref[...]
    lnb = lnb_ref[...]

    def fill(i, carry):
        rows = pl.ds(pl.multiple_of(i * CONV_ROWS, CONV_ROWS), CONV_ROWS)
        u_ref[pl.ds(pl.multiple_of(u_halo + i * CONV_ROWS, SUBLANES), CONV_ROWS), :] = (
            zc_ref[rows, :] * zh_ref[rows, :])
        a_ref[pl.ds(pl.multiple_of(a_halo + i * CONV_ROWS, CONV_ROWS), CONV_ROWS), :] = (
            zv_ref[rows, :] * _sigmoid(zg_ref[rows, :]))
        return carry

    lax.fori_loop(0, t_blk // CONV_ROWS, fill, 0)

    def conv(i, carry):
        base = i * CONV_ROWS
        rows = pl.ds(pl.multiple_of(base, CONV_ROWS), CONV_ROWS)
        uw = u_ref[pl.ds(pl.multiple_of(base, CONV_ROWS), u_halo + CONV_ROWS), :]
        acc = scw[SC_TAPS - 1:SC_TAPS, :] * uw[u_halo:, :]
        for j in range(SC_TAPS - 1):
            off = u_halo - (SC_TAPS - 1) + j
            acc += scw[j:j + 1, :] * uw[off:off + CONV_ROWS, :]
        ys_ref[rows, :] = (zb_ref[rows, :] * acc).astype(ys_ref.dtype)

        strips = []
        for s in range(a_ref.shape[1] // LANES):
            lanes = slice(s * LANES, (s + 1) * LANES)
            aw = a_ref[pl.ds(pl.multiple_of(base, CONV_ROWS), a_halo + CONV_ROWS), lanes]
            acc = ccb[:, lanes]
            for res in range(SUBLANES):
                shifted = aw if res == 0 else pltpu.roll(aw, a_halo + CONV_ROWS - res, axis=0)
                for j in range(CC_TAPS):
                    off = a_halo - (CC_TAPS - 1) + j
                    if off % SUBLANES == res:
                        acc += ccw[j:j + 1, lanes] * shifted[off - res:off - res + CONV_ROWS, :]
            strips.append(acc)
        acc = jnp.concatenate(strips, axis=1)
        mu = jnp.mean(acc, axis=-1, keepdims=True)
        d = acc - mu
        var = jnp.mean(d * d, axis=-1, keepdims=True)
        y = d * lax.rsqrt(var + EPS) * lng + lnb
        yc_ref[rows, :] = _silu(y).astype(yc_ref.dtype)
        return carry

    lax.fori_loop(0, t_blk // CONV_ROWS, conv, 0)

    @pl.when(tb == pl.num_programs(1) - 1)
    def _():
        ns_ref[0] = u_ref[u_halo + t_blk - (SC_TAPS - 1):u_halo + t_blk, :]
        nc_ref[0] = a_ref[a_halo + t_blk - (CC_TAPS - 1):a_halo + t_blk, :]


def _conv_prompt(z, scw, ccw, ccb, lng, lnb, *, layer, batch, seq, sc_off, cc_off):
    w = scw.shape[2]
    t_blk = CONV_TIME_BLOCK
    nt = seq // t_blk
    col = lambda off: pl.BlockSpec((t_blk, w), lambda n, t, off=off: (n * nt + t, off))
    whole = lambda a: _layer_block(a, layer)
    out_rows = pl.BlockSpec((t_blk, w), lambda n, t: (n * nt + t, 0))
    return pl.pallas_call(
        _conv_prompt_kernel,
        grid=(batch, nt),
        in_specs=[col(sc_off), col(sc_off + 1), col(sc_off + 2), col(cc_off), col(cc_off + 1),
                  whole(scw), whole(ccw), whole(ccb), whole(lng), whole(lnb)],
        out_specs=[out_rows, out_rows,
                   pl.BlockSpec((1, SC_TAPS - 1, w), lambda n, t: (n, 0, 0)),
                   pl.BlockSpec((1, CC_TAPS - 1, w), lambda n, t: (n, 0, 0))],
        out_shape=[jax.ShapeDtypeStruct((batch * seq, w), BF16),
                   jax.ShapeDtypeStruct((batch * seq, w), BF16),
                   jax.ShapeDtypeStruct((batch, SC_TAPS - 1, w), F32),
                   jax.ShapeDtypeStruct((batch, CC_TAPS - 1, w), F32)],
        scratch_shapes=[pltpu.VMEM((SUBLANES + t_blk, w), F32),
                        pltpu.VMEM((CONV_ROWS + t_blk, w), F32)],
        compiler_params=_params("parallel", "arbitrary"),
        name="conv_prompt",
    )(z, z, z, z, z, scw, ccw, ccb, lng, lnb)


def _mix_sample_kernel(*refs):
    so_ref, nc_ref = refs[-4], refs[-2]

    @pl.when(pl.program_id(0) == 0)
    def _():
        _mix_sample_step(*refs[:11], *refs[-5:])

    @pl.when(pl.program_id(0) > 0)
    def _():
        so_ref[...] = jnp.zeros(so_ref.shape, so_ref.dtype)
        nc_ref[...] = jnp.zeros(nc_ref.shape, nc_ref.dtype)


def _mix_sample_step(z_ref, st_ref, ss_ref, sc_ref, lb_ref, gn_ref, scw_ref, ccw_ref, ccb_ref,
                     lng_ref, lnb_ref, mix_ref, so_ref, ns_ref, nc_ref, oh_ref):
    hw = HG_HEADS * HG_DIM
    w = scw_ref.shape[1]
    eye = (lax.broadcasted_iota(jnp.int32, (HG_DIM, HG_DIM), 0)
           == lax.broadcasted_iota(jnp.int32, (HG_DIM, HG_DIM), 1))

    def column(x_row):
        return jnp.sum(jnp.where(eye, x_row, 0.0), axis=1, keepdims=True)

    q_all = _silu(z_ref[:, 0:hw])
    f_all, k_all = _forget_and_key(z_ref[:, hw:2 * hw], *_gate_constants(lb_ref[...]))
    decay_all = jnp.exp(jnp.log(jnp.maximum(f_all, F_FLOOR)))
    v_all = z_ref[:, 2 * hw:3 * hw]
    for t in range(z_ref.shape[0]):
        tr = slice(t, t + 1)
        for h in range(HG_HEADS):
            cols = slice(h * HG_DIM, (h + 1) * HG_DIM)
            s_new = (column(decay_all[tr, cols]) * st_ref[t, h]
                     + column(k_all[tr, cols]) * v_all[tr, cols])
            so_ref[t, h] = s_new
            oh_ref[tr, cols] = jnp.dot(q_all[tr, cols].astype(BF16), s_new.astype(BF16),
                                       preferred_element_type=F32)
    gate_all = _silu(z_ref[:, 3 * hw:4 * hw])
    for h in range(HG_HEADS):
        cols = slice(h * HG_DIM, (h + 1) * HG_DIM)
        o = oh_ref[:, cols]
        o = o * lax.rsqrt(jnp.mean(o * o, axis=-1, keepdims=True) + EPS) * gn_ref[:, cols]
        oh_ref[:, cols] = o * gate_all[:, cols]
    mix_ref[:, 0:hw] = oh_ref[...].astype(mix_ref.dtype)

    off = 4 * hw
    zb = z_ref[:, off:off + w]
    u = z_ref[:, off + w:off + 2 * w] * z_ref[:, off + 2 * w:off + 3 * w]
    conv = scw_ref[SC_TAPS - 1:SC_TAPS, :] * u
    for j in range(SC_TAPS - 1):
        conv += scw_ref[j:j + 1, :] * ss_ref[:, j * w:(j + 1) * w]
    mix_ref[:, hw:hw + w] = (zb * conv).astype(mix_ref.dtype)
    for j in range(SC_TAPS - 2):
        ns_ref[:, j * w:(j + 1) * w] = ss_ref[:, (j + 1) * w:(j + 2) * w]
    ns_ref[:, (SC_TAPS - 2) * w:] = u

    off = 4 * hw + 3 * w
    a = z_ref[:, off:off + w] * _sigmoid(z_ref[:, off + w:off + 2 * w])
    conv = ccb_ref[...] + ccw_ref[CC_TAPS - 1:CC_TAPS, :] * a
    for j in range(CC_TAPS - 1):
        conv += ccw_ref[j:j + 1, :] * sc_ref[:, j * w:(j + 1) * w]
    mu = jnp.mean(conv, axis=-1, keepdims=True)
    d = conv - mu
    var = jnp.mean(d * d, axis=-1, keepdims=True)
    y = d * lax.rsqrt(var + EPS) * lng_ref[...] + lnb_ref[...]
    mix_ref[:, hw + w:] = _silu(y).astype(mix_ref.dtype)
    nc_ref[:, :(CC_TAPS - 2) * w] = sc_ref[:, w:]
    nc_ref[:, (CC_TAPS - 2) * w:] = a


def _mix_sample(z, st_h, st_s, st_c, lb, gn, scw, ccw, ccb, lng, lnb, *, layer, prev):
    nb = z.shape[0]
    tb = SAMPLE_TOKENS_PER_STEP
    hw = HG_HEADS * HG_DIM
    w = scw.shape[2]
    nblk = nb // tb
    n_pass = st_h.shape[0] - layer if prev is None else 1
    tok = lambda p, i: jnp.where(p == 0, i, nblk - 1)
    rows = lambda width: pl.BlockSpec((tb, width), lambda p, i: (tok(p, i), 0))
    layer_rows = lambda width: pl.BlockSpec((None, tb, width), lambda p, i: (layer, tok(p, i), 0))
    whole = lambda a: _layer_block(a, layer)
    state_tile = (None, tb, HG_HEADS, HG_DIM, HG_DIM)
    args = [z, st_h, st_s, st_c, lb, gn, scw, ccw, ccb, lng, lnb]
    in_specs = [rows(z.shape[1]),
                pl.BlockSpec(state_tile, lambda p, i: (layer, tok(p, i), 0, 0, 0)),
                layer_rows(st_s.shape[2]), layer_rows(st_c.shape[2]),
                whole(lb), whole(gn), whole(scw), whole(ccw), whole(ccb), whole(lng), whole(lnb)]
    aliases = {}
    if prev is not None:
        aliases = {len(args): 1, len(args) + 1: 3}
        args += list(prev)
        in_specs += [pl.BlockSpec(memory_space=pl.ANY)] * 2
    return pl.pallas_call(
        _mix_sample_kernel,
        grid=(n_pass, nblk),
        in_specs=in_specs,
        out_specs=[rows(hw + 2 * w),
                   pl.BlockSpec(state_tile, lambda p, i: (layer + p, i, 0, 0, 0)),
                   rows(st_s.shape[2]),
                   pl.BlockSpec((None, tb, st_c.shape[2]), lambda p, i: (layer + p, i, 0))],
        out_shape=[jax.ShapeDtypeStruct((nb, hw + 2 * w), BF16),
                   jax.ShapeDtypeStruct(st_h.shape, F32),
                   jax.ShapeDtypeStruct((nb, st_s.shape[2]), F32),
                   jax.ShapeDtypeStruct(st_c.shape, F32)],
        scratch_shapes=[pltpu.VMEM((tb, hw), F32)],
        input_output_aliases=aliases,
        compiler_params=_params("arbitrary", "arbitrary"),
        name="mix_sample",
    )(*args)


def kernel(x_prompt, x_sample, state_hgrn, state_sconv, state_cconv, g_mix, w_in, hgrn_lb,
           hgrn_norm_g, sconv_w, cconv_w, cconv_b, cconv_ln_g, cconv_ln_b, w_out, g_mlp,
           w_up, w_down, g_final):
    batch, seq, d = x_prompt.shape
    nb = x_sample.shape[0]
    depth = w_in.shape[0]
    hw = HG_HEADS * HG_DIM
    w = sconv_w.shape[2]
    assert state_hgrn.shape[2:] == (HG_HEADS, HG_DIM, HG_DIM)
    assert w_in.shape[2] == 4 * hw + 5 * w and w_out.shape[1] == hw + 2 * w
    assert sconv_w.shape[1] == SC_TAPS and cconv_w.shape[1] == CC_TAPS
    assert seq % CONV_TIME_BLOCK == 0 and seq % CHUNK == 0 and nb % SAMPLE_TOKENS_PER_STEP == 0

    p = jax.nn.softmax(hgrn_lb.astype(F32), axis=0)
    lb_all = jnp.cumsum(p, axis=0) - p[0:1]

    rows = lambda a: a.reshape(depth, 1, -1)
    g1, g2, gf = rows(g_mix), rows(g_mlp), g_final.reshape(1, -1)
    lb, gn = rows(lb_all), rows(hgrn_norm_g)
    ccb, lng, lnb = rows(cconv_b), rows(cconv_ln_g), rows(cconv_ln_b)

    xp = x_prompt.reshape(batch * seq, d)
    xs = x_sample.reshape(nb, d)
    st_s = state_sconv.reshape(depth, nb, (SC_TAPS - 1) * w)
    st_c = state_cconv.reshape(depth, nb, (CC_TAPS - 1) * w)
    sc_off = 4 * hw // w
    cc_off = sc_off + 3

    ph, ps, pc, ss = [], [], [], []
    sample_states = None
    for li in range(depth):
        last = li == depth - 1

        z, w_in_b = _norm_matmul_cast(xs, g1, w_in, layer=li, tn=WEIGHT_BLOCK)
        mix, new_h, new_s, new_c = _mix_sample(
            z, state_hgrn, st_s, st_c, lb, gn, sconv_w, cconv_w, ccb, lng, lnb, layer=li,
            prev=sample_states)
        sample_states = (new_h, new_c)
        xs, w_out_b = _out_proj_cast(xs, mix, w_out, layer=li, tk=WEIGHT_BLOCK)
        xs, w_up_b, w_down_b = _ffn_cast(xs, g2, w_up, w_down, gf, layer=li, tf=WEIGHT_BLOCK,
                                         final_norm=last)
        ss.append(new_s.reshape(nb, SC_TAPS - 1, w))

        z = _norm_matmul(xp, g1, w_in_b, layer=li, tm=1024)
        mh, new_h = _hgrn_prompt(z, lb, gn, layer=li, batch=batch, seq=seq)
        ms, mc, new_s, new_c = _conv_prompt(z, sconv_w, cconv_w, ccb, lng, lnb, layer=li,
                                            batch=batch, seq=seq, sc_off=sc_off, cc_off=cc_off)
        xp = _out_proj(xp, mh, ms, mc, w_out_b, tm=512)
        xp = _ffn(xp, g2, w_up_b, w_down_b, gf, layer=li, tm=1024, final_norm=last)
        ph.append(new_h)
        ps.append(new_s)
        pc.append(new_c)

    sh, sc = sample_states
    return (xp.reshape(batch, seq, d), xs.reshape(nb, 1, d), jnp.stack(ph), jnp.stack(ps),
            jnp.stack(pc), sh, jnp.stack(ss), sc.reshape(depth, nb, CC_TAPS - 1, w))
```

```python
import functools

import numpy as np
import jax
import jax.numpy as jnp
from jax import lax
from jax.experimental import pallas as pl
from jax.experimental.pallas import tpu as pltpu

F32 = jnp.float32
BF16 = jnp.bfloat16

EPS = 1e-6
F_FLOOR = 1e-30

LANES = 128
SUBLANES = 8
VMEM_LIMIT_BYTES = 56 * 1024 * 1024

HG_HEADS = 8
HG_DIM = 128
SC_TAPS = 3
CC_TAPS = 31
CHUNK = 128
N_LEVELS = 7
SAMPLE_TOKENS_PER_STEP = 8
WEIGHT_BLOCK = 512
CONV_TIME_BLOCK = 512
CONV_ROWS = 32


def _params(*sem):
    return pltpu.CompilerParams(dimension_semantics=sem, vmem_limit_bytes=VMEM_LIMIT_BYTES)


def _layer_block(a, layer):
    zeros = (0,) * (a.ndim - 1)
    return pl.BlockSpec((None,) + a.shape[1:], lambda *_: (layer,) + zeros)


def _rmsnorm(x, g):
    return x * lax.rsqrt(jnp.mean(x * x, axis=-1, keepdims=True) + EPS) * g


def _sigmoid(x):
    return 0.5 * jnp.tanh(0.5 * x) + 0.5


def _silu(x):
    h = 0.5 * x
    return h + h * jnp.tanh(h)


def _gate_constants(lb):
    f_scale = 0.5 * (1.0 - lb)
    return f_scale, lb + f_scale


def _forget_and_key(z, f_scale, f_shift):
    w = f_scale * jnp.tanh(0.5 * z)
    return f_shift + w, f_scale - w


def _norm_matmul_kernel(x_ref, g_ref, w_ref, o_ref, h_ref):
    @pl.when(pl.program_id(1) == 0)
    def _():
        h_ref[...] = _rmsnorm(x_ref[...], g_ref[...]).astype(BF16)

    o_ref[...] = jnp.dot(h_ref[...], w_ref[...], preferred_element_type=F32)


def _norm_matmul(x, g, wb, *, layer, tm):
    m, k = x.shape
    nblk, _, tn = wb.shape
    return pl.pallas_call(
        _norm_matmul_kernel,
        grid=(m // tm, nblk),
        in_specs=[pl.BlockSpec((tm, k), lambda i, j: (i, 0)),
                  _layer_block(g, layer),
                  pl.BlockSpec((None, k, tn), lambda i, j: (j, 0, 0))],
        out_specs=pl.BlockSpec((tm, tn), lambda i, j: (i, j)),
        out_shape=jax.ShapeDtypeStruct((m, nblk * tn), F32),
        scratch_shapes=[pltpu.VMEM((tm, k), BF16)],
        compiler_params=_params("parallel", "arbitrary"),
        name="in_proj",
    )(x, g, wb)


def _norm_matmul_cast_kernel(x_ref, g_ref, w_ref, o_ref, wb_ref, h_ref):
    @pl.when(pl.program_id(0) == 0)
    def _():
        h_ref[...] = _rmsnorm(x_ref[...], g_ref[...]).astype(BF16)

    wb = w_ref[...].astype(BF16)
    wb_ref[...] = wb
    o_ref[...] = jnp.dot(h_ref[...], wb, preferred_element_type=F32)


def _norm_matmul_cast(x, g, w, *, layer, tn):
    m, k = x.shape
    n = w.shape[2]
    return pl.pallas_call(
        _norm_matmul_cast_kernel,
        grid=(n // tn,),
        in_specs=[pl.BlockSpec((m, k), lambda j: (0, 0)),
                  _layer_block(g, layer),
                  pl.BlockSpec((None, k, tn), lambda j: (layer, 0, j))],
        out_specs=[pl.BlockSpec((m, tn), lambda j: (0, j)),
                   pl.BlockSpec((None, k, tn), lambda j: (j, 0, 0))],
        out_shape=[jax.ShapeDtypeStruct((m, n), F32),
                   jax.ShapeDtypeStruct((n // tn, k, tn), BF16)],
        scratch_shapes=[pltpu.VMEM((m, k), BF16)],
        compiler_params=_params("arbitrary"),
        name="in_proj_cast",
    )(x, g, w)


def _out_proj_cast_kernel(x_ref, m_ref, w_ref, o_ref, wb_ref):
    @pl.when(pl.program_id(0) == 0)
    def _():
        o_ref[...] = x_ref[...]

    wb = w_ref[...].astype(BF16)
    wb_ref[...] = wb
    o_ref[...] += jnp.dot(m_ref[...], wb, preferred_element_type=F32)


def _out_proj_cast(x, mix, w_out, *, layer, tk):
    m, d = x.shape
    kdim = mix.shape[1]
    return pl.pallas_call(
        _out_proj_cast_kernel,
        grid=(kdim // tk,),
        in_specs=[pl.BlockSpec((m, d), lambda r: (0, 0)),
                  pl.BlockSpec((m, tk), lambda r: (0, r)),
                  pl.BlockSpec((None, tk, d), lambda r: (layer, r, 0))],
        out_specs=[pl.BlockSpec((m, d), lambda r: (0, 0)),
                   pl.BlockSpec((tk, d), lambda r: (r, 0))],
        out_shape=[jax.ShapeDtypeStruct((m, d), F32),
                   jax.ShapeDtypeStruct((kdim, d), BF16)],
        compiler_params=_params("arbitrary"),
        name="out_proj_cast",
    )(x, mix, w_out)


def _ffn_kernel(x_ref, g_ref, wu_ref, wd_ref, gf_ref, o_ref, h_ref, *, final_norm):
    j = pl.program_id(1)

    @pl.when(j == 0)
    def _():
        x = x_ref[...]
        h_ref[...] = _rmsnorm(x, g_ref[...]).astype(BF16)
        o_ref[...] = x

    u = jnp.dot(h_ref[...], wu_ref[...], preferred_element_type=F32)
    r = jnp.maximum(u, 0.0)
    o_ref[...] += jnp.dot((r * r).astype(BF16), wd_ref[...], preferred_element_type=F32)

    if final_norm:
        @pl.when(j == pl.num_programs(1) - 1)
        def _():
            o_ref[...] = _rmsnorm(o_ref[...], gf_ref[...])


def _ffn_cast_kernel(x_ref, g_ref, wu_ref, wd_ref, gf_ref, o_ref, wub_ref, wdb_ref, h_ref, *,
                     final_norm):
    j = pl.program_id(0)

    @pl.when(j == 0)
    def _():
        x = x_ref[...]
        h_ref[...] = _rmsnorm(x, g_ref[...]).astype(BF16)
        o_ref[...] = x

    wu = wu_ref[...].astype(BF16)
    wd = wd_ref[...].astype(BF16)
    wub_ref[...] = wu
    wdb_ref[...] = wd
    u = jnp.dot(h_ref[...], wu, preferred_element_type=F32)
    r = jnp.maximum(u, 0.0)
    o_ref[...] += jnp.dot((r * r).astype(BF16), wd, preferred_element_type=F32)

    if final_norm:
        @pl.when(j == pl.num_programs(0) - 1)
        def _():
            o_ref[...] = _rmsnorm(o_ref[...], gf_ref[...])


def _ffn_cast(x, g, wu, wd, gf, *, layer, tf, final_norm):
    m, d = x.shape
    dff = wu.shape[2]
    return pl.pallas_call(
        functools.partial(_ffn_cast_kernel, final_norm=final_norm),
        grid=(dff // tf,),
        in_specs=[pl.BlockSpec((m, d), lambda j: (0, 0)),
                  _layer_block(g, layer),
                  pl.BlockSpec((None, d, tf), lambda j: (layer, 0, j)),
                  pl.BlockSpec((None, tf, d), lambda j: (layer, j, 0)),
                  pl.BlockSpec((1, d), lambda j: (0, 0))],
        out_specs=[pl.BlockSpec((m, d), lambda j: (0, 0)),
                   pl.BlockSpec((None, d, tf), lambda j: (j, 0, 0)),
                   pl.BlockSpec((tf, d), lambda j: (j, 0))],
        out_shape=[jax.ShapeDtypeStruct((m, d), F32),
                   jax.ShapeDtypeStruct((dff // tf, d, tf), BF16),
                   jax.ShapeDtypeStruct((dff, d), BF16)],
        scratch_shapes=[pltpu.VMEM((m, d), BF16)],
        compiler_params=_params("arbitrary"),
        name="ffn_cast",
    )(x, g, wu, wd, gf)


def _ffn(x, g, wub, wdb, gf, *, layer, tm, final_norm):
    m, d = x.shape
    nblk, _, tf = wub.shape
    return pl.pallas_call(
        functools.partial(_ffn_kernel, final_norm=final_norm),
        grid=(m // tm, nblk),
        in_specs=[pl.BlockSpec((tm, d), lambda i, j: (i, 0)),
                  _layer_block(g, layer),
                  pl.BlockSpec((None, d, tf), lambda i, j: (j, 0, 0)),
                  pl.BlockSpec((tf, d), lambda i, j: (j, 0)),
                  pl.BlockSpec((1, d), lambda i, j: (0, 0))],
        out_specs=pl.BlockSpec((tm, d), lambda i, j: (i, 0)),
        out_shape=jax.ShapeDtypeStruct((m, d), F32),
        scratch_shapes=[pltpu.VMEM((tm, d), BF16)],
        compiler_params=_params("parallel", "arbitrary"),
        name="ffn",
    )(x, g, wub, wdb, gf)


LOG2E = 1.4426950408889634
LOW_LEVELS = 3


def _prefix_sum_matrix():
    t = np.arange(CHUNK)[:, None]
    r = np.arange(CHUNK)[None, :]
    return (r <= t).astype(np.float32)


def _hgrn_prompt_kernel(zq_ref, zf_ref, zi_ref, zo_ref, lb_ref, gn_ref, tri_ref, y_ref, s_ref):
    n_chunks = zq_ref.shape[0] // CHUNK
    n_tiles = CHUNK // SUBLANES
    lb = lb_ref[...]
    gn = gn_ref[...]
    tri = tri_ref[...]
    f_scale, f_shift = _gate_constants(lb)
    sub = lax.broadcasted_iota(jnp.int32, (SUBLANES, HG_DIM), 0)
    lane = lax.broadcasted_iota(jnp.int32, (SUBLANES, HG_DIM), 1)
    col = lane & (SUBLANES - 1)
    below = col < sub
    pat_low = [below & (((sub ^ col) >> l) == 1) for l in range(LOW_LEVELS)]
    pat_diag = col == sub
    upper_low = [((sub >> l) & 1) == 1 for l in range(LOW_LEVELS)]
    nt = (((1,), (1,)), ((), ()))
    tiled = lambda x: x.reshape(n_tiles, SUBLANES, HG_DIM)
    flat = lambda x: x.reshape(CHUNK, HG_DIM)

    def tile_row(x3, s):
        return jnp.broadcast_to(x3[:, s:s + 1, :], x3.shape)

    def chunk(c, st):
        rows = pl.ds(pl.multiple_of(c * CHUNK, CHUNK), CHUNK)
        q = _silu(zq_ref[rows, :])
        f, kk = _forget_and_key(zf_ref[rows, :], f_scale, f_shift)
        g = jnp.log(jnp.maximum(f, F_FLOOR)) * LOG2E
        v = zi_ref[rows, :].astype(BF16)

        g_hi = g.astype(BF16)
        g_lo = (g - g_hi.astype(F32)).astype(BF16)
        b2 = jnp.dot(tri, jnp.concatenate([g_hi, g_lo], axis=1), preferred_element_type=F32)
        b = b2[:, :HG_DIM] + b2[:, HG_DIM:]
        b_last = b[CHUNK - 1:CHUNK, :]
        qd = (q * jnp.exp2(jnp.minimum(b, 0.0))).astype(BF16)
        kd = (kk * jnp.exp2(jnp.minimum(b_last - b, 0.0))).astype(BF16)

        half = SUBLANES // 2
        b3, g3, q3, k3 = tiled(b), tiled(g), tiled(q), tiled(kk)
        ref_low = [None,
                   jnp.where(sub < half, tile_row(b3, 1), tile_row(b3, half + 1)),
                   tile_row(b3, half - 1)]
        a_low = []
        for l in range(LOW_LEVELS):
            upper = upper_low[l]
            nd = jnp.where(upper, g3, 0.0) if l == 0 else -jnp.abs(b3 - ref_low[l])
            x = flat(jnp.exp2(nd) * jnp.where(upper, q3, k3)).astype(BF16)
            a_low.append(lax.dot_general(x, x, nt, preferred_element_type=F32))

        a_up = {}
        for l in range(LOW_LEVELS, N_LEVELS):
            h = 1 << l
            pairs = CHUNK // (2 * h)
            bg = b.reshape(pairs, 2, h, HG_DIM)
            ref = bg[:, 0, h - 1:h, :]
            x_lo = kk.reshape(pairs, 2, h, HG_DIM)[:, 0] * jnp.exp2(jnp.minimum(ref - bg[:, 0], 0.0))
            x_up = q.reshape(pairs, 2, h, HG_DIM)[:, 1] * jnp.exp2(jnp.minimum(bg[:, 1] - ref, 0.0))
            x_all = jnp.stack([x_lo, x_up], axis=1).reshape(CHUNK, HG_DIM).astype(BF16)
            a_up[l] = lax.dot_general(x_up.reshape(CHUNK // 2, HG_DIM).astype(BF16), x_all, nt,
                                      preferred_element_type=F32)

        dqk = jnp.sum(q * kk, axis=1, keepdims=True)
        tiles = []
        for j in range(n_tiles):
            r = slice(j * SUBLANES, (j + 1) * SUBLANES)
            blk = jnp.where(pat_diag, dqk[r], 0.0)
            for l in range(LOW_LEVELS):
                blk = jnp.where(pat_low[l], a_low[l][r], blk)
            blk = jnp.where((lane >> 3) == j, blk, 0.0)
            for l in range(LOW_LEVELS, N_LEVELS):
                h = 1 << l
                t0 = j * SUBLANES
                if (t0 // h) % 2 == 1:
                    pair = t0 // (2 * h)
                    r_up = pair * h + (t0 - pair * 2 * h - h)
                    blk = jnp.where(lane < pair * 2 * h + h, a_up[l][r_up:r_up + SUBLANES], blk)
            tiles.append(blk)
        att = jnp.concatenate(tiles, axis=0)

        o = jnp.dot(att.astype(BF16), v, preferred_element_type=F32)
        o += lax.dot_general(qd, st.astype(BF16), nt, preferred_element_type=F32)
        st = st * jnp.exp2(jnp.minimum(b_last, 0.0)) + lax.dot_general(
            v, kd, (((0,), (0,)), ((), ())), preferred_element_type=F32)

        o = o * lax.rsqrt(jnp.mean(o * o, axis=-1, keepdims=True) + EPS) * gn
        y_ref[rows, :] = (o * _silu(zo_ref[rows, :])).astype(y_ref.dtype)
        return st

    st = lax.fori_loop(0, n_chunks, chunk, jnp.zeros((HG_DIM, HG_DIM), F32), unroll=8)
    s_ref[0, 0] = st.T


def _hgrn_prompt(z, lb, gn, *, layer, batch, seq):
    hw = HG_HEADS * HG_DIM
    col = lambda off: pl.BlockSpec((seq, HG_DIM), lambda n, h, off=off: (n, off + h))
    per_head = pl.BlockSpec((None, 1, HG_DIM), lambda n, h: (layer, 0, h))
    tri = jnp.asarray(_prefix_sum_matrix(), BF16)
    return pl.pallas_call(
        _hgrn_prompt_kernel,
        grid=(batch, HG_HEADS),
        in_specs=[col(0), col(HG_HEADS), col(2 * HG_HEADS), col(3 * HG_HEADS), per_head, per_head,
                  pl.BlockSpec(tri.shape, lambda n, h: (0, 0))],
        out_specs=[pl.BlockSpec((seq, HG_DIM), lambda n, h: (n, h)),
                   pl.BlockSpec((1, 1, HG_DIM, HG_DIM), lambda n, h: (n, h, 0, 0))],
        out_shape=[jax.ShapeDtypeStruct((batch * seq, hw), BF16),
                   jax.ShapeDtypeStruct((batch, HG_HEADS, HG_DIM, HG_DIM), F32)],
        compiler_params=_params("parallel", "parallel"),
        name="hgrn_prompt",
    )(z, z, z, z, lb, gn, tri)


def _conv_out_proj_kernel(x_ref, mh_ref, zb_ref, zc_ref, zh_ref, zv_ref, zg_ref, scw_ref, ccw_ref,
                          ccb_ref, lng_ref, lnb_ref, wh_ref, ws_ref, wc_ref,
                          o_ref, ns_ref, nc_ref,
                          u_ref, a_ref, ms_next, mc_next, ms_cur, mc_cur, *, tiles_per_seq):
    s = pl.program_id(0)
    t_blk = zb_ref.shape[0]
    u_halo = SUBLANES
    a_halo = CONV_ROWS

    @pl.when(s == 0)
    def _():
        ms_next[...] = jnp.zeros(ms_next.shape, ms_next.dtype)
        mc_next[...] = jnp.zeros(mc_next.shape, mc_next.dtype)
        u_ref[...] = jnp.zeros(u_ref.shape, u_ref.dtype)
        a_ref[...] = jnp.zeros(a_ref.shape, a_ref.dtype)

    ms_cur[...] = ms_next[...]
    mc_cur[...] = mc_next[...]
    acc = x_ref[...]
    acc += jnp.dot(mh_ref[...], wh_ref[...], preferred_element_type=F32)
    acc += jnp.dot(ms_cur[...], ws_ref[...], preferred_element_type=F32)
    acc += jnp.dot(mc_cur[...], wc_ref[...], preferred_element_type=F32)
    o_ref[...] = acc

    tile = jnp.minimum(s, pl.num_programs(0) - 2)
    seq_start = lax.rem(tile, tiles_per_seq) == 0
    u_ref[0:u_halo, :] = jnp.where(seq_start, 0.0, u_ref[t_blk:t_blk + u_halo, :])
    a_ref[0:a_halo, :] = jnp.where(seq_start, 0.0, a_ref[t_blk:t_blk + a_halo, :])
    u_ref[u_halo:, :] = zc_ref[...] * zh_ref[...]
    a_ref[a_halo:, :] = zv_ref[...] * _sigmoid(zg_ref[...])
    ccb = ccb_ref[...]
    lng = lng_ref[...]
    lnb = lnb_ref[...]
    for i in range(t_blk // CONV_ROWS):
        base = i * CONV_ROWS
        rows = slice(base, base + CONV_ROWS)
        uw = u_ref[base:base + u_halo + CONV_ROWS, :]
        acc = scw_ref[SC_TAPS - 1:SC_TAPS, :] * uw[u_halo:, :]
        for j in range(SC_TAPS - 1):
            off = u_halo - (SC_TAPS - 1) + j
            acc += scw_ref[j:j + 1, :] * uw[off:off + CONV_ROWS, :]
        ms_next[rows, :] = (zb_ref[rows, :] * acc).astype(ms_next.dtype)

        strips = []
        for st in range(a_ref.shape[1] // LANES):
            lanes = slice(st * LANES, (st + 1) * LANES)
            aw = a_ref[base:base + a_halo + CONV_ROWS, lanes]
            acc = ccb[:, lanes]
            for res in range(SUBLANES):
                shifted = aw if res == 0 else pltpu.roll(aw, a_halo + CONV_ROWS - res, axis=0)
                for j in range(CC_TAPS):
                    off = a_halo - (CC_TAPS - 1) + j
                    if off % SUBLANES == res:
                        acc += ccw_ref[j:j + 1, lanes] * shifted[off - res:off - res + CONV_ROWS, :]
            strips.append(acc)
        acc = jnp.concatenate(strips, axis=1)
        mu = jnp.mean(acc, axis=-1, keepdims=True)
        d = acc - mu
        var = jnp.mean(d * d, axis=-1, keepdims=True)
        mc_next[rows, :] = _silu(d * lax.rsqrt(var + EPS) * lng + lnb).astype(mc_next.dtype)

    ns_ref[0] = u_ref[u_halo + t_blk - (SC_TAPS - 1):u_halo + t_blk, :]
    nc_ref[0] = a_ref[a_halo + t_blk - (CC_TAPS - 1):a_halo + t_blk, :]


def _conv_out_proj(x, mh, z, scw, ccw, ccb, lng, lnb, w_out, *, layer, seq, sc_off, cc_off):
    m, d = x.shape
    hw = mh.shape[1]
    w = scw.shape[2]
    t_blk = CONV_TIME_BLOCK
    n_tiles = m // t_blk
    tiles_per_seq = seq // t_blk
    conv_tile = lambda s: jnp.minimum(s, n_tiles - 1)
    proj_tile = lambda s: jnp.maximum(s - 1, 0)
    zcol = lambda off: pl.BlockSpec((t_blk, w), lambda s, off=off: (conv_tile(s), off))
    prow = lambda width: pl.BlockSpec((t_blk, width), lambda s: (proj_tile(s), 0))
    whole = lambda a: _layer_block(a, layer)
    w_rows = lambda rows, blk: pl.BlockSpec((rows, d), lambda s: (blk, 0))
    state = lambda taps: pl.BlockSpec((1, taps - 1, w),
                                      lambda s: (conv_tile(s) // tiles_per_seq, 0, 0))
    return pl.pallas_call(
        functools.partial(_conv_out_proj_kernel, tiles_per_seq=tiles_per_seq),
        grid=(n_tiles + 1,),
        in_specs=[prow(d), prow(hw),
                  zcol(sc_off), zcol(sc_off + 1), zcol(sc_off + 2), zcol(cc_off), zcol(cc_off + 1),
                  whole(scw), whole(ccw), whole(ccb), whole(lng), whole(lnb),
                  w_rows(hw, 0), w_rows(w, hw // w), w_rows(w, hw // w + 1)],
        out_specs=[prow(d), state(SC_TAPS), state(CC_TAPS)],
        out_shape=[jax.ShapeDtypeStruct((m, d), F32),
                   jax.ShapeDtypeStruct((m // seq, SC_TAPS - 1, w), F32),
                   jax.ShapeDtypeStruct((m // seq, CC_TAPS - 1, w), F32)],
        scratch_shapes=[pltpu.VMEM((SUBLANES + t_blk, w), F32),
                        pltpu.VMEM((CONV_ROWS + t_blk, w), F32),
                        pltpu.VMEM((t_blk, w), BF16), pltpu.VMEM((t_blk, w), BF16),
                        pltpu.VMEM((t_blk, w), BF16), pltpu.VMEM((t_blk, w), BF16)],
        compiler_params=_params("arbitrary"),
        name="conv_out_proj",
    )(x, mh, z, z, z, z, z, scw, ccw, ccb, lng, lnb, w_out, w_out, w_out)


def _mix_sample_kernel(*refs):
    so_ref, nc_ref = refs[-5], refs[-3]

    @pl.when(pl.program_id(0) == 0)
    def _():
        _mix_sample_step(*refs[:11], *refs[-6:])

    @pl.when(pl.program_id(0) > 0)
    def _():
        so_ref[...] = jnp.zeros(so_ref.shape, so_ref.dtype)
        nc_ref[...] = jnp.zeros(nc_ref.shape, nc_ref.dtype)


def _mix_sample_step(z_ref, st_ref, ss_ref, sc_ref, lb_ref, gn_ref, scw_ref, ccw_ref, ccb_ref,
                     lng_ref, lnb_ref, mix_ref, so_ref, ns_ref, nc_ref, oh_ref, cv_ref):
    hw = HG_HEADS * HG_DIM
    w = scw_ref.shape[1]
    eye = (lax.broadcasted_iota(jnp.int32, (HG_DIM, HG_DIM), 0)
           == lax.broadcasted_iota(jnp.int32, (HG_DIM, HG_DIM), 1))

    def column(x_row):
        return jnp.sum(jnp.where(eye, x_row, 0.0), axis=1, keepdims=True)

    q_all = _silu(z_ref[:, 0:hw])
    f_all, k_all = _forget_and_key(z_ref[:, hw:2 * hw], *_gate_constants(lb_ref[...]))
    decay_all = jnp.exp(jnp.log(jnp.maximum(f_all, F_FLOOR)))
    v_all = z_ref[:, 2 * hw:3 * hw]
    for t in range(z_ref.shape[0]):
        tr = slice(t, t + 1)
        for h in range(HG_HEADS):
            cols = slice(h * HG_DIM, (h + 1) * HG_DIM)
            s_new = (column(decay_all[tr, cols]) * st_ref[t, h]
                     + column(k_all[tr, cols]) * v_all[tr, cols])
            so_ref[t, h] = s_new
            oh_ref[tr, cols] = jnp.dot(q_all[tr, cols].astype(BF16), s_new.astype(BF16),
                                       preferred_element_type=F32)
    gate_all = _silu(z_ref[:, 3 * hw:4 * hw])
    for h in range(HG_HEADS):
        cols = slice(h * HG_DIM, (h + 1) * HG_DIM)
        o = oh_ref[:, cols]
        o = o * lax.rsqrt(jnp.mean(o * o, axis=-1, keepdims=True) + EPS) * gn_ref[:, cols]
        oh_ref[:, cols] = o * gate_all[:, cols]
    mix_ref[:, 0:hw] = oh_ref[...].astype(mix_ref.dtype)

    off = 4 * hw
    zb = z_ref[:, off:off + w]
    u = z_ref[:, off + w:off + 2 * w] * z_ref[:, off + 2 * w:off + 3 * w]
    conv = scw_ref[SC_TAPS - 1:SC_TAPS, :] * u
    for j in range(SC_TAPS - 1):
        conv += scw_ref[j:j + 1, :] * ss_ref[:, j * w:(j + 1) * w]
    mix_ref[:, hw:hw + w] = (zb * conv).astype(mix_ref.dtype)
    for j in range(SC_TAPS - 2):
        ns_ref[:, j * w:(j + 1) * w] = ss_ref[:, (j + 1) * w:(j + 2) * w]
    ns_ref[:, (SC_TAPS - 2) * w:] = u

    off = 4 * hw + 3 * w
    a = z_ref[:, off:off + w] * _sigmoid(z_ref[:, off + w:off + 2 * w])
    past_taps = ccw_ref[0:CC_TAPS - 1, :]
    for t in range(z_ref.shape[0]):
        cv_ref[t:t + 1, :] = jnp.sum(past_taps * sc_ref[t], axis=0, keepdims=True)
        nc_ref[t, 0:CC_TAPS - 2, :] = sc_ref[t, 1:CC_TAPS - 1, :]
        nc_ref[t, CC_TAPS - 2:CC_TAPS - 1, :] = a[t:t + 1, :]
    conv = ccb_ref[...] + ccw_ref[CC_TAPS - 1:CC_TAPS, :] * a + cv_ref[...]
    mu = jnp.mean(conv, axis=-1, keepdims=True)
    d = conv - mu
    var = jnp.mean(d * d, axis=-1, keepdims=True)
    y = d * lax.rsqrt(var + EPS) * lng_ref[...] + lnb_ref[...]
    mix_ref[:, hw + w:] = _silu(y).astype(mix_ref.dtype)


def _mix_sample(z, st_h, st_s, st_c, lb, gn, scw, ccw, ccb, lng, lnb, *, layer, prev):
    nb = z.shape[0]
    tb = SAMPLE_TOKENS_PER_STEP
    hw = HG_HEADS * HG_DIM
    w = scw.shape[2]
    nblk = nb // tb
    n_pass = st_h.shape[0] - layer if prev is None else 1
    tok = lambda p, i: jnp.where(p == 0, i, nblk - 1)
    rows = lambda width: pl.BlockSpec((tb, width), lambda p, i: (tok(p, i), 0))
    layer_rows = lambda width: pl.BlockSpec((None, tb, width), lambda p, i: (layer, tok(p, i), 0))
    whole = lambda a: _layer_block(a, layer)
    state_tile = (None, tb, HG_HEADS, HG_DIM, HG_DIM)
    conv_tile = (None, tb) + st_c.shape[2:]
    args = [z, st_h, st_s, st_c, lb, gn, scw, ccw, ccb, lng, lnb]
    in_specs = [rows(z.shape[1]),
                pl.BlockSpec(state_tile, lambda p, i: (layer, tok(p, i), 0, 0, 0)),
                layer_rows(st_s.shape[2]),
                pl.BlockSpec(conv_tile, lambda p, i: (layer, tok(p, i), 0, 0)),
                whole(lb), whole(gn), whole(scw), whole(ccw), whole(ccb), whole(lng), whole(lnb)]
    aliases = {}
    if prev is not None:
        aliases = {len(args): 1, len(args) + 1: 3}
        args += list(prev)
        in_specs += [pl.BlockSpec(memory_space=pl.ANY)] * 2
    return pl.pallas_call(
        _mix_sample_kernel,
        grid=(n_pass, nblk),
        in_specs=in_specs,
        out_specs=[rows(hw + 2 * w),
                   pl.BlockSpec(state_tile, lambda p, i: (layer + p, i, 0, 0, 0)),
                   rows(st_s.shape[2]),
                   pl.BlockSpec(conv_tile, lambda p, i: (layer + p, i, 0, 0))],
        out_shape=[jax.ShapeDtypeStruct((nb, hw + 2 * w), BF16),
                   jax.ShapeDtypeStruct(st_h.shape, F32),
                   jax.ShapeDtypeStruct((nb, st_s.shape[2]), F32),
                   jax.ShapeDtypeStruct(st_c.shape, F32)],
        scratch_shapes=[pltpu.VMEM((tb, hw), F32), pltpu.VMEM((tb, w), F32)],
        input_output_aliases=aliases,
        compiler_params=_params("arbitrary", "arbitrary"),
        name="mix_sample",
    )(*args)


def kernel(x_prompt, x_sample, state_hgrn, state_sconv, state_cconv, g_mix, w_in, hgrn_lb,
           hgrn_norm_g, sconv_w, cconv_w, cconv_b, cconv_ln_g, cconv_ln_b, w_out, g_mlp,
           w_up, w_down, g_final):
    batch, seq, d = x_prompt.shape
    nb = x_sample.shape[0]
    depth = w_in.shape[0]
    hw = HG_HEADS * HG_DIM
    w = sconv_w.shape[2]
    assert state_hgrn.shape[2:] == (HG_HEADS, HG_DIM, HG_DIM)
    assert w_in.shape[2] == 4 * hw + 5 * w and w_out.shape[1] == hw + 2 * w
    assert sconv_w.shape[1] == SC_TAPS and cconv_w.shape[1] == CC_TAPS
    assert seq % CONV_TIME_BLOCK == 0 and seq % CHUNK == 0 and nb % SAMPLE_TOKENS_PER_STEP == 0

    p = jax.nn.softmax(hgrn_lb.astype(F32), axis=0)
    lb_all = jnp.cumsum(p, axis=0) - p[0:1]

    rows = lambda a: a.reshape(depth, 1, -1)
    g1, g2, gf = rows(g_mix), rows(g_mlp), g_final.reshape(1, -1)
    lb, gn = rows(lb_all), rows(hgrn_norm_g)
    ccb, lng, lnb = rows(cconv_b), rows(cconv_ln_g), rows(cconv_ln_b)

    xp = x_prompt.reshape(batch * seq, d)
    xs = x_sample.reshape(nb, d)
    st_s = state_sconv.reshape(depth, nb, (SC_TAPS - 1) * w)
    sc_off = 4 * hw // w
    cc_off = sc_off + 3

    ph, ps, pc, ss = [], [], [], []
    sample_states = None
    for li in range(depth):
        last = li == depth - 1

        z, w_in_b = _norm_matmul_cast(xs, g1, w_in, layer=li, tn=WEIGHT_BLOCK)
        mix, new_h, new_s, new_c = _mix_sample(
            z, state_hgrn, st_s, state_cconv, lb, gn, sconv_w, cconv_w, ccb, lng, lnb, layer=li,
            prev=sample_states)
        sample_states = (new_h, new_c)
        xs, w_out_b = _out_proj_cast(xs, mix, w_out, layer=li, tk=WEIGHT_BLOCK)
        xs, w_up_b, w_down_b = _ffn_cast(xs, g2, w_up, w_down, gf, layer=li, tf=WEIGHT_BLOCK,
                                         final_norm=last)
        ss.append(new_s.reshape(nb, SC_TAPS - 1, w))

        z = _norm_matmul(xp, g1, w_in_b, layer=li, tm=1024)
        mh, new_h = _hgrn_prompt(z, lb, gn, layer=li, batch=batch, seq=seq)
        xp, new_s, new_c = _conv_out_proj(xp, mh, z, sconv_w, cconv_w, ccb, lng, lnb, w_out_b,
                                          layer=li, seq=seq, sc_off=sc_off, cc_off=cc_off)
        xp = _ffn(xp, g2, w_up_b, w_down_b, gf, layer=li, tm=1024, final_norm=last)
        ph.append(new_h)
        ps.append(new_s)
        pc.append(new_c)

    sh, sc = sample_states
    return (xp.reshape(batch, seq, d), xs.reshape(nb, 1, d), jnp.stack(ph), jnp.stack(ps),
            jnp.stack(pc), sh, jnp.stack(ss), sc)
```

```python
import functools
from typing import NamedTuple, Optional

import numpy as np
import jax
import jax.numpy as jnp
from jax import lax
from jax.experimental import pallas as pl
from jax.experimental.pallas import tpu as pltpu

F32 = jnp.float32
BF16 = jnp.bfloat16

EPS = 1e-6
F_FLOOR = 1e-30

LANES = 128
SUBLANES = 8
BF16_SUBLANES = 16
VMEM_LIMIT_BYTES = 56 * 1024 * 1024

HG_HEADS = 8
HG_DIM = 128
SC_TAPS = 3
CC_TAPS = 31
CHUNK = 128
N_LEVELS = 7
SAMPLE_TOKENS_PER_STEP = 8
WEIGHT_BLOCK = 512
CONV_TIME_BLOCK = 512
CONV_ROWS = 32


def _params(*sem):
    return pltpu.CompilerParams(dimension_semantics=sem, vmem_limit_bytes=VMEM_LIMIT_BYTES)


def _layer_block(a, layer):
    zeros = (0,) * (a.ndim - 1)
    return pl.BlockSpec((None,) + a.shape[1:], lambda *_: (layer,) + zeros)


def _rmsnorm(x, g):
    return x * lax.rsqrt(jnp.mean(x * x, axis=-1, keepdims=True) + EPS) * g


def _sigmoid(x):
    return 0.5 * jnp.tanh(0.5 * x) + 0.5


def _silu(x):
    h = 0.5 * x
    return h + h * jnp.tanh(h)


def _gate_constants(lb):
    f_scale = 0.5 * (1.0 - lb)
    return f_scale, lb + f_scale


def _forget_and_key(z, f_scale, f_shift):
    w = f_scale * jnp.tanh(0.5 * z)
    return f_shift + w, f_scale - w


class _CastJob(NamedTuple):
    w: jax.Array
    layer: int
    col_block: Optional[int]


def _cast_job_plan(job, n_steps, step_of):
    _, rows, cols = job.w.shape
    chunk_rows = BF16_SUBLANES
    while rows % chunk_rows or rows // chunk_rows > n_steps:
        chunk_rows += BF16_SUBLANES
    n_chunks = rows // chunk_rows
    chunk = lambda *idx: jnp.minimum(step_of(*idx), n_chunks - 1)
    in_spec = pl.BlockSpec((None, chunk_rows, cols), lambda *idx: (job.layer, chunk(*idx), 0))
    if job.col_block is None:
        out_spec = pl.BlockSpec((chunk_rows, cols), lambda *idx: (chunk(*idx), 0))
        return in_spec, out_spec, jax.ShapeDtypeStruct((rows, cols), BF16)
    nblk = cols // job.col_block
    out_spec = pl.BlockSpec((nblk, chunk_rows, job.col_block), lambda *idx: (0, chunk(*idx), 0))
    return in_spec, out_spec, jax.ShapeDtypeStruct((nblk, rows, job.col_block), BF16)


def _run_cast_jobs(src_refs, dst_refs):
    for src, dst in zip(src_refs, dst_refs):
        if len(dst.shape) == 2:
            dst[...] = src[...].astype(dst.dtype)
        else:
            width = dst.shape[2]
            for c in range(dst.shape[0]):
                dst[c] = src[:, c * width:(c + 1) * width].astype(dst.dtype)


def _norm_matmul_kernel(x_ref, g_ref, w_ref, *rest, n_jobs):
    cast_src, o_ref, cast_dst, h_ref = rest[:n_jobs], rest[n_jobs], rest[n_jobs + 1:-1], rest[-1]

    @pl.when(pl.program_id(1) == 0)
    def _():
        h_ref[...] = _rmsnorm(x_ref[...], g_ref[...]).astype(BF16)

    o_ref[...] = jnp.dot(h_ref[...], w_ref[...], preferred_element_type=F32)
    _run_cast_jobs(cast_src, cast_dst)


def _norm_matmul(x, g, wb, *, layer, tm, cast_jobs=()):
    m, k = x.shape
    nblk, _, tn = wb.shape
    grid = (m // tm, nblk)
    plans = [_cast_job_plan(job, grid[0] * grid[1], lambda i, j: i * nblk + j)
             for job in cast_jobs]
    return pl.pallas_call(
        functools.partial(_norm_matmul_kernel, n_jobs=len(plans)),
        grid=grid,
        in_specs=[pl.BlockSpec((tm, k), lambda i, j: (i, 0)),
                  _layer_block(g, layer),
                  pl.BlockSpec((None, k, tn), lambda i, j: (j, 0, 0))] + [p[0] for p in plans],
        out_specs=[pl.BlockSpec((tm, tn), lambda i, j: (i, j))] + [p[1] for p in plans],
        out_shape=[jax.ShapeDtypeStruct((m, nblk * tn), F32)] + [p[2] for p in plans],
        scratch_shapes=[pltpu.VMEM((tm, k), BF16)],
        compiler_params=_params("arbitrary", "arbitrary"),
        name="in_proj",
    )(x, g, wb, *[job.w for job in cast_jobs])


def _norm_matmul_cast_kernel(x_ref, g_ref, w_ref, o_ref, wb_ref, h_ref):
    @pl.when(pl.program_id(0) == 0)
    def _():
        h_ref[...] = _rmsnorm(x_ref[...], g_ref[...]).astype(BF16)

    wb = w_ref[...].astype(BF16)
    wb_ref[...] = wb
    o_ref[...] = jnp.dot(h_ref[...], wb, preferred_element_type=F32)


def _norm_matmul_cast(x, g, w, *, layer, tn):
    m, k = x.shape
    n = w.shape[2]
    return pl.pallas_call(
        _norm_matmul_cast_kernel,
        grid=(n // tn,),
        in_specs=[pl.BlockSpec((m, k), lambda j: (0, 0)),
                  _layer_block(g, layer),
                  pl.BlockSpec((None, k, tn), lambda j: (layer, 0, j))],
        out_specs=[pl.BlockSpec((m, tn), lambda j: (0, j)),
                   pl.BlockSpec((None, k, tn), lambda j: (j, 0, 0))],
        out_shape=[jax.ShapeDtypeStruct((m, n), F32),
                   jax.ShapeDtypeStruct((n // tn, k, tn), BF16)],
        scratch_shapes=[pltpu.VMEM((m, k), BF16)],
        compiler_params=_params("arbitrary"),
        name="in_proj_cast",
    )(x, g, w)


def _out_proj_sample_kernel(x_ref, m_ref, w_ref, o_ref):
    @pl.when(pl.program_id(0) == 0)
    def _():
        o_ref[...] = x_ref[...]

    o_ref[...] += jnp.dot(m_ref[...], w_ref[...], preferred_element_type=F32)


def _out_proj_sample(x, mix, w_out_b, *, tk):
    m, d = x.shape
    kdim = mix.shape[1]
    return pl.pallas_call(
        _out_proj_sample_kernel,
        grid=(kdim // tk,),
        in_specs=[pl.BlockSpec((m, d), lambda r: (0, 0)),
                  pl.BlockSpec((m, tk), lambda r: (0, r)),
                  pl.BlockSpec((tk, d), lambda r: (r, 0))],
        out_specs=pl.BlockSpec((m, d), lambda r: (0, 0)),
        out_shape=jax.ShapeDtypeStruct((m, d), F32),
        compiler_params=_params("arbitrary"),
        name="out_proj_sample",
    )(x, mix, w_out_b)


def _ffn_kernel(x_ref, g_ref, wu_ref, wd_ref, gf_ref, *rest, final_norm, n_jobs):
    cast_src, o_ref, cast_dst, h_ref = rest[:n_jobs], rest[n_jobs], rest[n_jobs + 1:-1], rest[-1]
    j = pl.program_id(1)

    @pl.when(j == 0)
    def _():
        x = x_ref[...]
        h_ref[...] = _rmsnorm(x, g_ref[...]).astype(BF16)
        o_ref[...] = x

    u = jnp.dot(h_ref[...], wu_ref[...], preferred_element_type=F32)
    r = jnp.maximum(u, 0.0)
    o_ref[...] += jnp.dot((r * r).astype(BF16), wd_ref[...], preferred_element_type=F32)
    _run_cast_jobs(cast_src, cast_dst)

    if final_norm:
        @pl.when(j == pl.num_programs(1) - 1)
        def _():
            o_ref[...] = _rmsnorm(o_ref[...], gf_ref[...])


def _ffn(x, g, wub, wdb, gf, *, layer, tm, final_norm, cast_jobs=()):
    m, d = x.shape
    nblk, _, tf = wub.shape
    grid = (m // tm, nblk)
    plans = [_cast_job_plan(job, grid[0] * grid[1], lambda i, j: i * nblk + j)
             for job in cast_jobs]
    return pl.pallas_call(
        functools.partial(_ffn_kernel, final_norm=final_norm, n_jobs=len(plans)),
        grid=grid,
        in_specs=[pl.BlockSpec((tm, d), lambda i, j: (i, 0)),
                  _layer_block(g, layer),
                  pl.BlockSpec((None, d, tf), lambda i, j: (j, 0, 0)),
                  pl.BlockSpec((tf, d), lambda i, j: (j, 0)),
                  pl.BlockSpec((1, d), lambda i, j: (0, 0))] + [p[0] for p in plans],
        out_specs=[pl.BlockSpec((tm, d), lambda i, j: (i, 0))] + [p[1] for p in plans],
        out_shape=[jax.ShapeDtypeStruct((m, d), F32)] + [p[2] for p in plans],
        scratch_shapes=[pltpu.VMEM((tm, d), BF16)],
        compiler_params=_params("arbitrary", "arbitrary"),
        name="ffn",
    )(x, g, wub, wdb, gf, *[job.w for job in cast_jobs])


LOG2E = 1.4426950408889634
LOW_LEVELS = 3


def _prefix_sum_matrix():
    t = np.arange(CHUNK)[:, None]
    r = np.arange(CHUNK)[None, :]
    return (r <= t).astype(np.float32)


def _hgrn_prompt_kernel(zq_ref, zf_ref, zi_ref, zo_ref, lb_ref, gn_ref, tri_ref, y_ref, s_ref):
    n_chunks = zq_ref.shape[0] // CHUNK
    n_tiles = CHUNK // SUBLANES
    lb = lb_ref[...]
    gn = gn_ref[...]
    tri = tri_ref[...]
    f_scale, f_shift = _gate_constants(lb)
    sub = lax.broadcasted_iota(jnp.int32, (SUBLANES, HG_DIM), 0)
    lane = lax.broadcasted_iota(jnp.int32, (SUBLANES, HG_DIM), 1)
    col = lane & (SUBLANES - 1)
    below = col < sub
    pat_low = [below & (((sub ^ col) >> l) == 1) for l in range(LOW_LEVELS)]
    pat_diag = col == sub
    upper_low = [((sub >> l) & 1) == 1 for l in range(LOW_LEVELS)]
    nt = (((1,), (1,)), ((), ()))
    tiled = lambda x: x.reshape(n_tiles, SUBLANES, HG_DIM)
    flat = lambda x: x.reshape(CHUNK, HG_DIM)

    def tile_row(x3, s):
        return jnp.broadcast_to(x3[:, s:s + 1, :], x3.shape)

    def chunk(c, st):
        rows = pl.ds(pl.multiple_of(c * CHUNK, CHUNK), CHUNK)
        q = _silu(zq_ref[rows, :])
        f, kk = _forget_and_key(zf_ref[rows, :], f_scale, f_shift)
        g = jnp.log(jnp.maximum(f, F_FLOOR)) * LOG2E
        v = zi_ref[rows, :].astype(BF16)

        g_hi = g.astype(BF16)
        g_lo = (g - g_hi.astype(F32)).astype(BF16)
        b2 = jnp.dot(tri, jnp.concatenate([g_hi, g_lo], axis=1), preferred_element_type=F32)
        b = b2[:, :HG_DIM] + b2[:, HG_DIM:]
        b_last = b[CHUNK - 1:CHUNK, :]
        qd = (q * jnp.exp2(jnp.minimum(b, 0.0))).astype(BF16)
        kd = (kk * jnp.exp2(jnp.minimum(b_last - b, 0.0))).astype(BF16)

        half = SUBLANES // 2
        b3, g3, q3, k3 = tiled(b), tiled(g), tiled(q), tiled(kk)
        ref_low = [None,
                   jnp.where(sub < half, tile_row(b3, 1), tile_row(b3, half + 1)),
                   tile_row(b3, half - 1)]
        a_low = []
        for l in range(LOW_LEVELS):
            upper = upper_low[l]
            nd = jnp.where(upper, g3, 0.0) if l == 0 else -jnp.abs(b3 - ref_low[l])
            x = flat(jnp.exp2(nd) * jnp.where(upper, q3, k3)).astype(BF16)
            a_low.append(lax.dot_general(x, x, nt, preferred_element_type=F32))

        a_up = {}
        for l in range(LOW_LEVELS, N_LEVELS):
            h = 1 << l
            pairs = CHUNK // (2 * h)
            bg = b.reshape(pairs, 2, h, HG_DIM)
            ref = bg[:, 0, h - 1:h, :]
            x_lo = kk.reshape(pairs, 2, h, HG_DIM)[:, 0] * jnp.exp2(jnp.minimum(ref - bg[:, 0], 0.0))
            x_up = q.reshape(pairs, 2, h, HG_DIM)[:, 1] * jnp.exp2(jnp.minimum(bg[:, 1] - ref, 0.0))
            x_all = jnp.stack([x_lo, x_up], axis=1).reshape(CHUNK, HG_DIM).astype(BF16)
            a_up[l] = lax.dot_general(x_up.reshape(CHUNK // 2, HG_DIM).astype(BF16), x_all, nt,
                                      preferred_element_type=F32)

        dqk = jnp.sum(q * kk, axis=1, keepdims=True)
        tiles = []
        for j in range(n_tiles):
            r = slice(j * SUBLANES, (j + 1) * SUBLANES)
            blk = jnp.where(pat_diag, dqk[r], 0.0)
            for l in range(LOW_LEVELS):
                blk = jnp.where(pat_low[l], a_low[l][r], blk)
            blk = jnp.where((lane >> 3) == j, blk, 0.0)
            for l in range(LOW_LEVELS, N_LEVELS):
                h = 1 << l
                t0 = j * SUBLANES
                if (t0 // h) % 2 == 1:
                    pair = t0 // (2 * h)
                    r_up = pair * h + (t0 - pair * 2 * h - h)
                    blk = jnp.where(lane < pair * 2 * h + h, a_up[l][r_up:r_up + SUBLANES], blk)
            tiles.append(blk)
        att = jnp.concatenate(tiles, axis=0)

        o = jnp.dot(att.astype(BF16), v, preferred_element_type=F32)
        o += lax.dot_general(qd, st.astype(BF16), nt, preferred_element_type=F32)
        st = st * jnp.exp2(jnp.minimum(b_last, 0.0)) + lax.dot_general(
            v, kd, (((0,), (0,)), ((), ())), preferred_element_type=F32)

        o = o * lax.rsqrt(jnp.mean(o * o, axis=-1, keepdims=True) + EPS) * gn
        y_ref[rows, :] = (o * _silu(zo_ref[rows, :])).astype(y_ref.dtype)
        return st

    st = lax.fori_loop(0, n_chunks, chunk, jnp.zeros((HG_DIM, HG_DIM), F32), unroll=8)
    s_ref[0, 0] = st.T


def _hgrn_prompt(z, lb, gn, *, layer, batch, seq):
    hw = HG_HEADS * HG_DIM
    col = lambda off: pl.BlockSpec((seq, HG_DIM), lambda n, h, off=off: (n, off + h))
    per_head = pl.BlockSpec((None, 1, HG_DIM), lambda n, h: (layer, 0, h))
    tri = jnp.asarray(_prefix_sum_matrix(), BF16)
    return pl.pallas_call(
        _hgrn_prompt_kernel,
        grid=(batch, HG_HEADS),
        in_specs=[col(0), col(HG_HEADS), col(2 * HG_HEADS), col(3 * HG_HEADS), per_head, per_head,
                  pl.BlockSpec(tri.shape, lambda n, h: (0, 0))],
        out_specs=[pl.BlockSpec((seq, HG_DIM), lambda n, h: (n, h)),
                   pl.BlockSpec((1, 1, HG_DIM, HG_DIM), lambda n, h: (n, h, 0, 0))],
        out_shape=[jax.ShapeDtypeStruct((batch * seq, hw), BF16),
                   jax.ShapeDtypeStruct((batch, HG_HEADS, HG_DIM, HG_DIM), F32)],
        compiler_params=_params("parallel", "parallel"),
        name="hgrn_prompt",
    )(z, z, z, z, lb, gn, tri)


def _conv_out_proj_kernel(x_ref, mh_ref, zb_ref, zc_ref, zh_ref, zv_ref, zg_ref, scw_ref, ccw_ref,
                          ccb_ref, lng_ref, lnb_ref, wh_ref, ws_ref, wc_ref,
                          o_ref, ns_ref, nc_ref,
                          u_ref, a_ref, ms_next, mc_next, ms_cur, mc_cur, *, tiles_per_seq):
    s = pl.program_id(0)
    t_blk = zb_ref.shape[0]
    u_halo = SUBLANES
    a_halo = CONV_ROWS

    @pl.when(s == 0)
    def _():
        ms_next[...] = jnp.zeros(ms_next.shape, ms_next.dtype)
        mc_next[...] = jnp.zeros(mc_next.shape, mc_next.dtype)
        u_ref[...] = jnp.zeros(u_ref.shape, u_ref.dtype)
        a_ref[...] = jnp.zeros(a_ref.shape, a_ref.dtype)

    ms_cur[...] = ms_next[...]
    mc_cur[...] = mc_next[...]
    acc = x_ref[...]
    acc += jnp.dot(mh_ref[...], wh_ref[...], preferred_element_type=F32)
    acc += jnp.dot(ms_cur[...], ws_ref[...], preferred_element_type=F32)
    acc += jnp.dot(mc_cur[...], wc_ref[...], preferred_element_type=F32)
    o_ref[...] = acc

    tile = jnp.minimum(s, pl.num_programs(0) - 2)
    seq_start = lax.rem(tile, tiles_per_seq) == 0
    u_ref[0:u_halo, :] = jnp.where(seq_start, 0.0, u_ref[t_blk:t_blk + u_halo, :])
    a_ref[0:a_halo, :] = jnp.where(seq_start, 0.0, a_ref[t_blk:t_blk + a_halo, :])
    u_ref[u_halo:, :] = zc_ref[...] * zh_ref[...]
    a_ref[a_halo:, :] = zv_ref[...] * _sigmoid(zg_ref[...])
    ccb = ccb_ref[...]
    lng = lng_ref[...]
    lnb = lnb_ref[...]
    for i in range(t_blk // CONV_ROWS):
        base = i * CONV_ROWS
        rows = slice(base, base + CONV_ROWS)
        uw = u_ref[base:base + u_halo + CONV_ROWS, :]
        acc = scw_ref[SC_TAPS - 1:SC_TAPS, :] * uw[u_halo:, :]
        for j in range(SC_TAPS - 1):
            off = u_halo - (SC_TAPS - 1) + j
            acc += scw_ref[j:j + 1, :] * uw[off:off + CONV_ROWS, :]
        ms_next[rows, :] = (zb_ref[rows, :] * acc).astype(ms_next.dtype)

        strips = []
        for st in range(a_ref.shape[1] // LANES):
            lanes = slice(st * LANES, (st + 1) * LANES)
            aw = a_ref[base:base + a_halo + CONV_ROWS, lanes]
            acc = ccb[:, lanes]
            for res in range(SUBLANES):
                shifted = aw if res == 0 else pltpu.roll(aw, a_halo + CONV_ROWS - res, axis=0)
                for j in range(CC_TAPS):
                    off = a_halo - (CC_TAPS - 1) + j
                    if off % SUBLANES == res:
                        acc += ccw_ref[j:j + 1, lanes] * shifted[off - res:off - res + CONV_ROWS, :]
            strips.append(acc)
        acc = jnp.concatenate(strips, axis=1)
        mu = jnp.mean(acc, axis=-1, keepdims=True)
        d = acc - mu
        var = jnp.mean(d * d, axis=-1, keepdims=True)
        mc_next[rows, :] = _silu(d * lax.rsqrt(var + EPS) * lng + lnb).astype(mc_next.dtype)

    ns_ref[0] = u_ref[u_halo + t_blk - (SC_TAPS - 1):u_halo + t_blk, :]
    nc_ref[0] = a_ref[a_halo + t_blk - (CC_TAPS - 1):a_halo + t_blk, :]


def _conv_out_proj(x, mh, z, scw, ccw, ccb, lng, lnb, w_out, *, layer, seq, sc_off, cc_off):
    m, d = x.shape
    hw = mh.shape[1]
    w = scw.shape[2]
    t_blk = CONV_TIME_BLOCK
    n_tiles = m // t_blk
    tiles_per_seq = seq // t_blk
    conv_tile = lambda s: jnp.minimum(s, n_tiles - 1)
    proj_tile = lambda s: jnp.maximum(s - 1, 0)
    zcol = lambda off: pl.BlockSpec((t_blk, w), lambda s, off=off: (conv_tile(s), off))
    prow = lambda width: pl.BlockSpec((t_blk, width), lambda s: (proj_tile(s), 0))
    whole = lambda a: _layer_block(a, layer)
    w_rows = lambda rows, blk: pl.BlockSpec((rows, d), lambda s: (blk, 0))
    state = lambda taps: pl.BlockSpec((1, taps - 1, w),
                                      lambda s: (conv_tile(s) // tiles_per_seq, 0, 0))
    return pl.pallas_call(
        functools.partial(_conv_out_proj_kernel, tiles_per_seq=tiles_per_seq),
        grid=(n_tiles + 1,),
        in_specs=[prow(d), prow(hw),
                  zcol(sc_off), zcol(sc_off + 1), zcol(sc_off + 2), zcol(cc_off), zcol(cc_off + 1),
                  whole(scw), whole(ccw), whole(ccb), whole(lng), whole(lnb),
                  w_rows(hw, 0), w_rows(w, hw // w), w_rows(w, hw // w + 1)],
        out_specs=[prow(d), state(SC_TAPS), state(CC_TAPS)],
        out_shape=[jax.ShapeDtypeStruct((m, d), F32),
                   jax.ShapeDtypeStruct((m // seq, SC_TAPS - 1, w), F32),
                   jax.ShapeDtypeStruct((m // seq, CC_TAPS - 1, w), F32)],
        scratch_shapes=[pltpu.VMEM((SUBLANES + t_blk, w), F32),
                        pltpu.VMEM((CONV_ROWS + t_blk, w), F32),
                        pltpu.VMEM((t_blk, w), BF16), pltpu.VMEM((t_blk, w), BF16),
                        pltpu.VMEM((t_blk, w), BF16), pltpu.VMEM((t_blk, w), BF16)],
        compiler_params=_params("arbitrary"),
        name="conv_out_proj",
    )(x, mh, z, z, z, z, z, scw, ccw, ccb, lng, lnb, w_out, w_out, w_out)


def _mix_sample_kernel(*refs):
    so_ref, nc_ref = refs[-5], refs[-3]

    @pl.when(pl.program_id(0) == 0)
    def _():
        _mix_sample_step(*refs[:11], *refs[-6:])

    @pl.when(pl.program_id(0) > 0)
    def _():
        so_ref[...] = jnp.zeros(so_ref.shape, so_ref.dtype)
        nc_ref[...] = jnp.zeros(nc_ref.shape, nc_ref.dtype)


def _mix_sample_step(z_ref, st_ref, ss_ref, sc_ref, lb_ref, gn_ref, scw_ref, ccw_ref, ccb_ref,
                     lng_ref, lnb_ref, mix_ref, so_ref, ns_ref, nc_ref, oh_ref, cv_ref):
    hw = HG_HEADS * HG_DIM
    w = scw_ref.shape[1]
    eye = (lax.broadcasted_iota(jnp.int32, (HG_DIM, HG_DIM), 0)
           == lax.broadcasted_iota(jnp.int32, (HG_DIM, HG_DIM), 1))

    def column(x_row):
        return jnp.sum(jnp.where(eye, x_row, 0.0), axis=1, keepdims=True)

    q_all = _silu(z_ref[:, 0:hw])
    f_all, k_all = _forget_and_key(z_ref[:, hw:2 * hw], *_gate_constants(lb_ref[...]))
    decay_all = jnp.exp(jnp.log(jnp.maximum(f_all, F_FLOOR)))
    v_all = z_ref[:, 2 * hw:3 * hw]
    for t in range(z_ref.shape[0]):
        tr = slice(t, t + 1)
        for h in range(HG_HEADS):
            cols = slice(h * HG_DIM, (h + 1) * HG_DIM)
            s_new = (column(decay_all[tr, cols]) * st_ref[t, h]
                     + column(k_all[tr, cols]) * v_all[tr, cols])
            so_ref[t, h] = s_new
            oh_ref[tr, cols] = jnp.dot(q_all[tr, cols].astype(BF16), s_new.astype(BF16),
                                       preferred_element_type=F32)
    gate_all = _silu(z_ref[:, 3 * hw:4 * hw])
    for h in range(HG_HEADS):
        cols = slice(h * HG_DIM, (h + 1) * HG_DIM)
        o = oh_ref[:, cols]
        o = o * lax.rsqrt(jnp.mean(o * o, axis=-1, keepdims=True) + EPS) * gn_ref[:, cols]
        oh_ref[:, cols] = o * gate_all[:, cols]
    mix_ref[:, 0:hw] = oh_ref[...].astype(mix_ref.dtype)

    off = 4 * hw
    zb = z_ref[:, off:off + w]
    u = z_ref[:, off + w:off + 2 * w] * z_ref[:, off + 2 * w:off + 3 * w]
    conv = scw_ref[SC_TAPS - 1:SC_TAPS, :] * u
    for j in range(SC_TAPS - 1):
        conv += scw_ref[j:j + 1, :] * ss_ref[:, j * w:(j + 1) * w]
    mix_ref[:, hw:hw + w] = (zb * conv).astype(mix_ref.dtype)
    for j in range(SC_TAPS - 2):
        ns_ref[:, j * w:(j + 1) * w] = ss_ref[:, (j + 1) * w:(j + 2) * w]
    ns_ref[:, (SC_TAPS - 2) * w:] = u

    off = 4 * hw + 3 * w
    a = z_ref[:, off:off + w] * _sigmoid(z_ref[:, off + w:off + 2 * w])
    past_taps = ccw_ref[0:CC_TAPS - 1, :]
    for t in range(z_ref.shape[0]):
        cv_ref[t:t + 1, :] = jnp.sum(past_taps * sc_ref[t], axis=0, keepdims=True)
        nc_ref[t, 0:CC_TAPS - 2, :] = sc_ref[t, 1:CC_TAPS - 1, :]
        nc_ref[t, CC_TAPS - 2:CC_TAPS - 1, :] = a[t:t + 1, :]
    conv = ccb_ref[...] + ccw_ref[CC_TAPS - 1:CC_TAPS, :] * a + cv_ref[...]
    mu = jnp.mean(conv, axis=-1, keepdims=True)
    d = conv - mu
    var = jnp.mean(d * d, axis=-1, keepdims=True)
    y = d * lax.rsqrt(var + EPS) * lng_ref[...] + lnb_ref[...]
    mix_ref[:, hw + w:] = _silu(y).astype(mix_ref.dtype)


def _mix_sample(z, st_h, st_s, st_c, lb, gn, scw, ccw, ccb, lng, lnb, *, layer, prev):
    nb = z.shape[0]
    tb = SAMPLE_TOKENS_PER_STEP
    hw = HG_HEADS * HG_DIM
    w = scw.shape[2]
    nblk = nb // tb
    n_pass = st_h.shape[0] - layer if prev is None else 1
    tok = lambda p, i: jnp.where(p == 0, i, nblk - 1)
    rows = lambda width: pl.BlockSpec((tb, width), lambda p, i: (tok(p, i), 0))
    layer_rows = lambda width: pl.BlockSpec((None, tb, width), lambda p, i: (layer, tok(p, i), 0))
    whole = lambda a: _layer_block(a, layer)
    state_tile = (None, tb, HG_HEADS, HG_DIM, HG_DIM)
    conv_tile = (None, tb) + st_c.shape[2:]
    args = [z, st_h, st_s, st_c, lb, gn, scw, ccw, ccb, lng, lnb]
    in_specs = [rows(z.shape[1]),
                pl.BlockSpec(state_tile, lambda p, i: (layer, tok(p, i), 0, 0, 0)),
                layer_rows(st_s.shape[2]),
                pl.BlockSpec(conv_tile, lambda p, i: (layer, tok(p, i), 0, 0)),
                whole(lb), whole(gn), whole(scw), whole(ccw), whole(ccb), whole(lng), whole(lnb)]
    aliases = {}
    if prev is not None:
        aliases = {len(args): 1, len(args) + 1: 3}
        args += list(prev)
        in_specs += [pl.BlockSpec(memory_space=pl.ANY)] * 2
    return pl.pallas_call(
        _mix_sample_kernel,
        grid=(n_pass, nblk),
        in_specs=in_specs,
        out_specs=[rows(hw + 2 * w),
                   pl.BlockSpec(state_tile, lambda p, i: (layer + p, i, 0, 0, 0)),
                   rows(st_s.shape[2]),
                   pl.BlockSpec(conv_tile, lambda p, i: (layer + p, i, 0, 0))],
        out_shape=[jax.ShapeDtypeStruct((nb, hw + 2 * w), BF16),
                   jax.ShapeDtypeStruct(st_h.shape, F32),
                   jax.ShapeDtypeStruct((nb, st_s.shape[2]), F32),
                   jax.ShapeDtypeStruct(st_c.shape, F32)],
        scratch_shapes=[pltpu.VMEM((tb, hw), F32), pltpu.VMEM((tb, w), F32)],
        input_output_aliases=aliases,
        compiler_params=_params("arbitrary", "arbitrary"),
        name="mix_sample",
    )(*args)


def kernel(x_prompt, x_sample, state_hgrn, state_sconv, state_cconv, g_mix, w_in, hgrn_lb,
           hgrn_norm_g, sconv_w, cconv_w, cconv_b, cconv_ln_g, cconv_ln_b, w_out, g_mlp,
           w_up, w_down, g_final):
    batch, seq, d = x_prompt.shape
    nb = x_sample.shape[0]
    depth = w_in.shape[0]
    hw = HG_HEADS * HG_DIM
    w = sconv_w.shape[2]
    assert state_hgrn.shape[2:] == (HG_HEADS, HG_DIM, HG_DIM)
    assert w_in.shape[2] == 4 * hw + 5 * w and w_out.shape[1] == hw + 2 * w
    assert sconv_w.shape[1] == SC_TAPS and cconv_w.shape[1] == CC_TAPS
    assert seq % CONV_TIME_BLOCK == 0 and seq % CHUNK == 0 and nb % SAMPLE_TOKENS_PER_STEP == 0

    p = jax.nn.softmax(hgrn_lb.astype(F32), axis=0)
    lb_all = jnp.cumsum(p, axis=0) - p[0:1]

    rows = lambda a: a.reshape(depth, 1, -1)
    g1, g2, gf = rows(g_mix), rows(g_mlp), g_final.reshape(1, -1)
    lb, gn = rows(lb_all), rows(hgrn_norm_g)
    ccb, lng, lnb = rows(cconv_b), rows(cconv_ln_g), rows(cconv_ln_b)

    xp = x_prompt.reshape(batch * seq, d)
    xs = x_sample.reshape(nb, d)
    st_s = state_sconv.reshape(depth, nb, (SC_TAPS - 1) * w)
    sc_off = 4 * hw // w
    cc_off = sc_off + 3

    ph, ps, pc, ss = [], [], [], []
    sample_states = None
    for li in range(depth):
        last = li == depth - 1

        if li == 0:
            zs, w_in_b = _norm_matmul_cast(xs, g1, w_in, layer=li, tn=WEIGHT_BLOCK)
        else:
            zs, = _norm_matmul(xs, g1, w_in_b, layer=li, tm=nb)
        mix, new_h, new_s, new_c = _mix_sample(
            zs, state_hgrn, st_s, state_cconv, lb, gn, sconv_w, cconv_w, ccb, lng, lnb, layer=li,
            prev=sample_states)
        sample_states = (new_h, new_c)
        ss.append(new_s.reshape(nb, SC_TAPS - 1, w))

        jobs = [_CastJob(w_out, li, None), _CastJob(w_up, li, WEIGHT_BLOCK),
                _CastJob(w_down, li, None)] if li == 0 else []
        z, *cast = _norm_matmul(xp, g1, w_in_b, layer=li, tm=1024, cast_jobs=jobs)
        if li == 0:
            w_out_b, w_up_b, w_down_b = cast

        xs = _out_proj_sample(xs, mix, w_out_b, tk=WEIGHT_BLOCK)
        xs, = _ffn(xs, g2, w_up_b, w_down_b, gf, layer=li, tm=nb, final_norm=last)

        mh, new_h = _hgrn_prompt(z, lb, gn, layer=li, batch=batch, seq=seq)
        xp, new_s, new_c = _conv_out_proj(xp, mh, z, sconv_w, cconv_w, ccb, lng, lnb, w_out_b,
                                          layer=li, seq=seq, sc_off=sc_off, cc_off=cc_off)
        jobs = [] if last else [_CastJob(w_in, li + 1, WEIGHT_BLOCK), _CastJob(w_out, li + 1, None),
                                _CastJob(w_up, li + 1, WEIGHT_BLOCK), _CastJob(w_down, li + 1, None)]
        xp, *cast = _ffn(xp, g2, w_up_b, w_down_b, gf, layer=li, tm=1024, final_norm=last,
                         cast_jobs=jobs)
        if not last:
            w_in_b, w_out_b, w_up_b, w_down_b = cast
        ph.append(new_h)
        ps.append(new_s)
        pc.append(new_c)

    sh, sc = sample_states
    return (xp.reshape(batch, seq, d), xs.reshape(nb, 1, d), jnp.stack(ph), jnp.stack(ps),
            jnp.stack(pc), sh, jnp.stack(ss), sc)
```

```python
import functools
from typing import NamedTuple, Optional

import numpy as np
import jax
import jax.numpy as jnp
from jax import lax
from jax.experimental import pallas as pl
from jax.experimental.pallas import tpu as pltpu

F32 = jnp.float32
BF16 = jnp.bfloat16

EPS = 1e-6
F_FLOOR = 1e-30

LANES = 128
SUBLANES = 8
BF16_SUBLANES = 16
VMEM_LIMIT_BYTES = 56 * 1024 * 1024

HG_HEADS = 8
HG_DIM = 128
SC_TAPS = 3
CC_TAPS = 31
CHUNK = 128
N_LEVELS = 7
SAMPLE_TOKENS_PER_STEP = 8
WEIGHT_BLOCK = 512
CONV_TIME_BLOCK = 512
CONV_ROWS = 32


def _params(*sem):
    return pltpu.CompilerParams(dimension_semantics=sem, vmem_limit_bytes=VMEM_LIMIT_BYTES)


def _layer_block(a, layer):
    zeros = (0,) * (a.ndim - 1)
    return pl.BlockSpec((None,) + a.shape[1:], lambda *_: (layer,) + zeros)


def _rmsnorm(x, g):
    return x * lax.rsqrt(jnp.mean(x * x, axis=-1, keepdims=True) + EPS) * g


def _sigmoid(x):
    return 0.5 * jnp.tanh(0.5 * x) + 0.5


def _silu(x):
    h = 0.5 * x
    return h + h * jnp.tanh(h)


def _gate_constants(lb):
    f_scale = 0.5 * (1.0 - lb)
    return f_scale, lb + f_scale


def _forget_and_key(z, f_scale, f_shift):
    w = f_scale * jnp.tanh(0.5 * z)
    return f_shift + w, f_scale - w


class _CastJob(NamedTuple):
    w: jax.Array
    layer: int
    col_block: Optional[int]


def _cast_job_plan(job, n_steps, step_of):
    _, rows, cols = job.w.shape
    chunk_rows = BF16_SUBLANES
    while rows % chunk_rows or rows // chunk_rows > n_steps:
        chunk_rows += BF16_SUBLANES
    n_chunks = rows // chunk_rows
    chunk = lambda *idx: jnp.minimum(step_of(*idx), n_chunks - 1)
    in_spec = pl.BlockSpec((None, chunk_rows, cols), lambda *idx: (job.layer, chunk(*idx), 0))
    if job.col_block is None:
        out_spec = pl.BlockSpec((chunk_rows, cols), lambda *idx: (chunk(*idx), 0))
        return in_spec, out_spec, jax.ShapeDtypeStruct((rows, cols), BF16)
    nblk = cols // job.col_block
    out_spec = pl.BlockSpec((nblk, chunk_rows, job.col_block), lambda *idx: (0, chunk(*idx), 0))
    return in_spec, out_spec, jax.ShapeDtypeStruct((nblk, rows, job.col_block), BF16)


def _run_cast_jobs(src_refs, dst_refs):
    for src, dst in zip(src_refs, dst_refs):
        if len(dst.shape) == 2:
            dst[...] = src[...].astype(dst.dtype)
        else:
            width = dst.shape[2]
            for c in range(dst.shape[0]):
                dst[c] = src[:, c * width:(c + 1) * width].astype(dst.dtype)


def _norm_matmul_kernel(x_ref, g_ref, w_ref, *rest, n_jobs):
    cast_src, o_ref, cast_dst, h_ref = rest[:n_jobs], rest[n_jobs], rest[n_jobs + 1:-1], rest[-1]

    @pl.when(pl.program_id(1) == 0)
    def _():
        h_ref[...] = _rmsnorm(x_ref[...], g_ref[...]).astype(BF16)

    o_ref[...] = jnp.dot(h_ref[...], w_ref[...], preferred_element_type=F32)
    _run_cast_jobs(cast_src, cast_dst)


def _norm_matmul(x, g, wb, *, layer, tm, cast_jobs=()):
    m, k = x.shape
    nblk, _, tn = wb.shape
    grid = (m // tm, nblk)
    plans = [_cast_job_plan(job, grid[0] * grid[1], lambda i, j: i * nblk + j)
             for job in cast_jobs]
    return pl.pallas_call(
        functools.partial(_norm_matmul_kernel, n_jobs=len(plans)),
        grid=grid,
        in_specs=[pl.BlockSpec((tm, k), lambda i, j: (i, 0)),
                  _layer_block(g, layer),
                  pl.BlockSpec((None, k, tn), lambda i, j: (j, 0, 0))] + [p[0] for p in plans],
        out_specs=[pl.BlockSpec((None, tm, tn), lambda i, j: (j, i, 0))] + [p[1] for p in plans],
        out_shape=[jax.ShapeDtypeStruct((nblk, m, tn), F32)] + [p[2] for p in plans],
        scratch_shapes=[pltpu.VMEM((tm, k), BF16)],
        compiler_params=_params("arbitrary", "arbitrary"),
        name="in_proj",
    )(x, g, wb, *[job.w for job in cast_jobs])


def _norm_matmul_cast_kernel(x_ref, g_ref, w_ref, o_ref, wb_ref, h_ref):
    @pl.when(pl.program_id(0) == 0)
    def _():
        h_ref[...] = _rmsnorm(x_ref[...], g_ref[...]).astype(BF16)

    wb = w_ref[...].astype(BF16)
    wb_ref[...] = wb
    o_ref[...] = jnp.dot(h_ref[...], wb, preferred_element_type=F32)


def _norm_matmul_cast(x, g, w, *, layer, tn):
    m, k = x.shape
    n = w.shape[2]
    return pl.pallas_call(
        _norm_matmul_cast_kernel,
        grid=(n // tn,),
        in_specs=[pl.BlockSpec((m, k), lambda j: (0, 0)),
                  _layer_block(g, layer),
                  pl.BlockSpec((None, k, tn), lambda j: (layer, 0, j))],
        out_specs=[pl.BlockSpec((None, m, tn), lambda j: (j, 0, 0)),
                   pl.BlockSpec((None, k, tn), lambda j: (j, 0, 0))],
        out_shape=[jax.ShapeDtypeStruct((n // tn, m, tn), F32),
                   jax.ShapeDtypeStruct((n // tn, k, tn), BF16)],
        scratch_shapes=[pltpu.VMEM((m, k), BF16)],
        compiler_params=_params("arbitrary"),
        name="in_proj_cast",
    )(x, g, w)


def _out_proj_sample_kernel(x_ref, m_ref, w_ref, o_ref):
    @pl.when(pl.program_id(0) == 0)
    def _():
        o_ref[...] = x_ref[...]

    o_ref[...] += jnp.dot(m_ref[...], w_ref[...], preferred_element_type=F32)


def _out_proj_sample(x, mix, w_out_b, *, tk):
    m, d = x.shape
    kdim = mix.shape[1]
    return pl.pallas_call(
        _out_proj_sample_kernel,
        grid=(kdim // tk,),
        in_specs=[pl.BlockSpec((m, d), lambda r: (0, 0)),
                  pl.BlockSpec((m, tk), lambda r: (0, r)),
                  pl.BlockSpec((tk, d), lambda r: (r, 0))],
        out_specs=pl.BlockSpec((m, d), lambda r: (0, 0)),
        out_shape=jax.ShapeDtypeStruct((m, d), F32),
        compiler_params=_params("arbitrary"),
        name="out_proj_sample",
    )(x, mix, w_out_b)


def _ffn_kernel(x_ref, g_ref, wu_ref, wd_ref, gf_ref, *rest, final_norm, n_jobs):
    cast_src, o_ref, cast_dst, h_ref = rest[:n_jobs], rest[n_jobs], rest[n_jobs + 1:-1], rest[-1]
    j = pl.program_id(1)

    @pl.when(j == 0)
    def _():
        x = x_ref[...]
        h_ref[...] = _rmsnorm(x, g_ref[...]).astype(BF16)
        o_ref[...] = x

    u = jnp.dot(h_ref[...], wu_ref[...], preferred_element_type=F32)
    r = jnp.maximum(u, 0.0)
    o_ref[...] += jnp.dot((r * r).astype(BF16), wd_ref[...], preferred_element_type=F32)
    _run_cast_jobs(cast_src, cast_dst)

    if final_norm:
        @pl.when(j == pl.num_programs(1) - 1)
        def _():
            o_ref[...] = _rmsnorm(o_ref[...], gf_ref[...])


def _ffn(x, g, wub, wdb, gf, *, layer, tm, final_norm, cast_jobs=()):
    m, d = x.shape
    nblk, _, tf = wub.shape
    grid = (m // tm, nblk)
    plans = [_cast_job_plan(job, grid[0] * grid[1], lambda i, j: i * nblk + j)
             for job in cast_jobs]
    return pl.pallas_call(
        functools.partial(_ffn_kernel, final_norm=final_norm, n_jobs=len(plans)),
        grid=grid,
        in_specs=[pl.BlockSpec((tm, d), lambda i, j: (i, 0)),
                  _layer_block(g, layer),
                  pl.BlockSpec((None, d, tf), lambda i, j: (j, 0, 0)),
                  pl.BlockSpec((tf, d), lambda i, j: (j, 0)),
                  pl.BlockSpec((1, d), lambda i, j: (0, 0))] + [p[0] for p in plans],
        out_specs=[pl.BlockSpec((tm, d), lambda i, j: (i, 0))] + [p[1] for p in plans],
        out_shape=[jax.ShapeDtypeStruct((m, d), F32)] + [p[2] for p in plans],
        scratch_shapes=[pltpu.VMEM((tm, d), BF16)],
        compiler_params=_params("arbitrary", "arbitrary"),
        name="ffn",
    )(x, g, wub, wdb, gf, *[job.w for job in cast_jobs])


LOG2E = 1.4426950408889634
LOW_LEVELS = 3


def _prefix_sum_matrix():
    t = np.arange(CHUNK)[:, None]
    r = np.arange(CHUNK)[None, :]
    return (r <= t).astype(np.float32)


def _hgrn_prompt_kernel(zq_ref, zf_ref, zi_ref, zo_ref, lb_ref, gn_ref, tri_ref, y_ref, s_ref):
    n_chunks = zq_ref.shape[0] // CHUNK
    n_tiles = CHUNK // SUBLANES
    lb = lb_ref[...]
    gn = gn_ref[...]
    tri = tri_ref[...]
    f_scale, f_shift = _gate_constants(lb)
    sub = lax.broadcasted_iota(jnp.int32, (SUBLANES, HG_DIM), 0)
    lane = lax.broadcasted_iota(jnp.int32, (SUBLANES, HG_DIM), 1)
    col = lane & (SUBLANES - 1)
    below = col < sub
    pat_low = [below & (((sub ^ col) >> l) == 1) for l in range(LOW_LEVELS)]
    pat_diag = col == sub
    upper_low = [((sub >> l) & 1) == 1 for l in range(LOW_LEVELS)]
    nt = (((1,), (1,)), ((), ()))
    tiled = lambda x: x.reshape(n_tiles, SUBLANES, HG_DIM)
    flat = lambda x: x.reshape(CHUNK, HG_DIM)

    def tile_row(x3, s):
        return jnp.broadcast_to(x3[:, s:s + 1, :], x3.shape)

    def chunk(c, st):
        rows = pl.ds(pl.multiple_of(c * CHUNK, CHUNK), CHUNK)
        q = _silu(zq_ref[rows, :])
        f, kk = _forget_and_key(zf_ref[rows, :], f_scale, f_shift)
        g = jnp.log(jnp.maximum(f, F_FLOOR)) * LOG2E
        v = zi_ref[rows, :].astype(BF16)

        g_hi = g.astype(BF16)
        g_lo = (g - g_hi.astype(F32)).astype(BF16)
        b2 = jnp.dot(tri, jnp.concatenate([g_hi, g_lo], axis=1), preferred_element_type=F32)
        b = b2[:, :HG_DIM] + b2[:, HG_DIM:]
        b_last = b[CHUNK - 1:CHUNK, :]
        qd = (q * jnp.exp2(jnp.minimum(b, 0.0))).astype(BF16)
        kd = (kk * jnp.exp2(jnp.minimum(b_last - b, 0.0))).astype(BF16)

        half = SUBLANES // 2
        b3, g3, q3, k3 = tiled(b), tiled(g), tiled(q), tiled(kk)
        ref_low = [None,
                   jnp.where(sub < half, tile_row(b3, 1), tile_row(b3, half + 1)),
                   tile_row(b3, half - 1)]
        a_low = []
        for l in range(LOW_LEVELS):
            upper = upper_low[l]
            nd = jnp.where(upper, g3, 0.0) if l == 0 else -jnp.abs(b3 - ref_low[l])
            x = flat(jnp.exp2(nd) * jnp.where(upper, q3, k3)).astype(BF16)
            a_low.append(lax.dot_general(x, x, nt, preferred_element_type=F32))

        a_up = {}
        for l in range(LOW_LEVELS, N_LEVELS):
            h = 1 << l
            pairs = CHUNK // (2 * h)
            bg = b.reshape(pairs, 2, h, HG_DIM)
            ref = bg[:, 0, h - 1:h, :]
            x_lo = kk.reshape(pairs, 2, h, HG_DIM)[:, 0] * jnp.exp2(jnp.minimum(ref - bg[:, 0], 0.0))
            x_up = q.reshape(pairs, 2, h, HG_DIM)[:, 1] * jnp.exp2(jnp.minimum(bg[:, 1] - ref, 0.0))
            x_all = jnp.stack([x_lo, x_up], axis=1).reshape(CHUNK, HG_DIM).astype(BF16)
            a_up[l] = lax.dot_general(x_up.reshape(CHUNK // 2, HG_DIM).astype(BF16), x_all, nt,
                                      preferred_element_type=F32)

        dqk = jnp.sum(q * kk, axis=1, keepdims=True)
        tiles = []
        for j in range(n_tiles):
            r = slice(j * SUBLANES, (j + 1) * SUBLANES)
            blk = jnp.where(pat_diag, dqk[r], 0.0)
            for l in range(LOW_LEVELS):
                blk = jnp.where(pat_low[l], a_low[l][r], blk)
            blk = jnp.where((lane >> 3) == j, blk, 0.0)
            for l in range(LOW_LEVELS, N_LEVELS):
                h = 1 << l
                t0 = j * SUBLANES
                if (t0 // h) % 2 == 1:
                    pair = t0 // (2 * h)
                    r_up = pair * h + (t0 - pair * 2 * h - h)
                    blk = jnp.where(lane < pair * 2 * h + h, a_up[l][r_up:r_up + SUBLANES], blk)
            tiles.append(blk)
        att = jnp.concatenate(tiles, axis=0)

        o = jnp.dot(att.astype(BF16), v, preferred_element_type=F32)
        o += lax.dot_general(qd, st.astype(BF16), nt, preferred_element_type=F32)
        st = st * jnp.exp2(jnp.minimum(b_last, 0.0)) + lax.dot_general(
            v, kd, (((0,), (0,)), ((), ())), preferred_element_type=F32)

        o = o * lax.rsqrt(jnp.mean(o * o, axis=-1, keepdims=True) + EPS) * gn
        y_ref[rows, :] = (o * _silu(zo_ref[rows, :])).astype(y_ref.dtype)
        return st

    st = lax.fori_loop(0, n_chunks, chunk, jnp.zeros((HG_DIM, HG_DIM), F32), unroll=8)
    s_ref[0, 0] = st.T


def _hgrn_prompt(z, lb, gn, *, layer, batch, seq):
    hw = HG_HEADS * HG_DIM
    per_blk = z.shape[2] // HG_DIM
    col = lambda off: pl.BlockSpec(
        (None, seq, HG_DIM), lambda n, h, off=off: ((off + h) // per_blk, n, (off + h) % per_blk))
    per_head = pl.BlockSpec((None, 1, HG_DIM), lambda n, h: (layer, 0, h))
    tri = jnp.asarray(_prefix_sum_matrix(), BF16)
    return pl.pallas_call(
        _hgrn_prompt_kernel,
        grid=(batch, HG_HEADS),
        in_specs=[col(0), col(HG_HEADS), col(2 * HG_HEADS), col(3 * HG_HEADS), per_head, per_head,
                  pl.BlockSpec(tri.shape, lambda n, h: (0, 0))],
        out_specs=[pl.BlockSpec((seq, HG_DIM), lambda n, h: (n, h)),
                   pl.BlockSpec((1, 1, HG_DIM, HG_DIM), lambda n, h: (n, h, 0, 0))],
        out_shape=[jax.ShapeDtypeStruct((batch * seq, hw), BF16),
                   jax.ShapeDtypeStruct((batch, HG_HEADS, HG_DIM, HG_DIM), F32)],
        compiler_params=_params("parallel", "parallel"),
        name="hgrn_prompt",
    )(z, z, z, z, lb, gn, tri)


def _conv_out_proj_kernel(x_ref, mh_ref, zb_ref, zc_ref, zh_ref, zv_ref, zg_ref, scw_ref, ccw_ref,
                          ccb_ref, lng_ref, lnb_ref, wh_ref, ws_ref, wc_ref,
                          o_ref, ns_ref, nc_ref,
                          u_ref, a_ref, ms_next, mc_next, ms_cur, mc_cur, *, tiles_per_seq):
    s = pl.program_id(0)
    t_blk = zb_ref.shape[0]
    u_halo = SUBLANES
    a_halo = CONV_ROWS

    @pl.when(s == 0)
    def _():
        ms_next[...] = jnp.zeros(ms_next.shape, ms_next.dtype)
        mc_next[...] = jnp.zeros(mc_next.shape, mc_next.dtype)
        u_ref[...] = jnp.zeros(u_ref.shape, u_ref.dtype)
        a_ref[...] = jnp.zeros(a_ref.shape, a_ref.dtype)

    ms_cur[...] = ms_next[...]
    mc_cur[...] = mc_next[...]
    acc = x_ref[...]
    acc += jnp.dot(mh_ref[...], wh_ref[...], preferred_element_type=F32)
    acc += jnp.dot(ms_cur[...], ws_ref[...], preferred_element_type=F32)
    acc += jnp.dot(mc_cur[...], wc_ref[...], preferred_element_type=F32)
    o_ref[...] = acc

    tile = jnp.minimum(s, pl.num_programs(0) - 2)
    seq_start = lax.rem(tile, tiles_per_seq) == 0
    u_ref[0:u_halo, :] = jnp.where(seq_start, 0.0, u_ref[t_blk:t_blk + u_halo, :])
    a_ref[0:a_halo, :] = jnp.where(seq_start, 0.0, a_ref[t_blk:t_blk + a_halo, :])
    u_ref[u_halo:, :] = zc_ref[...] * zh_ref[...]
    a_ref[a_halo:, :] = zv_ref[...] * _sigmoid(zg_ref[...])
    ccb = ccb_ref[...]
    lng = lng_ref[...]
    lnb = lnb_ref[...]
    for i in range(t_blk // CONV_ROWS):
        base = i * CONV_ROWS
        rows = slice(base, base + CONV_ROWS)
        uw = u_ref[base:base + u_halo + CONV_ROWS, :]
        acc = scw_ref[SC_TAPS - 1:SC_TAPS, :] * uw[u_halo:, :]
        for j in range(SC_TAPS - 1):
            off = u_halo - (SC_TAPS - 1) + j
            acc += scw_ref[j:j + 1, :] * uw[off:off + CONV_ROWS, :]
        ms_next[rows, :] = (zb_ref[rows, :] * acc).astype(ms_next.dtype)

        strips = []
        for st in range(a_ref.shape[1] // LANES):
            lanes = slice(st * LANES, (st + 1) * LANES)
            aw = a_ref[base:base + a_halo + CONV_ROWS, lanes]
            acc = ccb[:, lanes]
            for res in range(SUBLANES):
                shifted = aw if res == 0 else pltpu.roll(aw, a_halo + CONV_ROWS - res, axis=0)
                for j in range(CC_TAPS):
                    off = a_halo - (CC_TAPS - 1) + j
                    if off % SUBLANES == res:
                        acc += ccw_ref[j:j + 1, lanes] * shifted[off - res:off - res + CONV_ROWS, :]
            strips.append(acc)
        acc = jnp.concatenate(strips, axis=1)
        mu = jnp.mean(acc, axis=-1, keepdims=True)
        d = acc - mu
        var = jnp.mean(d * d, axis=-1, keepdims=True)
        mc_next[rows, :] = _silu(d * lax.rsqrt(var + EPS) * lng + lnb).astype(mc_next.dtype)

    ns_ref[0] = u_ref[u_halo + t_blk - (SC_TAPS - 1):u_halo + t_blk, :]
    nc_ref[0] = a_ref[a_halo + t_blk - (CC_TAPS - 1):a_halo + t_blk, :]


def _conv_out_proj(x, mh, z, scw, ccw, ccb, lng, lnb, w_out, *, layer, seq, sc_off, cc_off):
    m, d = x.shape
    hw = mh.shape[1]
    w = scw.shape[2]
    t_blk = CONV_TIME_BLOCK
    n_tiles = m // t_blk
    tiles_per_seq = seq // t_blk
    conv_tile = lambda s: jnp.minimum(s, n_tiles - 1)
    proj_tile = lambda s: jnp.maximum(s - 1, 0)
    assert z.shape[2] == w
    zcol = lambda off: pl.BlockSpec((None, t_blk, w), lambda s, off=off: (off, conv_tile(s), 0))
    prow = lambda width: pl.BlockSpec((t_blk, width), lambda s: (proj_tile(s), 0))
    whole = lambda a: _layer_block(a, layer)
    w_rows = lambda rows, blk: pl.BlockSpec((rows, d), lambda s: (blk, 0))
    state = lambda taps: pl.BlockSpec((1, taps - 1, w),
                                      lambda s: (conv_tile(s) // tiles_per_seq, 0, 0))
    return pl.pallas_call(
        functools.partial(_conv_out_proj_kernel, tiles_per_seq=tiles_per_seq),
        grid=(n_tiles + 1,),
        in_specs=[prow(d), prow(hw),
                  zcol(sc_off), zcol(sc_off + 1), zcol(sc_off + 2), zcol(cc_off), zcol(cc_off + 1),
                  whole(scw), whole(ccw), whole(ccb), whole(lng), whole(lnb),
                  w_rows(hw, 0), w_rows(w, hw // w), w_rows(w, hw // w + 1)],
        out_specs=[prow(d), state(SC_TAPS), state(CC_TAPS)],
        out_shape=[jax.ShapeDtypeStruct((m, d), F32),
                   jax.ShapeDtypeStruct((m // seq, SC_TAPS - 1, w), F32),
                   jax.ShapeDtypeStruct((m // seq, CC_TAPS - 1, w), F32)],
        scratch_shapes=[pltpu.VMEM((SUBLANES + t_blk, w), F32),
                        pltpu.VMEM((CONV_ROWS + t_blk, w), F32),
                        pltpu.VMEM((t_blk, w), BF16), pltpu.VMEM((t_blk, w), BF16),
                        pltpu.VMEM((t_blk, w), BF16), pltpu.VMEM((t_blk, w), BF16)],
        compiler_params=_params("arbitrary"),
        name="conv_out_proj",
    )(x, mh, z, z, z, z, z, scw, ccw, ccb, lng, lnb, w_out, w_out, w_out)


def _mix_sample_kernel(*refs):
    so_ref, nc_ref = refs[-5], refs[-3]

    @pl.when(pl.program_id(0) == 0)
    def _():
        _mix_sample_step(*refs[:11], *refs[-6:])

    @pl.when(pl.program_id(0) > 0)
    def _():
        so_ref[...] = jnp.zeros(so_ref.shape, so_ref.dtype)
        nc_ref[...] = jnp.zeros(nc_ref.shape, nc_ref.dtype)


def _mix_sample_step(z_ref, st_ref, ss_ref, sc_ref, lb_ref, gn_ref, scw_ref, ccw_ref, ccb_ref,
                     lng_ref, lnb_ref, mix_ref, so_ref, ns_ref, nc_ref, oh_ref, cv_ref):
    hw = HG_HEADS * HG_DIM
    w = scw_ref.shape[1]
    eye = (lax.broadcasted_iota(jnp.int32, (HG_DIM, HG_DIM), 0)
           == lax.broadcasted_iota(jnp.int32, (HG_DIM, HG_DIM), 1))

    def column(x_row):
        return jnp.sum(jnp.where(eye, x_row, 0.0), axis=1, keepdims=True)

    n_tok = z_ref.shape[1]
    zw = z_ref.shape[2]

    def zcols(lo, hi):
        parts = [z_ref[c, :, max(lo - c * zw, 0):min(hi - c * zw, zw)]
                 for c in range(lo // zw, (hi - 1) // zw + 1)]
        return parts[0] if len(parts) == 1 else jnp.concatenate(parts, axis=1)

    q_all = _silu(zcols(0, hw))
    f_all, k_all = _forget_and_key(zcols(hw, 2 * hw), *_gate_constants(lb_ref[...]))
    decay_all = jnp.exp(jnp.log(jnp.maximum(f_all, F_FLOOR)))
    v_all = zcols(2 * hw, 3 * hw)
    for t in range(n_tok):
        tr = slice(t, t + 1)
        for h in range(HG_HEADS):
            cols = slice(h * HG_DIM, (h + 1) * HG_DIM)
            s_new = (column(decay_all[tr, cols]) * st_ref[t, h]
                     + column(k_all[tr, cols]) * v_all[tr, cols])
            so_ref[t, h] = s_new
            oh_ref[tr, cols] = jnp.dot(q_all[tr, cols].astype(BF16), s_new.astype(BF16),
                                       preferred_element_type=F32)
    gate_all = _silu(zcols(3 * hw, 4 * hw))
    for h in range(HG_HEADS):
        cols = slice(h * HG_DIM, (h + 1) * HG_DIM)
        o = oh_ref[:, cols]
        o = o * lax.rsqrt(jnp.mean(o * o, axis=-1, keepdims=True) + EPS) * gn_ref[:, cols]
        oh_ref[:, cols] = o * gate_all[:, cols]
    mix_ref[:, 0:hw] = oh_ref[...].astype(mix_ref.dtype)

    off = 4 * hw
    zb = zcols(off, off + w)
    u = zcols(off + w, off + 2 * w) * zcols(off + 2 * w, off + 3 * w)
    conv = scw_ref[SC_TAPS - 1:SC_TAPS, :] * u
    for j in range(SC_TAPS - 1):
        conv += scw_ref[j:j + 1, :] * ss_ref[:, j * w:(j + 1) * w]
    mix_ref[:, hw:hw + w] = (zb * conv).astype(mix_ref.dtype)
    for j in range(SC_TAPS - 2):
        ns_ref[:, j * w:(j + 1) * w] = ss_ref[:, (j + 1) * w:(j + 2) * w]
    ns_ref[:, (SC_TAPS - 2) * w:] = u

    off = 4 * hw + 3 * w
    a = zcols(off, off + w) * _sigmoid(zcols(off + w, off + 2 * w))
    past_taps = ccw_ref[0:CC_TAPS - 1, :]
    for t in range(n_tok):
        cv_ref[t:t + 1, :] = jnp.sum(past_taps * sc_ref[t], axis=0, keepdims=True)
        nc_ref[t, 0:CC_TAPS - 2, :] = sc_ref[t, 1:CC_TAPS - 1, :]
        nc_ref[t, CC_TAPS - 2:CC_TAPS - 1, :] = a[t:t + 1, :]
    conv = ccb_ref[...] + ccw_ref[CC_TAPS - 1:CC_TAPS, :] * a + cv_ref[...]
    mu = jnp.mean(conv, axis=-1, keepdims=True)
    d = conv - mu
    var = jnp.mean(d * d, axis=-1, keepdims=True)
    y = d * lax.rsqrt(var + EPS) * lng_ref[...] + lnb_ref[...]
    mix_ref[:, hw + w:] = _silu(y).astype(mix_ref.dtype)


def _mix_sample(z, st_h, st_s, st_c, lb, gn, scw, ccw, ccb, lng, lnb, *, layer, prev):
    nb = z.shape[1]
    tb = SAMPLE_TOKENS_PER_STEP
    hw = HG_HEADS * HG_DIM
    w = scw.shape[2]
    nblk = nb // tb
    n_pass = st_h.shape[0] - layer if prev is None else 1
    tok = lambda p, i: jnp.where(p == 0, i, nblk - 1)
    rows = lambda width: pl.BlockSpec((tb, width), lambda p, i: (tok(p, i), 0))
    layer_rows = lambda width: pl.BlockSpec((None, tb, width), lambda p, i: (layer, tok(p, i), 0))
    whole = lambda a: _layer_block(a, layer)
    state_tile = (None, tb, HG_HEADS, HG_DIM, HG_DIM)
    conv_tile = (None, tb) + st_c.shape[2:]
    args = [z, st_h, st_s, st_c, lb, gn, scw, ccw, ccb, lng, lnb]
    in_specs = [pl.BlockSpec((z.shape[0], tb, z.shape[2]), lambda p, i: (0, tok(p, i), 0)),
                pl.BlockSpec(state_tile, lambda p, i: (layer, tok(p, i), 0, 0, 0)),
                layer_rows(st_s.shape[2]),
                pl.BlockSpec(conv_tile, lambda p, i: (layer, tok(p, i), 0, 0)),
                whole(lb), whole(gn), whole(scw), whole(ccw), whole(ccb), whole(lng), whole(lnb)]
    aliases = {}
    if prev is not None:
        aliases = {len(args): 1, len(args) + 1: 3}
        args += list(prev)
        in_specs += [pl.BlockSpec(memory_space=pl.ANY)] * 2
    return pl.pallas_call(
        _mix_sample_kernel,
        grid=(n_pass, nblk),
        in_specs=in_specs,
        out_specs=[rows(hw + 2 * w),
                   pl.BlockSpec(state_tile, lambda p, i: (layer + p, i, 0, 0, 0)),
                   rows(st_s.shape[2]),
                   pl.BlockSpec(conv_tile, lambda p, i: (layer + p, i, 0, 0))],
        out_shape=[jax.ShapeDtypeStruct((nb, hw + 2 * w), BF16),
                   jax.ShapeDtypeStruct(st_h.shape, F32),
                   jax.ShapeDtypeStruct((nb, st_s.shape[2]), F32),
                   jax.ShapeDtypeStruct(st_c.shape, F32)],
        scratch_shapes=[pltpu.VMEM((tb, hw), F32), pltpu.VMEM((tb, w), F32)],
        input_output_aliases=aliases,
        compiler_params=_params("arbitrary", "arbitrary"),
        name="mix_sample",
    )(*args)


def kernel(x_prompt, x_sample, state_hgrn, state_sconv, state_cconv, g_mix, w_in, hgrn_lb,
           hgrn_norm_g, sconv_w, cconv_w, cconv_b, cconv_ln_g, cconv_ln_b, w_out, g_mlp,
           w_up, w_down, g_final):
    batch, seq, d = x_prompt.shape
    nb = x_sample.shape[0]
    depth = w_in.shape[0]
    hw = HG_HEADS * HG_DIM
    w = sconv_w.shape[2]
    assert state_hgrn.shape[2:] == (HG_HEADS, HG_DIM, HG_DIM)
    assert w_in.shape[2] == 4 * hw + 5 * w and w_out.shape[1] == hw + 2 * w
    assert sconv_w.shape[1] == SC_TAPS and cconv_w.shape[1] == CC_TAPS
    assert seq % CONV_TIME_BLOCK == 0 and seq % CHUNK == 0 and nb % SAMPLE_TOKENS_PER_STEP == 0

    p = jax.nn.softmax(hgrn_lb.astype(F32), axis=0)
    lb_all = jnp.cumsum(p, axis=0) - p[0:1]

    rows = lambda a: a.reshape(depth, 1, -1)
    g1, g2, gf = rows(g_mix), rows(g_mlp), g_final.reshape(1, -1)
    lb, gn = rows(lb_all), rows(hgrn_norm_g)
    ccb, lng, lnb = rows(cconv_b), rows(cconv_ln_g), rows(cconv_ln_b)

    xp = x_prompt.reshape(batch * seq, d)
    xs = x_sample.reshape(nb, d)
    st_s = state_sconv.reshape(depth, nb, (SC_TAPS - 1) * w)
    sc_off = 4 * hw // w
    cc_off = sc_off + 3

    ph, ps, pc, ss = [], [], [], []
    sample_states = None
    for li in range(depth):
        last = li == depth - 1

        if li == 0:
            zs, w_in_b = _norm_matmul_cast(xs, g1, w_in, layer=li, tn=WEIGHT_BLOCK)
        else:
            zs, = _norm_matmul(xs, g1, w_in_b, layer=li, tm=nb)
        mix, new_h, new_s, new_c = _mix_sample(
            zs, state_hgrn, st_s, state_cconv, lb, gn, sconv_w, cconv_w, ccb, lng, lnb, layer=li,
            prev=sample_states)
        sample_states = (new_h, new_c)
        ss.append(new_s.reshape(nb, SC_TAPS - 1, w))

        jobs = [_CastJob(w_out, li, None), _CastJob(w_up, li, WEIGHT_BLOCK),
                _CastJob(w_down, li, None)] if li == 0 else []
        z, *cast = _norm_matmul(xp, g1, w_in_b, layer=li, tm=1024, cast_jobs=jobs)
        if li == 0:
            w_out_b, w_up_b, w_down_b = cast

        xs = _out_proj_sample(xs, mix, w_out_b, tk=WEIGHT_BLOCK)
        xs, = _ffn(xs, g2, w_up_b, w_down_b, gf, layer=li, tm=nb, final_norm=last)

        mh, new_h = _hgrn_prompt(z, lb, gn, layer=li, batch=batch, seq=seq)
        xp, new_s, new_c = _conv_out_proj(xp, mh, z, sconv_w, cconv_w, ccb, lng, lnb, w_out_b,
                                          layer=li, seq=seq, sc_off=sc_off, cc_off=cc_off)
        jobs = [] if last else [_CastJob(w_in, li + 1, WEIGHT_BLOCK), _CastJob(w_out, li + 1, None),
                                _CastJob(w_up, li + 1, WEIGHT_BLOCK), _CastJob(w_down, li + 1, None)]
        xp, *cast = _ffn(xp, g2, w_up_b, w_down_b, gf, layer=li, tm=1024, final_norm=last,
                         cast_jobs=jobs)
        if not last:
            w_in_b, w_out_b, w_up_b, w_down_b = cast
        ph.append(new_h)
        ps.append(new_s)
        pc.append(new_c)

    sh, sc = sample_states
    return (xp.reshape(batch, seq, d), xs.reshape(nb, 1, d), jnp.stack(ph), jnp.stack(ps),
            jnp.stack(pc), sh, jnp.stack(ss), sc)
```

```python
import functools
from typing import NamedTuple, Optional

import numpy as np
import jax
import jax.numpy as jnp
from jax import lax
from jax.experimental import pallas as pl
from jax.experimental.pallas import tpu as pltpu

F32 = jnp.float32
BF16 = jnp.bfloat16

EPS = 1e-6
F_FLOOR = 1e-30

LANES = 128
SUBLANES = 8
BF16_SUBLANES = 16
VMEM_LIMIT_BYTES = 56 * 1024 * 1024

HG_HEADS = 8
HG_DIM = 128
SC_TAPS = 3
CC_TAPS = 31
CHUNK = 128
N_LEVELS = 7
SAMPLE_TOKENS_PER_STEP = 8
WEIGHT_BLOCK = 512
CONV_TIME_BLOCK = 512
CONV_ROWS = 32


def _params(*sem):
    return pltpu.CompilerParams(dimension_semantics=sem, vmem_limit_bytes=VMEM_LIMIT_BYTES)


def _layer_block(a, layer):
    zeros = (0,) * (a.ndim - 1)
    return pl.BlockSpec((None,) + a.shape[1:], lambda *_: (layer,) + zeros)


def _rmsnorm(x, g):
    return x * lax.rsqrt(jnp.mean(x * x, axis=-1, keepdims=True) + EPS) * g


def _sigmoid(x):
    return 0.5 * jnp.tanh(0.5 * x) + 0.5


def _silu(x):
    h = 0.5 * x
    return h + h * jnp.tanh(h)


def _gate_constants(lb):
    f_scale = 0.5 * (1.0 - lb)
    return f_scale, lb + f_scale


def _forget_and_key(z, f_scale, f_shift):
    w = f_scale * jnp.tanh(0.5 * z)
    return f_shift + w, f_scale - w


class _CastJob(NamedTuple):
    w: jax.Array
    layer: int
    col_block: Optional[int]


def _cast_job_plan(job, n_steps, step_of):
    _, rows, cols = job.w.shape
    chunk_rows = BF16_SUBLANES
    while rows % chunk_rows or rows // chunk_rows > n_steps:
        chunk_rows += BF16_SUBLANES
    n_chunks = rows // chunk_rows
    chunk = lambda *idx: jnp.minimum(step_of(*idx), n_chunks - 1)
    in_spec = pl.BlockSpec((None, chunk_rows, cols), lambda *idx: (job.layer, chunk(*idx), 0))
    if job.col_block is None:
        out_spec = pl.BlockSpec((chunk_rows, cols), lambda *idx: (chunk(*idx), 0))
        return in_spec, out_spec, jax.ShapeDtypeStruct((rows, cols), BF16)
    nblk = cols // job.col_block
    out_spec = pl.BlockSpec((nblk, chunk_rows, job.col_block), lambda *idx: (0, chunk(*idx), 0))
    return in_spec, out_spec, jax.ShapeDtypeStruct((nblk, rows, job.col_block), BF16)


def _run_cast_jobs(src_refs, dst_refs):
    for src, dst in zip(src_refs, dst_refs):
        if len(dst.shape) == 2:
            dst[...] = src[...].astype(dst.dtype)
        else:
            width = dst.shape[2]
            for c in range(dst.shape[0]):
                dst[c] = src[:, c * width:(c + 1) * width].astype(dst.dtype)


def _norm_matmul_kernel(x_ref, g_ref, w_ref, *rest, n_jobs):
    cast_src, o_ref, cast_dst, h_ref = rest[:n_jobs], rest[n_jobs], rest[n_jobs + 1:-1], rest[-1]

    @pl.when(pl.program_id(1) == 0)
    def _():
        h_ref[...] = _rmsnorm(x_ref[...], g_ref[...]).astype(BF16)

    o_ref[...] = jnp.dot(h_ref[...], w_ref[...], preferred_element_type=F32)
    _run_cast_jobs(cast_src, cast_dst)


def _norm_matmul(x, g, wb, *, layer, tm, cast_jobs=()):
    m, k = x.shape
    nblk, _, tn = wb.shape
    grid = (m // tm, nblk)
    plans = [_cast_job_plan(job, grid[0] * grid[1], lambda i, j: i * nblk + j)
             for job in cast_jobs]
    return pl.pallas_call(
        functools.partial(_norm_matmul_kernel, n_jobs=len(plans)),
        grid=grid,
        in_specs=[pl.BlockSpec((tm, k), lambda i, j: (i, 0)),
                  _layer_block(g, layer),
                  pl.BlockSpec((None, k, tn), lambda i, j: (j, 0, 0))] + [p[0] for p in plans],
        out_specs=[pl.BlockSpec((None, tm, tn), lambda i, j: (j, i, 0))] + [p[1] for p in plans],
        out_shape=[jax.ShapeDtypeStruct((nblk, m, tn), F32)] + [p[2] for p in plans],
        scratch_shapes=[pltpu.VMEM((tm, k), BF16)],
        compiler_params=_params("arbitrary", "arbitrary"),
        name="in_proj",
    )(x, g, wb, *[job.w for job in cast_jobs])


def _norm_matmul_cast_kernel(x_ref, g_ref, w_ref, o_ref, wb_ref, h_ref):
    @pl.when(pl.program_id(0) == 0)
    def _():
        h_ref[...] = _rmsnorm(x_ref[...], g_ref[...]).astype(BF16)

    wb = w_ref[...].astype(BF16)
    wb_ref[...] = wb
    o_ref[...] = jnp.dot(h_ref[...], wb, preferred_element_type=F32)


def _norm_matmul_cast(x, g, w, *, layer, tn):
    m, k = x.shape
    n = w.shape[2]
    return pl.pallas_call(
        _norm_matmul_cast_kernel,
        grid=(n // tn,),
        in_specs=[pl.BlockSpec((m, k), lambda j: (0, 0)),
                  _layer_block(g, layer),
                  pl.BlockSpec((None, k, tn), lambda j: (layer, 0, j))],
        out_specs=[pl.BlockSpec((None, m, tn), lambda j: (j, 0, 0)),
                   pl.BlockSpec((None, k, tn), lambda j: (j, 0, 0))],
        out_shape=[jax.ShapeDtypeStruct((n // tn, m, tn), F32),
                   jax.ShapeDtypeStruct((n // tn, k, tn), BF16)],
        scratch_shapes=[pltpu.VMEM((m, k), BF16)],
        compiler_params=_params("arbitrary"),
        name="in_proj_cast",
    )(x, g, w)


def _out_proj_sample_kernel(x_ref, m_ref, w_ref, o_ref):
    @pl.when(pl.program_id(0) == 0)
    def _():
        o_ref[...] = x_ref[...]

    o_ref[...] += jnp.dot(m_ref[...], w_ref[...], preferred_element_type=F32)


def _out_proj_sample(x, mix, w_out_b, *, tk):
    m, d = x.shape
    kdim = mix.shape[1]
    return pl.pallas_call(
        _out_proj_sample_kernel,
        grid=(kdim // tk,),
        in_specs=[pl.BlockSpec((m, d), lambda r: (0, 0)),
                  pl.BlockSpec((m, tk), lambda r: (0, r)),
                  pl.BlockSpec((tk, d), lambda r: (r, 0))],
        out_specs=pl.BlockSpec((m, d), lambda r: (0, 0)),
        out_shape=jax.ShapeDtypeStruct((m, d), F32),
        compiler_params=_params("arbitrary"),
        name="out_proj_sample",
    )(x, mix, w_out_b)


def _ffn_kernel(x_ref, g_ref, wu_ref, wd_ref, gf_ref, *rest, final_norm, n_jobs):
    cast_src, o_ref, cast_dst, h_ref = rest[:n_jobs], rest[n_jobs], rest[n_jobs + 1:-1], rest[-1]
    j = pl.program_id(1)

    @pl.when(j == 0)
    def _():
        x = x_ref[...]
        h_ref[...] = _rmsnorm(x, g_ref[...]).astype(BF16)
        o_ref[...] = x

    u = jnp.dot(h_ref[...], wu_ref[...], preferred_element_type=F32)
    r = jnp.maximum(u, 0.0)
    o_ref[...] += jnp.dot((r * r).astype(BF16), wd_ref[...], preferred_element_type=F32)
    _run_cast_jobs(cast_src, cast_dst)

    if final_norm:
        @pl.when(j == pl.num_programs(1) - 1)
        def _():
            o_ref[...] = _rmsnorm(o_ref[...], gf_ref[...])


def _ffn(x, g, wub, wdb, gf, *, layer, tm, final_norm, cast_jobs=()):
    m, d = x.shape
    nblk, _, tf = wub.shape
    grid = (m // tm, nblk)
    plans = [_cast_job_plan(job, grid[0] * grid[1], lambda i, j: i * nblk + j)
             for job in cast_jobs]
    return pl.pallas_call(
        functools.partial(_ffn_kernel, final_norm=final_norm, n_jobs=len(plans)),
        grid=grid,
        in_specs=[pl.BlockSpec((tm, d), lambda i, j: (i, 0)),
                  _layer_block(g, layer),
                  pl.BlockSpec((None, d, tf), lambda i, j: (j, 0, 0)),
                  pl.BlockSpec((tf, d), lambda i, j: (j, 0)),
                  pl.BlockSpec((1, d), lambda i, j: (0, 0))] + [p[0] for p in plans],
        out_specs=[pl.BlockSpec((tm, d), lambda i, j: (i, 0))] + [p[1] for p in plans],
        out_shape=[jax.ShapeDtypeStruct((m, d), F32)] + [p[2] for p in plans],
        scratch_shapes=[pltpu.VMEM((tm, d), BF16)],
        compiler_params=_params("arbitrary", "arbitrary"),
        name="ffn",
    )(x, g, wub, wdb, gf, *[job.w for job in cast_jobs])


LOG2E = 1.4426950408889634
LOW_LEVELS = 3


def _prefix_sum_matrix():
    t = np.arange(CHUNK)[:, None]
    r = np.arange(CHUNK)[None, :]
    return (r <= t).astype(np.float32)


def _hgrn_prompt_kernel(zq_ref, zf_ref, zi_ref, zo_ref, lb_ref, gn_ref, tri_ref, *rest, n_jobs):
    cast_src, (y_ref, s_ref), cast_dst = rest[:n_jobs], rest[n_jobs:n_jobs + 2], rest[n_jobs + 2:]
    _run_cast_jobs(cast_src, cast_dst)
    n_chunks = zq_ref.shape[0] // CHUNK
    n_tiles = CHUNK // SUBLANES
    lb = lb_ref[...]
    gn = gn_ref[...]
    tri = tri_ref[...]
    f_scale, f_shift = _gate_constants(lb)
    sub = lax.broadcasted_iota(jnp.int32, (SUBLANES, HG_DIM), 0)
    lane = lax.broadcasted_iota(jnp.int32, (SUBLANES, HG_DIM), 1)
    col = lane & (SUBLANES - 1)
    below = col < sub
    pat_low = [below & (((sub ^ col) >> l) == 1) for l in range(LOW_LEVELS)]
    pat_diag = col == sub
    upper_low = [((sub >> l) & 1) == 1 for l in range(LOW_LEVELS)]
    nt = (((1,), (1,)), ((), ()))
    tiled = lambda x: x.reshape(n_tiles, SUBLANES, HG_DIM)
    flat = lambda x: x.reshape(CHUNK, HG_DIM)

    def tile_row(x3, s):
        return jnp.broadcast_to(x3[:, s:s + 1, :], x3.shape)

    def chunk(c, st):
        rows = pl.ds(pl.multiple_of(c * CHUNK, CHUNK), CHUNK)
        q = _silu(zq_ref[rows, :])
        f, kk = _forget_and_key(zf_ref[rows, :], f_scale, f_shift)
        g = jnp.log(jnp.maximum(f, F_FLOOR)) * LOG2E
        v = zi_ref[rows, :].astype(BF16)

        g_hi = g.astype(BF16)
        g_lo = (g - g_hi.astype(F32)).astype(BF16)
        b2 = jnp.dot(tri, jnp.concatenate([g_hi, g_lo], axis=1), preferred_element_type=F32)
        b = b2[:, :HG_DIM] + b2[:, HG_DIM:]
        b_last = b[CHUNK - 1:CHUNK, :]
        qd = (q * jnp.exp2(jnp.minimum(b, 0.0))).astype(BF16)
        kd = (kk * jnp.exp2(jnp.minimum(b_last - b, 0.0))).astype(BF16)

        half = SUBLANES // 2
        b3, g3, q3, k3 = tiled(b), tiled(g), tiled(q), tiled(kk)
        ref_low = [None,
                   jnp.where(sub < half, tile_row(b3, 1), tile_row(b3, half + 1)),
                   tile_row(b3, half - 1)]
        a_low = []
        for l in range(LOW_LEVELS):
            upper = upper_low[l]
            nd = jnp.where(upper, g3, 0.0) if l == 0 else -jnp.abs(b3 - ref_low[l])
            x = flat(jnp.exp2(nd) * jnp.where(upper, q3, k3)).astype(BF16)
            a_low.append(lax.dot_general(x, x, nt, preferred_element_type=F32))

        a_up = {}
        for l in range(LOW_LEVELS, N_LEVELS):
            h = 1 << l
            pairs = CHUNK // (2 * h)
            bg = b.reshape(pairs, 2, h, HG_DIM)
            ref = bg[:, 0, h - 1:h, :]
            x_lo = kk.reshape(pairs, 2, h, HG_DIM)[:, 0] * jnp.exp2(jnp.minimum(ref - bg[:, 0], 0.0))
            x_up = q.reshape(pairs, 2, h, HG_DIM)[:, 1] * jnp.exp2(jnp.minimum(bg[:, 1] - ref, 0.0))
            x_all = jnp.stack([x_lo, x_up], axis=1).reshape(CHUNK, HG_DIM).astype(BF16)
            a_up[l] = lax.dot_general(x_up.reshape(CHUNK // 2, HG_DIM).astype(BF16), x_all, nt,
                                      preferred_element_type=F32)

        dqk = jnp.sum(q * kk, axis=1, keepdims=True)
        tiles = []
        for j in range(n_tiles):
            r = slice(j * SUBLANES, (j + 1) * SUBLANES)
            blk = jnp.where(pat_diag, dqk[r], 0.0)
            for l in range(LOW_LEVELS):
                blk = jnp.where(pat_low[l], a_low[l][r], blk)
            blk = jnp.where((lane >> 3) == j, blk, 0.0)
            for l in range(LOW_LEVELS, N_LEVELS):
                h = 1 << l
                t0 = j * SUBLANES
                if (t0 // h) % 2 == 1:
                    pair = t0 // (2 * h)
                    r_up = pair * h + (t0 - pair * 2 * h - h)
                    blk = jnp.where(lane < pair * 2 * h + h, a_up[l][r_up:r_up + SUBLANES], blk)
            tiles.append(blk)
        att = jnp.concatenate(tiles, axis=0)

        o = jnp.dot(att.astype(BF16), v, preferred_element_type=F32)
        o += lax.dot_general(qd, st.astype(BF16), nt, preferred_element_type=F32)
        st = st * jnp.exp2(jnp.minimum(b_last, 0.0)) + lax.dot_general(
            v, kd, (((0,), (0,)), ((), ())), preferred_element_type=F32)

        o = o * lax.rsqrt(jnp.mean(o * o, axis=-1, keepdims=True) + EPS) * gn
        y_ref[rows, :] = (o * _silu(zo_ref[rows, :])).astype(y_ref.dtype)
        return st

    st = lax.fori_loop(0, n_chunks, chunk, jnp.zeros((HG_DIM, HG_DIM), F32), unroll=16)
    s_ref[0, 0] = st.T


def _hgrn_prompt(z, lb, gn, *, layer, batch, seq, cast_jobs=()):
    hw = HG_HEADS * HG_DIM
    plans = [_cast_job_plan(job, batch * HG_HEADS, lambda n, h: n * HG_HEADS + h)
             for job in cast_jobs]
    per_blk = z.shape[2] // HG_DIM
    col = lambda off: pl.BlockSpec(
        (None, seq, HG_DIM), lambda n, h, off=off: ((off + h) // per_blk, n, (off + h) % per_blk))
    per_head = pl.BlockSpec((None, 1, HG_DIM), lambda n, h: (layer, 0, h))
    tri = jnp.asarray(_prefix_sum_matrix(), BF16)
    return pl.pallas_call(
        functools.partial(_hgrn_prompt_kernel, n_jobs=len(plans)),
        grid=(batch, HG_HEADS),
        in_specs=[col(0), col(HG_HEADS), col(2 * HG_HEADS), col(3 * HG_HEADS), per_head, per_head,
                  pl.BlockSpec(tri.shape, lambda n, h: (0, 0))] + [p[0] for p in plans],
        out_specs=[pl.BlockSpec((seq, HG_DIM), lambda n, h: (n, h)),
                   pl.BlockSpec((1, 1, HG_DIM, HG_DIM), lambda n, h: (n, h, 0, 0))]
        + [p[1] for p in plans],
        out_shape=[jax.ShapeDtypeStruct((batch * seq, hw), BF16),
                   jax.ShapeDtypeStruct((batch, HG_HEADS, HG_DIM, HG_DIM), F32)]
        + [p[2] for p in plans],
        compiler_params=_params("arbitrary", "arbitrary"),
        name="hgrn_prompt",
    )(z, z, z, z, lb, gn, tri, *[job.w for job in cast_jobs])


def _conv_out_proj_kernel(x_ref, mh_ref, zb_ref, zc_ref, zh_ref, zv_ref, zg_ref, scw_ref, ccw_ref,
                          ccb_ref, lng_ref, lnb_ref, wh_ref, ws_ref, wc_ref,
                          o_ref, ns_ref, nc_ref,
                          u_ref, a_ref, ms_next, mc_next, ms_cur, mc_cur, *, tiles_per_seq):
    s = pl.program_id(0)
    t_blk = zb_ref.shape[0]
    u_halo = SUBLANES
    a_halo = CONV_ROWS

    @pl.when(s == 0)
    def _():
        ms_next[...] = jnp.zeros(ms_next.shape, ms_next.dtype)
        mc_next[...] = jnp.zeros(mc_next.shape, mc_next.dtype)
        u_ref[...] = jnp.zeros(u_ref.shape, u_ref.dtype)
        a_ref[...] = jnp.zeros(a_ref.shape, a_ref.dtype)

    ms_cur[...] = ms_next[...]
    mc_cur[...] = mc_next[...]
    acc = x_ref[...]
    acc += jnp.dot(mh_ref[...], wh_ref[...], preferred_element_type=F32)
    acc += jnp.dot(ms_cur[...], ws_ref[...], preferred_element_type=F32)
    acc += jnp.dot(mc_cur[...], wc_ref[...], preferred_element_type=F32)
    o_ref[...] = acc

    tile = jnp.minimum(s, pl.num_programs(0) - 2)
    seq_start = lax.rem(tile, tiles_per_seq) == 0
    u_ref[0:u_halo, :] = jnp.where(seq_start, 0.0, u_ref[t_blk:t_blk + u_halo, :])
    a_ref[0:a_halo, :] = jnp.where(seq_start, 0.0, a_ref[t_blk:t_blk + a_halo, :])
    u_ref[u_halo:, :] = zc_ref[...] * zh_ref[...]
    a_ref[a_halo:, :] = zv_ref[...] * _sigmoid(zg_ref[...])
    ccb = ccb_ref[...]
    lng = lng_ref[...]
    lnb = lnb_ref[...]
    for i in range(t_blk // CONV_ROWS):
        base = i * CONV_ROWS
        rows = slice(base, base + CONV_ROWS)
        uw = u_ref[base:base + u_halo + CONV_ROWS, :]
        acc = scw_ref[SC_TAPS - 1:SC_TAPS, :] * uw[u_halo:, :]
        for j in range(SC_TAPS - 1):
            off = u_halo - (SC_TAPS - 1) + j
            acc += scw_ref[j:j + 1, :] * uw[off:off + CONV_ROWS, :]
        ms_next[rows, :] = (zb_ref[rows, :] * acc).astype(ms_next.dtype)

        strips = []
        for st in range(a_ref.shape[1] // LANES):
            lanes = slice(st * LANES, (st + 1) * LANES)
            aw = a_ref[base:base + a_halo + CONV_ROWS, lanes]
            acc = ccb[:, lanes]
            for res in range(SUBLANES):
                shifted = aw if res == 0 else pltpu.roll(aw, a_halo + CONV_ROWS - res, axis=0)
                for j in range(CC_TAPS):
                    off = a_halo - (CC_TAPS - 1) + j
                    if off % SUBLANES == res:
                        acc += ccw_ref[j:j + 1, lanes] * shifted[off - res:off - res + CONV_ROWS, :]
            strips.append(acc)
        acc = jnp.concatenate(strips, axis=1)
        mu = jnp.mean(acc, axis=-1, keepdims=True)
        d = acc - mu
        var = jnp.mean(d * d, axis=-1, keepdims=True)
        mc_next[rows, :] = _silu(d * lax.rsqrt(var + EPS) * lng + lnb).astype(mc_next.dtype)

    ns_ref[0] = u_ref[u_halo + t_blk - (SC_TAPS - 1):u_halo + t_blk, :]
    nc_ref[0] = a_ref[a_halo + t_blk - (CC_TAPS - 1):a_halo + t_blk, :]


def _conv_out_proj(x, mh, z, scw, ccw, ccb, lng, lnb, w_out, *, layer, seq, sc_off, cc_off):
    m, d = x.shape
    hw = mh.shape[1]
    w = scw.shape[2]
    t_blk = CONV_TIME_BLOCK
    n_tiles = m // t_blk
    tiles_per_seq = seq // t_blk
    conv_tile = lambda s: jnp.minimum(s, n_tiles - 1)
    proj_tile = lambda s: jnp.maximum(s - 1, 0)
    assert z.shape[2] == w
    zcol = lambda off: pl.BlockSpec((None, t_blk, w), lambda s, off=off: (off, conv_tile(s), 0))
    prow = lambda width: pl.BlockSpec((t_blk, width), lambda s: (proj_tile(s), 0))
    whole = lambda a: _layer_block(a, layer)
    w_rows = lambda rows, blk: pl.BlockSpec((rows, d), lambda s: (blk, 0))
    state = lambda taps: pl.BlockSpec((1, taps - 1, w),
                                      lambda s: (conv_tile(s) // tiles_per_seq, 0, 0))
    return pl.pallas_call(
        functools.partial(_conv_out_proj_kernel, tiles_per_seq=tiles_per_seq),
        grid=(n_tiles + 1,),
        in_specs=[prow(d), prow(hw),
                  zcol(sc_off), zcol(sc_off + 1), zcol(sc_off + 2), zcol(cc_off), zcol(cc_off + 1),
                  whole(scw), whole(ccw), whole(ccb), whole(lng), whole(lnb),
                  w_rows(hw, 0), w_rows(w, hw // w), w_rows(w, hw // w + 1)],
        out_specs=[prow(d), state(SC_TAPS), state(CC_TAPS)],
        out_shape=[jax.ShapeDtypeStruct((m, d), F32),
                   jax.ShapeDtypeStruct((m // seq, SC_TAPS - 1, w), F32),
                   jax.ShapeDtypeStruct((m // seq, CC_TAPS - 1, w), F32)],
        scratch_shapes=[pltpu.VMEM((SUBLANES + t_blk, w), F32),
                        pltpu.VMEM((CONV_ROWS + t_blk, w), F32),
                        pltpu.VMEM((t_blk, w), BF16), pltpu.VMEM((t_blk, w), BF16),
                        pltpu.VMEM((t_blk, w), BF16), pltpu.VMEM((t_blk, w), BF16)],
        compiler_params=_params("arbitrary"),
        name="conv_out_proj",
    )(x, mh, z, z, z, z, z, scw, ccw, ccb, lng, lnb, w_out, w_out, w_out)


def _mix_sample_kernel(*refs):
    so_ref, nc_ref = refs[-5], refs[-3]

    @pl.when(pl.program_id(0) == 0)
    def _():
        _mix_sample_step(*refs[:11], *refs[-6:])

    @pl.when(pl.program_id(0) > 0)
    def _():
        so_ref[...] = jnp.zeros(so_ref.shape, so_ref.dtype)
        nc_ref[...] = jnp.zeros(nc_ref.shape, nc_ref.dtype)


def _mix_sample_step(z_ref, st_ref, ss_ref, sc_ref, lb_ref, gn_ref, scw_ref, ccw_ref, ccb_ref,
                     lng_ref, lnb_ref, mix_ref, so_ref, ns_ref, nc_ref, oh_ref, cv_ref):
    hw = HG_HEADS * HG_DIM
    w = scw_ref.shape[1]
    eye = (lax.broadcasted_iota(jnp.int32, (HG_DIM, HG_DIM), 0)
           == lax.broadcasted_iota(jnp.int32, (HG_DIM, HG_DIM), 1))

    def column(x_row):
        return jnp.sum(jnp.where(eye, x_row, 0.0), axis=1, keepdims=True)

    n_tok = z_ref.shape[1]
    zw = z_ref.shape[2]

    def zcols(lo, hi):
        parts = [z_ref[c, :, max(lo - c * zw, 0):min(hi - c * zw, zw)]
                 for c in range(lo // zw, (hi - 1) // zw + 1)]
        return parts[0] if len(parts) == 1 else jnp.concatenate(parts, axis=1)

    q_all = _silu(zcols(0, hw))
    f_all, k_all = _forget_and_key(zcols(hw, 2 * hw), *_gate_constants(lb_ref[...]))
    decay_all = jnp.exp(jnp.log(jnp.maximum(f_all, F_FLOOR)))
    v_all = zcols(2 * hw, 3 * hw)
    for t in range(n_tok):
        tr = slice(t, t + 1)
        for h in range(HG_HEADS):
            cols = slice(h * HG_DIM, (h + 1) * HG_DIM)
            s_new = (column(decay_all[tr, cols]) * st_ref[t, h]
                     + column(k_all[tr, cols]) * v_all[tr, cols])
            so_ref[t, h] = s_new
            oh_ref[tr, cols] = jnp.dot(q_all[tr, cols].astype(BF16), s_new.astype(BF16),
                                       preferred_element_type=F32)
    gate_all = _silu(zcols(3 * hw, 4 * hw))
    for h in range(HG_HEADS):
        cols = slice(h * HG_DIM, (h + 1) * HG_DIM)
        o = oh_ref[:, cols]
        o = o * lax.rsqrt(jnp.mean(o * o, axis=-1, keepdims=True) + EPS) * gn_ref[:, cols]
        oh_ref[:, cols] = o * gate_all[:, cols]
    mix_ref[:, 0:hw] = oh_ref[...].astype(mix_ref.dtype)

    off = 4 * hw
    zb = zcols(off, off + w)
    u = zcols(off + w, off + 2 * w) * zcols(off + 2 * w, off + 3 * w)
    conv = scw_ref[SC_TAPS - 1:SC_TAPS, :] * u
    for j in range(SC_TAPS - 1):
        conv += scw_ref[j:j + 1, :] * ss_ref[:, j * w:(j + 1) * w]
    mix_ref[:, hw:hw + w] = (zb * conv).astype(mix_ref.dtype)
    for j in range(SC_TAPS - 2):
        ns_ref[:, j * w:(j + 1) * w] = ss_ref[:, (j + 1) * w:(j + 2) * w]
    ns_ref[:, (SC_TAPS - 2) * w:] = u

    off = 4 * hw + 3 * w
    a = zcols(off, off + w) * _sigmoid(zcols(off + w, off + 2 * w))
    past_taps = ccw_ref[0:CC_TAPS - 1, :]
    for t in range(n_tok):
        cv_ref[t:t + 1, :] = jnp.sum(past_taps * sc_ref[t], axis=0, keepdims=True)
        nc_ref[t, 0:CC_TAPS - 2, :] = sc_ref[t, 1:CC_TAPS - 1, :]
        nc_ref[t, CC_TAPS - 2:CC_TAPS - 1, :] = a[t:t + 1, :]
    conv = ccb_ref[...] + ccw_ref[CC_TAPS - 1:CC_TAPS, :] * a + cv_ref[...]
    mu = jnp.mean(conv, axis=-1, keepdims=True)
    d = conv - mu
    var = jnp.mean(d * d, axis=-1, keepdims=True)
    y = d * lax.rsqrt(var + EPS) * lng_ref[...] + lnb_ref[...]
    mix_ref[:, hw + w:] = _silu(y).astype(mix_ref.dtype)


def _mix_sample(z, st_h, st_s, st_c, lb, gn, scw, ccw, ccb, lng, lnb, *, layer, prev):
    nb = z.shape[1]
    tb = SAMPLE_TOKENS_PER_STEP
    hw = HG_HEADS * HG_DIM
    w = scw.shape[2]
    nblk = nb // tb
    n_pass = st_h.shape[0] - layer if prev is None else 1
    tok = lambda p, i: jnp.where(p == 0, i, nblk - 1)
    rows = lambda width: pl.BlockSpec((tb, width), lambda p, i: (tok(p, i), 0))
    layer_rows = lambda width: pl.BlockSpec((None, tb, width), lambda p, i: (layer, tok(p, i), 0))
    whole = lambda a: _layer_block(a, layer)
    state_tile = (None, tb, HG_HEADS, HG_DIM, HG_DIM)
    conv_tile = (None, tb) + st_c.shape[2:]
    args = [z, st_h, st_s, st_c, lb, gn, scw, ccw, ccb, lng, lnb]
    in_specs = [pl.BlockSpec((z.shape[0], tb, z.shape[2]), lambda p, i: (0, tok(p, i), 0)),
                pl.BlockSpec(state_tile, lambda p, i: (layer, tok(p, i), 0, 0, 0)),
                layer_rows(st_s.shape[2]),
                pl.BlockSpec(conv_tile, lambda p, i: (layer, tok(p, i), 0, 0)),
                whole(lb), whole(gn), whole(scw), whole(ccw), whole(ccb), whole(lng), whole(lnb)]
    aliases = {}
    if prev is not None:
        aliases = {len(args): 1, len(args) + 1: 3}
        args += list(prev)
        in_specs += [pl.BlockSpec(memory_space=pl.ANY)] * 2
    return pl.pallas_call(
        _mix_sample_kernel,
        grid=(n_pass, nblk),
        in_specs=in_specs,
        out_specs=[rows(hw + 2 * w),
                   pl.BlockSpec(state_tile, lambda p, i: (layer + p, i, 0, 0, 0)),
                   rows(st_s.shape[2]),
                   pl.BlockSpec(conv_tile, lambda p, i: (layer + p, i, 0, 0))],
        out_shape=[jax.ShapeDtypeStruct((nb, hw + 2 * w), BF16),
                   jax.ShapeDtypeStruct(st_h.shape, F32),
                   jax.ShapeDtypeStruct((nb, st_s.shape[2]), F32),
                   jax.ShapeDtypeStruct(st_c.shape, F32)],
        scratch_shapes=[pltpu.VMEM((tb, hw), F32), pltpu.VMEM((tb, w), F32)],
        input_output_aliases=aliases,
        compiler_params=_params("arbitrary", "arbitrary"),
        name="mix_sample",
    )(*args)


def kernel(x_prompt, x_sample, state_hgrn, state_sconv, state_cconv, g_mix, w_in, hgrn_lb,
           hgrn_norm_g, sconv_w, cconv_w, cconv_b, cconv_ln_g, cconv_ln_b, w_out, g_mlp,
           w_up, w_down, g_final):
    batch, seq, d = x_prompt.shape
    nb = x_sample.shape[0]
    depth = w_in.shape[0]
    hw = HG_HEADS * HG_DIM
    w = sconv_w.shape[2]
    assert state_hgrn.shape[2:] == (HG_HEADS, HG_DIM, HG_DIM)
    assert w_in.shape[2] == 4 * hw + 5 * w and w_out.shape[1] == hw + 2 * w
    assert sconv_w.shape[1] == SC_TAPS and cconv_w.shape[1] == CC_TAPS
    assert seq % CONV_TIME_BLOCK == 0 and seq % CHUNK == 0 and nb % SAMPLE_TOKENS_PER_STEP == 0

    p = jax.nn.softmax(hgrn_lb.astype(F32), axis=0)
    lb_all = jnp.cumsum(p, axis=0) - p[0:1]

    rows = lambda a: a.reshape(depth, 1, -1)
    g1, g2, gf = rows(g_mix), rows(g_mlp), g_final.reshape(1, -1)
    lb, gn = rows(lb_all), rows(hgrn_norm_g)
    ccb, lng, lnb = rows(cconv_b), rows(cconv_ln_g), rows(cconv_ln_b)

    xp = x_prompt.reshape(batch * seq, d)
    xs = x_sample.reshape(nb, d)
    st_s = state_sconv.reshape(depth, nb, (SC_TAPS - 1) * w)
    sc_off = 4 * hw // w
    cc_off = sc_off + 3

    ph, ps, pc, ss = [], [], [], []
    sample_states = None
    for li in range(depth):
        last = li == depth - 1

        if li == 0:
            zs, w_in_b = _norm_matmul_cast(xs, g1, w_in, layer=li, tn=WEIGHT_BLOCK)
        else:
            zs, = _norm_matmul(xs, g1, w_in_b, layer=li, tm=nb)
        mix, new_h, new_s, new_c = _mix_sample(
            zs, state_hgrn, st_s, state_cconv, lb, gn, sconv_w, cconv_w, ccb, lng, lnb, layer=li,
            prev=sample_states)
        sample_states = (new_h, new_c)
        ss.append(new_s.reshape(nb, SC_TAPS - 1, w))

        z, = _norm_matmul(xp, g1, w_in_b, layer=li, tm=1024)
        jobs = [_CastJob(w_out, li, None), _CastJob(w_up, li, WEIGHT_BLOCK),
                _CastJob(w_down, li, None)] if li == 0 else []
        mh, new_h, *cast = _hgrn_prompt(z, lb, gn, layer=li, batch=batch, seq=seq, cast_jobs=jobs)
        if li == 0:
            w_out_b, w_up_b, w_down_b = cast

        xs = _out_proj_sample(xs, mix, w_out_b, tk=WEIGHT_BLOCK)
        xs, = _ffn(xs, g2, w_up_b, w_down_b, gf, layer=li, tm=nb, final_norm=last)

        xp, new_s, new_c = _conv_out_proj(xp, mh, z, sconv_w, cconv_w, ccb, lng, lnb, w_out_b,
                                          layer=li, seq=seq, sc_off=sc_off, cc_off=cc_off)
        jobs = [] if last else [_CastJob(w_in, li + 1, WEIGHT_BLOCK), _CastJob(w_out, li + 1, None),
                                _CastJob(w_up, li + 1, WEIGHT_BLOCK), _CastJob(w_down, li + 1, None)]
        xp, *cast = _ffn(xp, g2, w_up_b, w_down_b, gf, layer=li, tm=1024, final_norm=last,
                         cast_jobs=jobs)
        if not last:
            w_in_b, w_out_b, w_up_b, w_down_b = cast
        ph.append(new_h)
        ps.append(new_s)
        pc.append(new_c)

    sh, sc = sample_states
    return (xp.reshape(batch, seq, d), xs.reshape(nb, 1, d), jnp.stack(ph), jnp.stack(ps),
            jnp.stack(pc), sh, jnp.stack(ss), sc)
```

```python
import functools
from typing import NamedTuple, Optional

import numpy as np
import jax
import jax.numpy as jnp
from jax import lax
from jax.experimental import pallas as pl
from jax.experimental.pallas import tpu as pltpu

F32 = jnp.float32
BF16 = jnp.bfloat16

EPS = 1e-6
F_FLOOR = 1e-30

LANES = 128
SUBLANES = 8
BF16_SUBLANES = 16
VMEM_LIMIT_BYTES = 56 * 1024 * 1024
IN_PROJ_VMEM_LIMIT_BYTES = 58 * 1024 * 1024

HG_HEADS = 8
HG_DIM = 128
SC_TAPS = 3
CC_TAPS = 31
CHUNK = 128
N_LEVELS = 7
SAMPLE_TOKENS_PER_STEP = 8
WEIGHT_BLOCK = 512
CONV_TIME_BLOCK = 512
CONV_ROWS = 32


def _params(*sem, vmem_limit_bytes=VMEM_LIMIT_BYTES):
    return pltpu.CompilerParams(dimension_semantics=sem, vmem_limit_bytes=vmem_limit_bytes)


def _layer_block(a, layer):
    zeros = (0,) * (a.ndim - 1)
    return pl.BlockSpec((None,) + a.shape[1:], lambda *_: (layer,) + zeros)


def _rmsnorm(x, g):
    return x * lax.rsqrt(jnp.mean(x * x, axis=-1, keepdims=True) + EPS) * g


def _sigmoid(x):
    return 0.5 * jnp.tanh(0.5 * x) + 0.5


def _silu(x):
    h = 0.5 * x
    return h + h * jnp.tanh(h)


def _gate_constants(lb):
    f_scale = 0.5 * (1.0 - lb)
    return f_scale, lb + f_scale


def _forget_and_key(z, f_scale, f_shift):
    w = f_scale * jnp.tanh(0.5 * z)
    return f_shift + w, f_scale - w


class _CastJob(NamedTuple):
    w: jax.Array
    layer: int
    col_block: Optional[int]


def _cast_job_plan(job, n_steps, step_of):
    _, rows, cols = job.w.shape
    chunk_rows = BF16_SUBLANES
    while rows % chunk_rows or rows // chunk_rows > n_steps:
        chunk_rows += BF16_SUBLANES
    n_chunks = rows // chunk_rows
    chunk = lambda *idx: jnp.minimum(step_of(*idx), n_chunks - 1)
    in_spec = pl.BlockSpec((None, chunk_rows, cols), lambda *idx: (job.layer, chunk(*idx), 0))
    if job.col_block is None:
        out_spec = pl.BlockSpec((chunk_rows, cols), lambda *idx: (chunk(*idx), 0))
        return in_spec, out_spec, jax.ShapeDtypeStruct((rows, cols), BF16)
    nblk = cols // job.col_block
    out_spec = pl.BlockSpec((nblk, chunk_rows, job.col_block), lambda *idx: (0, chunk(*idx), 0))
    return in_spec, out_spec, jax.ShapeDtypeStruct((nblk, rows, job.col_block), BF16)


def _run_cast_jobs(src_refs, dst_refs):
    for src, dst in zip(src_refs, dst_refs):
        if len(dst.shape) == 2:
            dst[...] = src[...].astype(dst.dtype)
        else:
            width = dst.shape[2]
            for c in range(dst.shape[0]):
                dst[c] = src[:, c * width:(c + 1) * width].astype(dst.dtype)


def _norm_matmul_kernel(x_ref, g_ref, w_ref, o_ref, h_ref):
    @pl.when(pl.program_id(1) == 0)
    def _():
        h_ref[...] = _rmsnorm(x_ref[...], g_ref[...]).astype(BF16)

    o_ref[...] = jnp.dot(h_ref[...], w_ref[...], preferred_element_type=F32)


def _norm_matmul(x, g, wb, *, layer, tm):
    m, k = x.shape
    nblk, _, tn = wb.shape
    return pl.pallas_call(
        _norm_matmul_kernel,
        grid=(m // tm, nblk),
        in_specs=[pl.BlockSpec((tm, k), lambda i, j: (i, 0)),
                  _layer_block(g, layer),
                  pl.BlockSpec((None, k, tn), lambda i, j: (j, 0, 0))],
        out_specs=pl.BlockSpec((None, tm, tn), lambda i, j: (j, i, 0)),
        out_shape=jax.ShapeDtypeStruct((nblk, m, tn), F32),
        scratch_shapes=[pltpu.VMEM((tm, k), BF16)],
        compiler_params=_params("arbitrary", "arbitrary",
                                vmem_limit_bytes=IN_PROJ_VMEM_LIMIT_BYTES),
        name="in_proj",
    )(x, g, wb)


def _norm_matmul_cast_kernel(x_ref, g_ref, w_ref, o_ref, wb_ref, h_ref):
    @pl.when(pl.program_id(0) == 0)
    def _():
        h_ref[...] = _rmsnorm(x_ref[...], g_ref[...]).astype(BF16)

    wb = w_ref[...].astype(BF16)
    wb_ref[...] = wb
    o_ref[...] = jnp.dot(h_ref[...], wb, preferred_element_type=F32)


def _norm_matmul_cast(x, g, w, *, layer, tn):
    m, k = x.shape
    n = w.shape[2]
    return pl.pallas_call(
        _norm_matmul_cast_kernel,
        grid=(n // tn,),
        in_specs=[pl.BlockSpec((m, k), lambda j: (0, 0)),
                  _layer_block(g, layer),
                  pl.BlockSpec((None, k, tn), lambda j: (layer, 0, j))],
        out_specs=[pl.BlockSpec((None, m, tn), lambda j: (j, 0, 0)),
                   pl.BlockSpec((None, k, tn), lambda j: (j, 0, 0))],
        out_shape=[jax.ShapeDtypeStruct((n // tn, m, tn), F32),
                   jax.ShapeDtypeStruct((n // tn, k, tn), BF16)],
        scratch_shapes=[pltpu.VMEM((m, k), BF16)],
        compiler_params=_params("arbitrary"),
        name="in_proj_cast",
    )(x, g, w)


def _out_proj_sample_kernel(x_ref, m_ref, w_ref, o_ref):
    @pl.when(pl.program_id(0) == 0)
    def _():
        o_ref[...] = x_ref[...]

    o_ref[...] += jnp.dot(m_ref[...], w_ref[...], preferred_element_type=F32)


def _out_proj_sample(x, mix, w_out_b, *, tk):
    m, d = x.shape
    kdim = mix.shape[1]
    return pl.pallas_call(
        _out_proj_sample_kernel,
        grid=(kdim // tk,),
        in_specs=[pl.BlockSpec((m, d), lambda r: (0, 0)),
                  pl.BlockSpec((m, tk), lambda r: (0, r)),
                  pl.BlockSpec((tk, d), lambda r: (r, 0))],
        out_specs=pl.BlockSpec((m, d), lambda r: (0, 0)),
        out_shape=jax.ShapeDtypeStruct((m, d), F32),
        compiler_params=_params("arbitrary"),
        name="out_proj_sample",
    )(x, mix, w_out_b)


def _ffn_kernel(x_ref, g_ref, wu_ref, wd_ref, gf_ref, *rest, final_norm, n_jobs):
    cast_src, o_ref, cast_dst, h_ref = rest[:n_jobs], rest[n_jobs], rest[n_jobs + 1:-1], rest[-1]
    j = pl.program_id(1)

    @pl.when(j == 0)
    def _():
        x = x_ref[...]
        h_ref[...] = _rmsnorm(x, g_ref[...]).astype(BF16)
        o_ref[...] = x

    u = jnp.dot(h_ref[...], wu_ref[...], preferred_element_type=F32)
    r = jnp.maximum(u, 0.0)
    o_ref[...] += jnp.dot((r * r).astype(BF16), wd_ref[...], preferred_element_type=F32)
    _run_cast_jobs(cast_src, cast_dst)

    if final_norm:
        @pl.when(j == pl.num_programs(1) - 1)
        def _():
            o_ref[...] = _rmsnorm(o_ref[...], gf_ref[...])


def _ffn(x, g, wub, wdb, gf, *, layer, tm, final_norm, cast_jobs=()):
    m, d = x.shape
    nblk, _, tf = wub.shape
    grid = (m // tm, nblk)
    plans = [_cast_job_plan(job, grid[0] * grid[1], lambda i, j: i * nblk + j)
             for job in cast_jobs]
    return pl.pallas_call(
        functools.partial(_ffn_kernel, final_norm=final_norm, n_jobs=len(plans)),
        grid=grid,
        in_specs=[pl.BlockSpec((tm, d), lambda i, j: (i, 0)),
                  _layer_block(g, layer),
                  pl.BlockSpec((None, d, tf), lambda i, j: (j, 0, 0)),
                  pl.BlockSpec((tf, d), lambda i, j: (j, 0)),
                  pl.BlockSpec((1, d), lambda i, j: (0, 0))] + [p[0] for p in plans],
        out_specs=[pl.BlockSpec((tm, d), lambda i, j: (i, 0))] + [p[1] for p in plans],
        out_shape=[jax.ShapeDtypeStruct((m, d), F32)] + [p[2] for p in plans],
        scratch_shapes=[pltpu.VMEM((tm, d), BF16)],
        compiler_params=_params("arbitrary", "arbitrary"),
        name="ffn",
    )(x, g, wub, wdb, gf, *[job.w for job in cast_jobs])


LOG2E = 1.4426950408889634
LOW_LEVELS = 3


def _prefix_sum_matrix():
    t = np.arange(CHUNK)[:, None]
    r = np.arange(CHUNK)[None, :]
    return (r <= t).astype(np.float32)


def _hgrn_prompt_kernel(zq_ref, zf_ref, zi_ref, zo_ref, lb_ref, gn_ref, tri_ref, *rest, n_jobs):
    cast_src, (y_ref, s_ref), cast_dst = rest[:n_jobs], rest[n_jobs:n_jobs + 2], rest[n_jobs + 2:]
    _run_cast_jobs(cast_src, cast_dst)
    n_chunks = zq_ref.shape[0] // CHUNK
    n_tiles = CHUNK // SUBLANES
    lb = lb_ref[...]
    gn = gn_ref[...]
    tri = tri_ref[...]
    f_scale, f_shift = _gate_constants(lb)
    sub = lax.broadcasted_iota(jnp.int32, (SUBLANES, HG_DIM), 0)
    lane = lax.broadcasted_iota(jnp.int32, (SUBLANES, HG_DIM), 1)
    col = lane & (SUBLANES - 1)
    below = col < sub
    pat_low = [below & (((sub ^ col) >> l) == 1) for l in range(LOW_LEVELS)]
    pat_diag = col == sub
    upper_low = [((sub >> l) & 1) == 1 for l in range(LOW_LEVELS)]
    nt = (((1,), (1,)), ((), ()))
    tiled = lambda x: x.reshape(n_tiles, SUBLANES, HG_DIM)
    flat = lambda x: x.reshape(CHUNK, HG_DIM)

    def tile_row(x3, s):
        return jnp.broadcast_to(x3[:, s:s + 1, :], x3.shape)

    def chunk(c, st):
        rows = pl.ds(pl.multiple_of(c * CHUNK, CHUNK), CHUNK)
        q = _silu(zq_ref[rows, :])
        f, kk = _forget_and_key(zf_ref[rows, :], f_scale, f_shift)
        fc = jnp.maximum(f, F_FLOOR)
        g = jnp.log(fc) * LOG2E
        v = zi_ref[rows, :].astype(BF16)

        g_hi = g.astype(BF16)
        g_lo = (g - g_hi.astype(F32)).astype(BF16)
        b2 = jnp.dot(tri, jnp.concatenate([g_hi, g_lo], axis=1), preferred_element_type=F32)
        b = b2[:, :HG_DIM] + b2[:, HG_DIM:]
        b_last = b[CHUNK - 1:CHUNK, :]
        qd = (q * jnp.exp2(jnp.minimum(b, 0.0))).astype(BF16)
        kd = (kk * jnp.exp2(jnp.minimum(b_last - b, 0.0))).astype(BF16)

        half = SUBLANES // 2
        b3, q3, k3 = tiled(b), tiled(q), tiled(kk)
        ref_low = [None,
                   jnp.where(sub < half, tile_row(b3, 1), tile_row(b3, half + 1)),
                   tile_row(b3, half - 1)]
        a_low = []
        for l in range(LOW_LEVELS):
            upper = upper_low[l]
            if l == 0:
                x3 = jnp.where(upper, q3 * tiled(fc), k3)
            else:
                x3 = jnp.exp2(-jnp.abs(b3 - ref_low[l])) * jnp.where(upper, q3, k3)
            x = flat(x3).astype(BF16)
            a_low.append(lax.dot_general(x, x, nt, preferred_element_type=F32))

        a_up = {}
        for l in range(LOW_LEVELS, N_LEVELS):
            h = 1 << l
            pairs = CHUNK // (2 * h)
            bg = b.reshape(pairs, 2, h, HG_DIM)
            ref = bg[:, 0, h - 1:h, :]
            x_lo = kk.reshape(pairs, 2, h, HG_DIM)[:, 0] * jnp.exp2(jnp.minimum(ref - bg[:, 0], 0.0))
            x_up = q.reshape(pairs, 2, h, HG_DIM)[:, 1] * jnp.exp2(jnp.minimum(bg[:, 1] - ref, 0.0))
            x_all = jnp.stack([x_lo, x_up], axis=1).reshape(CHUNK, HG_DIM).astype(BF16)
            a_up[l] = lax.dot_general(x_up.reshape(CHUNK // 2, HG_DIM).astype(BF16), x_all, nt,
                                      preferred_element_type=F32)

        dqk = jnp.sum(q * kk, axis=1, keepdims=True)
        tiles = []
        for j in range(n_tiles):
            r = slice(j * SUBLANES, (j + 1) * SUBLANES)
            blk = jnp.where(pat_diag, dqk[r], 0.0)
            for l in range(LOW_LEVELS):
                blk = jnp.where(pat_low[l], a_low[l][r], blk)
            blk = jnp.where((lane >> 3) == j, blk, 0.0)
            for l in range(LOW_LEVELS, N_LEVELS):
                h = 1 << l
                t0 = j * SUBLANES
                if (t0 // h) % 2 == 1:
                    pair = t0 // (2 * h)
                    r_up = pair * h + (t0 - pair * 2 * h - h)
                    blk = jnp.where(lane < pair * 2 * h + h, a_up[l][r_up:r_up + SUBLANES], blk)
            tiles.append(blk)
        att = jnp.concatenate(tiles, axis=0)

        o = jnp.dot(att.astype(BF16), v, preferred_element_type=F32)
        o += lax.dot_general(qd, st.astype(BF16), nt, preferred_element_type=F32)
        st = st * jnp.exp2(jnp.minimum(b_last, 0.0)) + lax.dot_general(
            v, kd, (((0,), (0,)), ((), ())), preferred_element_type=F32)

        o = o * lax.rsqrt(jnp.mean(o * o, axis=-1, keepdims=True) + EPS) * gn
        y_ref[rows, :] = (o * _silu(zo_ref[rows, :])).astype(y_ref.dtype)
        return st

    st = lax.fori_loop(0, n_chunks, chunk, jnp.zeros((HG_DIM, HG_DIM), F32), unroll=16)
    s_ref[0, 0] = st.T


def _hgrn_prompt(z, lb, gn, *, layer, batch, seq, cast_jobs=()):
    hw = HG_HEADS * HG_DIM
    plans = [_cast_job_plan(job, batch * HG_HEADS, lambda n, h: n * HG_HEADS + h)
             for job in cast_jobs]
    per_blk = z.shape[2] // HG_DIM
    col = lambda off: pl.BlockSpec(
        (None, seq, HG_DIM), lambda n, h, off=off: ((off + h) // per_blk, n, (off + h) % per_blk))
    per_head = pl.BlockSpec((None, 1, HG_DIM), lambda n, h: (layer, 0, h))
    tri = jnp.asarray(_prefix_sum_matrix(), BF16)
    return pl.pallas_call(
        functools.partial(_hgrn_prompt_kernel, n_jobs=len(plans)),
        grid=(batch, HG_HEADS),
        in_specs=[col(0), col(HG_HEADS), col(2 * HG_HEADS), col(3 * HG_HEADS), per_head, per_head,
                  pl.BlockSpec(tri.shape, lambda n, h: (0, 0))] + [p[0] for p in plans],
        out_specs=[pl.BlockSpec((seq, HG_DIM), lambda n, h: (n, h)),
                   pl.BlockSpec((1, 1, HG_DIM, HG_DIM), lambda n, h: (n, h, 0, 0))]
        + [p[1] for p in plans],
        out_shape=[jax.ShapeDtypeStruct((batch * seq, hw), BF16),
                   jax.ShapeDtypeStruct((batch, HG_HEADS, HG_DIM, HG_DIM), F32)]
        + [p[2] for p in plans],
        compiler_params=_params("arbitrary", "arbitrary"),
        name="hgrn_prompt",
    )(z, z, z, z, lb, gn, tri, *[job.w for job in cast_jobs])


def _conv_out_proj_kernel(x_ref, mh_ref, zb_ref, zc_ref, zh_ref, zv_ref, zg_ref, scw_ref, ccw_ref,
                          ccb_ref, lng_ref, lnb_ref, wh_ref, ws_ref, wc_ref,
                          o_ref, ns_ref, nc_ref,
                          u_ref, a_ref, ms_next, mc_next, ms_cur, mc_cur, *, tiles_per_seq):
    s = pl.program_id(0)
    t_blk = zb_ref.shape[0]
    u_halo = SUBLANES
    a_halo = CONV_ROWS

    @pl.when(s == 0)
    def _():
        ms_next[...] = jnp.zeros(ms_next.shape, ms_next.dtype)
        mc_next[...] = jnp.zeros(mc_next.shape, mc_next.dtype)
        u_ref[...] = jnp.zeros(u_ref.shape, u_ref.dtype)
        a_ref[...] = jnp.zeros(a_ref.shape, a_ref.dtype)

    ms_cur[...] = ms_next[...]
    mc_cur[...] = mc_next[...]
    acc = x_ref[...]
    acc += jnp.dot(mh_ref[...], wh_ref[...], preferred_element_type=F32)
    acc += jnp.dot(ms_cur[...], ws_ref[...], preferred_element_type=F32)
    acc += jnp.dot(mc_cur[...], wc_ref[...], preferred_element_type=F32)
    o_ref[...] = acc

    tile = jnp.minimum(s, pl.num_programs(0) - 2)
    seq_start = lax.rem(tile, tiles_per_seq) == 0
    u_ref[0:u_halo, :] = jnp.where(seq_start, 0.0, u_ref[t_blk:t_blk + u_halo, :])
    a_ref[0:a_halo, :] = jnp.where(seq_start, 0.0, a_ref[t_blk:t_blk + a_halo, :])
    u_ref[u_halo:, :] = zc_ref[...] * zh_ref[...]
    a_ref[a_halo:, :] = zv_ref[...] * _sigmoid(zg_ref[...])
    ccb = ccb_ref[...]
    lng = lng_ref[...]
    lnb = lnb_ref[...]
    for i in range(t_blk // CONV_ROWS):
        base = i * CONV_ROWS
        rows = slice(base, base + CONV_ROWS)
        uw = u_ref[base:base + u_halo + CONV_ROWS, :]
        acc = scw_ref[SC_TAPS - 1:SC_TAPS, :] * uw[u_halo:, :]
        for j in range(SC_TAPS - 1):
            off = u_halo - (SC_TAPS - 1) + j
            acc += scw_ref[j:j + 1, :] * uw[off:off + CONV_ROWS, :]
        ms_next[rows, :] = (zb_ref[rows, :] * acc).astype(ms_next.dtype)

        strips = []
        for st in range(a_ref.shape[1] // LANES):
            lanes = slice(st * LANES, (st + 1) * LANES)
            aw = a_ref[base:base + a_halo + CONV_ROWS, lanes]
            acc = ccb[:, lanes]
            for res in range(SUBLANES):
                shifted = aw if res == 0 else pltpu.roll(aw, a_halo + CONV_ROWS - res, axis=0)
                for j in range(CC_TAPS):
                    off = a_halo - (CC_TAPS - 1) + j
                    if off % SUBLANES == res:
                        acc += ccw_ref[j:j + 1, lanes] * shifted[off - res:off - res + CONV_ROWS, :]
            strips.append(acc)
        acc = jnp.concatenate(strips, axis=1)
        mu = jnp.mean(acc, axis=-1, keepdims=True)
        d = acc - mu
        var = jnp.mean(d * d, axis=-1, keepdims=True)
        mc_next[rows, :] = _silu(d * lax.rsqrt(var + EPS) * lng + lnb).astype(mc_next.dtype)

    ns_ref[0] = u_ref[u_halo + t_blk - (SC_TAPS - 1):u_halo + t_blk, :]
    nc_ref[0] = a_ref[a_halo + t_blk - (CC_TAPS - 1):a_halo + t_blk, :]


def _conv_out_proj(x, mh, z, scw, ccw, ccb, lng, lnb, w_out, *, layer, seq, sc_off, cc_off):
    m, d = x.shape
    hw = mh.shape[1]
    w = scw.shape[2]
    t_blk = CONV_TIME_BLOCK
    n_tiles = m // t_blk
    tiles_per_seq = seq // t_blk
    conv_tile = lambda s: jnp.minimum(s, n_tiles - 1)
    proj_tile = lambda s: jnp.maximum(s - 1, 0)
    assert z.shape[2] == w
    zcol = lambda off: pl.BlockSpec((None, t_blk, w), lambda s, off=off: (off, conv_tile(s), 0))
    prow = lambda width: pl.BlockSpec((t_blk, width), lambda s: (proj_tile(s), 0))
    whole = lambda a: _layer_block(a, layer)
    w_rows = lambda rows, blk: pl.BlockSpec((rows, d), lambda s: (blk, 0))
    state = lambda taps: pl.BlockSpec((1, taps - 1, w),
                                      lambda s: (conv_tile(s) // tiles_per_seq, 0, 0))
    return pl.pallas_call(
        functools.partial(_conv_out_proj_kernel, tiles_per_seq=tiles_per_seq),
        grid=(n_tiles + 1,),
        in_specs=[prow(d), prow(hw),
                  zcol(sc_off), zcol(sc_off + 1), zcol(sc_off + 2), zcol(cc_off), zcol(cc_off + 1),
                  whole(scw), whole(ccw), whole(ccb), whole(lng), whole(lnb),
                  w_rows(hw, 0), w_rows(w, hw // w), w_rows(w, hw // w + 1)],
        out_specs=[prow(d), state(SC_TAPS), state(CC_TAPS)],
        out_shape=[jax.ShapeDtypeStruct((m, d), F32),
                   jax.ShapeDtypeStruct((m // seq, SC_TAPS - 1, w), F32),
                   jax.ShapeDtypeStruct((m // seq, CC_TAPS - 1, w), F32)],
        scratch_shapes=[pltpu.VMEM((SUBLANES + t_blk, w), F32),
                        pltpu.VMEM((CONV_ROWS + t_blk, w), F32),
                        pltpu.VMEM((t_blk, w), BF16), pltpu.VMEM((t_blk, w), BF16),
                        pltpu.VMEM((t_blk, w), BF16), pltpu.VMEM((t_blk, w), BF16)],
        compiler_params=_params("arbitrary"),
        name="conv_out_proj",
    )(x, mh, z, z, z, z, z, scw, ccw, ccb, lng, lnb, w_out, w_out, w_out)


def _mix_sample_kernel(*refs):
    so_ref, nc_ref = refs[-5], refs[-3]

    @pl.when(pl.program_id(0) == 0)
    def _():
        _mix_sample_step(*refs[:11], *refs[-6:])

    @pl.when(pl.program_id(0) > 0)
    def _():
        so_ref[...] = jnp.zeros(so_ref.shape, so_ref.dtype)
        nc_ref[...] = jnp.zeros(nc_ref.shape, nc_ref.dtype)


def _mix_sample_step(z_ref, st_ref, ss_ref, sc_ref, lb_ref, gn_ref, scw_ref, ccw_ref, ccb_ref,
                     lng_ref, lnb_ref, mix_ref, so_ref, ns_ref, nc_ref, oh_ref, cv_ref):
    hw = HG_HEADS * HG_DIM
    w = scw_ref.shape[1]
    eye = (lax.broadcasted_iota(jnp.int32, (HG_DIM, HG_DIM), 0)
           == lax.broadcasted_iota(jnp.int32, (HG_DIM, HG_DIM), 1))

    def column(x_row):
        return jnp.sum(jnp.where(eye, x_row, 0.0), axis=1, keepdims=True)

    n_tok = z_ref.shape[1]
    zw = z_ref.shape[2]

    def zcols(lo, hi):
        parts = [z_ref[c, :, max(lo - c * zw, 0):min(hi - c * zw, zw)]
                 for c in range(lo // zw, (hi - 1) // zw + 1)]
        return parts[0] if len(parts) == 1 else jnp.concatenate(parts, axis=1)

    q_all = _silu(zcols(0, hw))
    f_all, k_all = _forget_and_key(zcols(hw, 2 * hw), *_gate_constants(lb_ref[...]))
    decay_all = jnp.exp(jnp.log(jnp.maximum(f_all, F_FLOOR)))
    v_all = zcols(2 * hw, 3 * hw)
    for t in range(n_tok):
        tr = slice(t, t + 1)
        for h in range(HG_HEADS):
            cols = slice(h * HG_DIM, (h + 1) * HG_DIM)
            s_new = (column(decay_all[tr, cols]) * st_ref[t, h]
                     + column(k_all[tr, cols]) * v_all[tr, cols])
            so_ref[t, h] = s_new
            oh_ref[tr, cols] = jnp.dot(q_all[tr, cols].astype(BF16), s_new.astype(BF16),
                                       preferred_element_type=F32)
    gate_all = _silu(zcols(3 * hw, 4 * hw))
    for h in range(HG_HEADS):
        cols = slice(h * HG_DIM, (h + 1) * HG_DIM)
        o = oh_ref[:, cols]
        o = o * lax.rsqrt(jnp.mean(o * o, axis=-1, keepdims=True) + EPS) * gn_ref[:, cols]
        oh_ref[:, cols] = o * gate_all[:, cols]
    mix_ref[:, 0:hw] = oh_ref[...].astype(mix_ref.dtype)

    off = 4 * hw
    zb = zcols(off, off + w)
    u = zcols(off + w, off + 2 * w) * zcols(off + 2 * w, off + 3 * w)
    conv = scw_ref[SC_TAPS - 1:SC_TAPS, :] * u
    for j in range(SC_TAPS - 1):
        conv += scw_ref[j:j + 1, :] * ss_ref[:, j * w:(j + 1) * w]
    mix_ref[:, hw:hw + w] = (zb * conv).astype(mix_ref.dtype)
    for j in range(SC_TAPS - 2):
        ns_ref[:, j * w:(j + 1) * w] = ss_ref[:, (j + 1) * w:(j + 2) * w]
    ns_ref[:, (SC_TAPS - 2) * w:] = u

    off = 4 * hw + 3 * w
    a = zcols(off, off + w) * _sigmoid(zcols(off + w, off + 2 * w))
    past_taps = ccw_ref[0:CC_TAPS - 1, :]
    for t in range(n_tok):
        cv_ref[t:t + 1, :] = jnp.sum(past_taps * sc_ref[t], axis=0, keepdims=True)
        nc_ref[t, 0:CC_TAPS - 2, :] = sc_ref[t, 1:CC_TAPS - 1, :]
        nc_ref[t, CC_TAPS - 2:CC_TAPS - 1, :] = a[t:t + 1, :]
    conv = ccb_ref[...] + ccw_ref[CC_TAPS - 1:CC_TAPS, :] * a + cv_ref[...]
    mu = jnp.mean(conv, axis=-1, keepdims=True)
    d = conv - mu
    var = jnp.mean(d * d, axis=-1, keepdims=True)
    y = d * lax.rsqrt(var + EPS) * lng_ref[...] + lnb_ref[...]
    mix_ref[:, hw + w:] = _silu(y).astype(mix_ref.dtype)


def _mix_sample(z, st_h, st_s, st_c, lb, gn, scw, ccw, ccb, lng, lnb, *, layer, prev):
    nb = z.shape[1]
    tb = SAMPLE_TOKENS_PER_STEP
    hw = HG_HEADS * HG_DIM
    w = scw.shape[2]
    nblk = nb // tb
    n_pass = st_h.shape[0] - layer if prev is None else 1
    tok = lambda p, i: jnp.where(p == 0, i, nblk - 1)
    rows = lambda width: pl.BlockSpec((tb, width), lambda p, i: (tok(p, i), 0))
    layer_rows = lambda width: pl.BlockSpec((None, tb, width), lambda p, i: (layer, tok(p, i), 0))
    whole = lambda a: _layer_block(a, layer)
    state_tile = (None, tb, HG_HEADS, HG_DIM, HG_DIM)
    conv_tile = (None, tb) + st_c.shape[2:]
    args = [z, st_h, st_s, st_c, lb, gn, scw, ccw, ccb, lng, lnb]
    in_specs = [pl.BlockSpec((z.shape[0], tb, z.shape[2]), lambda p, i: (0, tok(p, i), 0)),
                pl.BlockSpec(state_tile, lambda p, i: (layer, tok(p, i), 0, 0, 0)),
                layer_rows(st_s.shape[2]),
                pl.BlockSpec(conv_tile, lambda p, i: (layer, tok(p, i), 0, 0)),
                whole(lb), whole(gn), whole(scw), whole(ccw), whole(ccb), whole(lng), whole(lnb)]
    aliases = {}
    if prev is not None:
        aliases = {len(args): 1, len(args) + 1: 3}
        args += list(prev)
        in_specs += [pl.BlockSpec(memory_space=pl.ANY)] * 2
    return pl.pallas_call(
        _mix_sample_kernel,
        grid=(n_pass, nblk),
        in_specs=in_specs,
        out_specs=[rows(hw + 2 * w),
                   pl.BlockSpec(state_tile, lambda p, i: (layer + p, i, 0, 0, 0)),
                   rows(st_s.shape[2]),
                   pl.BlockSpec(conv_tile, lambda p, i: (layer + p, i, 0, 0))],
        out_shape=[jax.ShapeDtypeStruct((nb, hw + 2 * w), BF16),
                   jax.ShapeDtypeStruct(st_h.shape, F32),
                   jax.ShapeDtypeStruct((nb, st_s.shape[2]), F32),
                   jax.ShapeDtypeStruct(st_c.shape, F32)],
        scratch_shapes=[pltpu.VMEM((tb, hw), F32), pltpu.VMEM((tb, w), F32)],
        input_output_aliases=aliases,
        compiler_params=_params("arbitrary", "arbitrary"),
        name="mix_sample",
    )(*args)


def kernel(x_prompt, x_sample, state_hgrn, state_sconv, state_cconv, g_mix, w_in, hgrn_lb,
           hgrn_norm_g, sconv_w, cconv_w, cconv_b, cconv_ln_g, cconv_ln_b, w_out, g_mlp,
           w_up, w_down, g_final):
    batch, seq, d = x_prompt.shape
    nb = x_sample.shape[0]
    depth = w_in.shape[0]
    hw = HG_HEADS * HG_DIM
    w = sconv_w.shape[2]
    assert state_hgrn.shape[2:] == (HG_HEADS, HG_DIM, HG_DIM)
    assert w_in.shape[2] == 4 * hw + 5 * w and w_out.shape[1] == hw + 2 * w
    assert sconv_w.shape[1] == SC_TAPS and cconv_w.shape[1] == CC_TAPS
    assert seq % CONV_TIME_BLOCK == 0 and seq % CHUNK == 0 and nb % SAMPLE_TOKENS_PER_STEP == 0

    p = jax.nn.softmax(hgrn_lb.astype(F32), axis=0)
    lb_all = jnp.cumsum(p, axis=0) - p[0:1]

    rows = lambda a: a.reshape(depth, 1, -1)
    g1, g2, gf = rows(g_mix), rows(g_mlp), g_final.reshape(1, -1)
    lb, gn = rows(lb_all), rows(hgrn_norm_g)
    ccb, lng, lnb = rows(cconv_b), rows(cconv_ln_g), rows(cconv_ln_b)

    xp = x_prompt.reshape(batch * seq, d)
    xs = x_sample.reshape(nb, d)
    st_s = state_sconv.reshape(depth, nb, (SC_TAPS - 1) * w)
    sc_off = 4 * hw // w
    cc_off = sc_off + 3

    ph, ps, pc, ss = [], [], [], []
    sample_states = None
    for li in range(depth):
        last = li == depth - 1

        if li == 0:
            zs, w_in_b = _norm_matmul_cast(xs, g1, w_in, layer=li, tn=WEIGHT_BLOCK)
        else:
            zs = _norm_matmul(xs, g1, w_in_b, layer=li, tm=nb)
        mix, new_h, new_s, new_c = _mix_sample(
            zs, state_hgrn, st_s, state_cconv, lb, gn, sconv_w, cconv_w, ccb, lng, lnb, layer=li,
            prev=sample_states)
        sample_states = (new_h, new_c)
        ss.append(new_s.reshape(nb, SC_TAPS - 1, w))

        z = _norm_matmul(xp, g1, w_in_b, layer=li, tm=2048)
        jobs = [_CastJob(w_out, li, None), _CastJob(w_up, li, WEIGHT_BLOCK),
                _CastJob(w_down, li, None)] if li == 0 else []
        mh, new_h, *cast = _hgrn_prompt(z, lb, gn, layer=li, batch=batch, seq=seq, cast_jobs=jobs)
        if li == 0:
            w_out_b, w_up_b, w_down_b = cast

        xs = _out_proj_sample(xs, mix, w_out_b, tk=WEIGHT_BLOCK)
        xs, = _ffn(xs, g2, w_up_b, w_down_b, gf, layer=li, tm=nb, final_norm=last)

        xp, new_s, new_c = _conv_out_proj(xp, mh, z, sconv_w, cconv_w, ccb, lng, lnb, w_out_b,
                                          layer=li, seq=seq, sc_off=sc_off, cc_off=cc_off)
        jobs = [] if last else [_CastJob(w_in, li + 1, WEIGHT_BLOCK), _CastJob(w_out, li + 1, None),
                                _CastJob(w_up, li + 1, WEIGHT_BLOCK), _CastJob(w_down, li + 1, None)]
        xp, *cast = _ffn(xp, g2, w_up_b, w_down_b, gf, layer=li, tm=1024, final_norm=last,
                         cast_jobs=jobs)
        if not last:
            w_in_b, w_out_b, w_up_b, w_down_b = cast
        ph.append(new_h)
        ps.append(new_s)
        pc.append(new_c)

    sh, sc = sample_states
    return (xp.reshape(batch, seq, d), xs.reshape(nb, 1, d), jnp.stack(ph), jnp.stack(ps),
            jnp.stack(pc), sh, jnp.stack(ss), sc)
```

```python
import functools
from typing import NamedTuple, Optional

import numpy as np
import jax
import jax.numpy as jnp
from jax import lax
from jax.experimental import pallas as pl
from jax.experimental.pallas import tpu as pltpu

F32 = jnp.float32
BF16 = jnp.bfloat16

EPS = 1e-6
F_FLOOR = 1e-30

LANES = 128
SUBLANES = 8
BF16_SUBLANES = 16
VMEM_LIMIT_BYTES = 56 * 1024 * 1024
IN_PROJ_VMEM_LIMIT_BYTES = 58 * 1024 * 1024

HG_HEADS = 8
HG_DIM = 128
SC_TAPS = 3
CC_TAPS = 31
CHUNK = 128
N_LEVELS = 7
SAMPLE_TOKENS_PER_STEP = 8
WEIGHT_BLOCK = 512
CONV_TIME_BLOCK = 512
CONV_ROWS = 32


def _params(*sem, vmem_limit_bytes=VMEM_LIMIT_BYTES):
    return pltpu.CompilerParams(dimension_semantics=sem, vmem_limit_bytes=vmem_limit_bytes)


def _layer_block(a, layer):
    zeros = (0,) * (a.ndim - 1)
    return pl.BlockSpec((None,) + a.shape[1:], lambda *_: (layer,) + zeros)


def _rmsnorm(x, g):
    return x * lax.rsqrt(jnp.mean(x * x, axis=-1, keepdims=True) + EPS) * g


def _sigmoid(x):
    return 0.5 * jnp.tanh(0.5 * x) + 0.5


def _silu(x):
    h = 0.5 * x
    return h + h * jnp.tanh(h)


def _gate_constants(lb):
    f_scale = 0.5 * (1.0 - lb)
    return f_scale, lb + f_scale


def _forget_and_key(z, f_scale, f_shift):
    w = f_scale * jnp.tanh(0.5 * z)
    return f_shift + w, f_scale - w


class _CastJob(NamedTuple):
    w: jax.Array
    layer: int
    col_block: Optional[int]


def _cast_job_plan(job, n_steps, step_of):
    _, rows, cols = job.w.shape
    chunk_rows = BF16_SUBLANES
    while rows % chunk_rows or rows // chunk_rows > n_steps:
        chunk_rows += BF16_SUBLANES
    n_chunks = rows // chunk_rows
    chunk = lambda *idx: jnp.minimum(step_of(*idx), n_chunks - 1)
    in_spec = pl.BlockSpec((None, chunk_rows, cols), lambda *idx: (job.layer, chunk(*idx), 0))
    if job.col_block is None:
        out_spec = pl.BlockSpec((chunk_rows, cols), lambda *idx: (chunk(*idx), 0))
        return in_spec, out_spec, jax.ShapeDtypeStruct((rows, cols), BF16)
    nblk = cols // job.col_block
    out_spec = pl.BlockSpec((nblk, chunk_rows, job.col_block), lambda *idx: (0, chunk(*idx), 0))
    return in_spec, out_spec, jax.ShapeDtypeStruct((nblk, rows, job.col_block), BF16)


def _run_cast_jobs(src_refs, dst_refs):
    for src, dst in zip(src_refs, dst_refs):
        if len(dst.shape) == 2:
            dst[...] = src[...].astype(dst.dtype)
        else:
            width = dst.shape[2]
            for c in range(dst.shape[0]):
                dst[c] = src[:, c * width:(c + 1) * width].astype(dst.dtype)


def _norm_matmul_kernel(x_ref, g_ref, w_ref, o_ref, h_ref):
    @pl.when(pl.program_id(1) == 0)
    def _():
        h_ref[...] = _rmsnorm(x_ref[...], g_ref[...]).astype(BF16)

    o_ref[...] = jnp.dot(h_ref[...], w_ref[...], preferred_element_type=F32)


def _norm_matmul(x, g, wb, *, layer, tm):
    m, k = x.shape
    nblk, _, tn = wb.shape
    return pl.pallas_call(
        _norm_matmul_kernel,
        grid=(m // tm, nblk),
        in_specs=[pl.BlockSpec((tm, k), lambda i, j: (i, 0)),
                  _layer_block(g, layer),
                  pl.BlockSpec((None, k, tn), lambda i, j: (j, 0, 0))],
        out_specs=pl.BlockSpec((None, tm, tn), lambda i, j: (j, i, 0)),
        out_shape=jax.ShapeDtypeStruct((nblk, m, tn), F32),
        scratch_shapes=[pltpu.VMEM((tm, k), BF16)],
        compiler_params=_params("arbitrary", "arbitrary",
                                vmem_limit_bytes=IN_PROJ_VMEM_LIMIT_BYTES),
        name="in_proj",
    )(x, g, wb)


def _norm_matmul_cast_kernel(x_ref, g_ref, w_ref, o_ref, wb_ref, h_ref):
    @pl.when(pl.program_id(0) == 0)
    def _():
        h_ref[...] = _rmsnorm(x_ref[...], g_ref[...]).astype(BF16)

    wb = w_ref[...].astype(BF16)
    wb_ref[...] = wb
    o_ref[...] = jnp.dot(h_ref[...], wb, preferred_element_type=F32)


def _norm_matmul_cast(x, g, w, *, layer, tn):
    m, k = x.shape
    n = w.shape[2]
    return pl.pallas_call(
        _norm_matmul_cast_kernel,
        grid=(n // tn,),
        in_specs=[pl.BlockSpec((m, k), lambda j: (0, 0)),
                  _layer_block(g, layer),
                  pl.BlockSpec((None, k, tn), lambda j: (layer, 0, j))],
        out_specs=[pl.BlockSpec((None, m, tn), lambda j: (j, 0, 0)),
                   pl.BlockSpec((None, k, tn), lambda j: (j, 0, 0))],
        out_shape=[jax.ShapeDtypeStruct((n // tn, m, tn), F32),
                   jax.ShapeDtypeStruct((n // tn, k, tn), BF16)],
        scratch_shapes=[pltpu.VMEM((m, k), BF16)],
        compiler_params=_params("arbitrary"),
        name="in_proj_cast",
    )(x, g, w)


def _out_proj_sample_kernel(x_ref, m_ref, w_ref, o_ref):
    @pl.when(pl.program_id(0) == 0)
    def _():
        o_ref[...] = x_ref[...]

    o_ref[...] += jnp.dot(m_ref[...], w_ref[...], preferred_element_type=F32)


def _out_proj_sample(x, mix, w_out_b, *, tk):
    m, d = x.shape
    kdim = mix.shape[1]
    return pl.pallas_call(
        _out_proj_sample_kernel,
        grid=(kdim // tk,),
        in_specs=[pl.BlockSpec((m, d), lambda r: (0, 0)),
                  pl.BlockSpec((m, tk), lambda r: (0, r)),
                  pl.BlockSpec((tk, d), lambda r: (r, 0))],
        out_specs=pl.BlockSpec((m, d), lambda r: (0, 0)),
        out_shape=jax.ShapeDtypeStruct((m, d), F32),
        compiler_params=_params("arbitrary"),
        name="out_proj_sample",
    )(x, mix, w_out_b)


def _ffn_kernel(x_ref, g_ref, wu_ref, wd_ref, gf_ref, xs_ref, *rest, final_norm, n_jobs):
    cast_src, (o_ref, ys_ref), cast_dst = rest[:n_jobs], rest[n_jobs:n_jobs + 2], rest[n_jobs + 2:-2]
    h_ref, hs_ref = rest[-2:]
    j = pl.program_id(1)

    def mlp_step(src_ref, acc_ref, hid_ref):
        @pl.when(j == 0)
        def _():
            x = src_ref[...]
            hid_ref[...] = _rmsnorm(x, g_ref[...]).astype(BF16)
            acc_ref[...] = x

        u = jnp.dot(hid_ref[...], wu_ref[...], preferred_element_type=F32)
        r = jnp.maximum(u, 0.0)
        acc_ref[...] += jnp.dot((r * r).astype(BF16), wd_ref[...], preferred_element_type=F32)

        if final_norm:
            @pl.when(j == pl.num_programs(1) - 1)
            def _():
                acc_ref[...] = _rmsnorm(acc_ref[...], gf_ref[...])

    mlp_step(x_ref, o_ref, h_ref)

    @pl.when(pl.program_id(0) == 0)
    def _():
        mlp_step(xs_ref, ys_ref, hs_ref)

    _run_cast_jobs(cast_src, cast_dst)


def _ffn(x, xs, g, wub, wdb, gf, *, layer, tm, final_norm, cast_jobs=()):
    m, d = x.shape
    ms = xs.shape[0]
    nblk, _, tf = wub.shape
    grid = (m // tm, nblk)
    plans = [_cast_job_plan(job, grid[0] * grid[1], lambda i, j: i * nblk + j)
             for job in cast_jobs]
    resident = pl.BlockSpec((ms, d), lambda i, j: (0, 0))
    return pl.pallas_call(
        functools.partial(_ffn_kernel, final_norm=final_norm, n_jobs=len(plans)),
        grid=grid,
        in_specs=[pl.BlockSpec((tm, d), lambda i, j: (i, 0)),
                  _layer_block(g, layer),
                  pl.BlockSpec((None, d, tf), lambda i, j: (j, 0, 0)),
                  pl.BlockSpec((tf, d), lambda i, j: (j, 0)),
                  pl.BlockSpec((1, d), lambda i, j: (0, 0)),
                  resident] + [p[0] for p in plans],
        out_specs=[pl.BlockSpec((tm, d), lambda i, j: (i, 0)), resident] + [p[1] for p in plans],
        out_shape=[jax.ShapeDtypeStruct((m, d), F32), jax.ShapeDtypeStruct((ms, d), F32)]
        + [p[2] for p in plans],
        scratch_shapes=[pltpu.VMEM((tm, d), BF16), pltpu.VMEM((ms, d), BF16)],
        compiler_params=_params("arbitrary", "arbitrary"),
        name="ffn",
    )(x, g, wub, wdb, gf, xs, *[job.w for job in cast_jobs])


LOG2E = 1.4426950408889634
LOW_LEVELS = 3


def _prefix_sum_matrix():
    t = np.arange(CHUNK)[:, None]
    r = np.arange(CHUNK)[None, :]
    return (r <= t).astype(np.float32)


def _hgrn_prompt_kernel(zq_ref, zf_ref, zi_ref, zo_ref, lb_ref, gn_ref, tri_ref, *rest, n_jobs):
    cast_src, (y_ref, s_ref), cast_dst = rest[:n_jobs], rest[n_jobs:n_jobs + 2], rest[n_jobs + 2:]
    _run_cast_jobs(cast_src, cast_dst)
    n_chunks = zq_ref.shape[0] // CHUNK
    n_tiles = CHUNK // SUBLANES
    lb = lb_ref[...]
    gn = gn_ref[...]
    tri = tri_ref[...]
    f_scale, f_shift = _gate_constants(lb)
    sub = lax.broadcasted_iota(jnp.int32, (SUBLANES, HG_DIM), 0)
    lane = lax.broadcasted_iota(jnp.int32, (SUBLANES, HG_DIM), 1)
    col = lane & (SUBLANES - 1)
    below = col < sub
    pat_low = [below & (((sub ^ col) >> l) == 1) for l in range(LOW_LEVELS)]
    pat_diag = col == sub
    upper_low = [((sub >> l) & 1) == 1 for l in range(LOW_LEVELS)]
    nt = (((1,), (1,)), ((), ()))
    tiled = lambda x: x.reshape(n_tiles, SUBLANES, HG_DIM)
    flat = lambda x: x.reshape(CHUNK, HG_DIM)

    def tile_row(x3, s):
        return jnp.broadcast_to(x3[:, s:s + 1, :], x3.shape)

    def chunk(c, st):
        rows = pl.ds(pl.multiple_of(c * CHUNK, CHUNK), CHUNK)
        q = _silu(zq_ref[rows, :])
        f, kk = _forget_and_key(zf_ref[rows, :], f_scale, f_shift)
        fc = jnp.maximum(f, F_FLOOR)
        g = jnp.log(fc) * LOG2E
        v = zi_ref[rows, :].astype(BF16)

        g_hi = g.astype(BF16)
        g_lo = (g - g_hi.astype(F32)).astype(BF16)
        b2 = jnp.dot(tri, jnp.concatenate([g_hi, g_lo], axis=1), preferred_element_type=F32)
        b = b2[:, :HG_DIM] + b2[:, HG_DIM:]
        b_last = b[CHUNK - 1:CHUNK, :]
        qd = (q * jnp.exp2(jnp.minimum(b, 0.0))).astype(BF16)
        kd = (kk * jnp.exp2(jnp.minimum(b_last - b, 0.0))).astype(BF16)

        half = SUBLANES // 2
        b3, q3, k3 = tiled(b), tiled(q), tiled(kk)
        ref_low = [None,
                   jnp.where(sub < half, tile_row(b3, 1), tile_row(b3, half + 1)),
                   tile_row(b3, half - 1)]
        a_low = []
        for l in range(LOW_LEVELS):
            upper = upper_low[l]
            if l == 0:
                x3 = jnp.where(upper, q3 * tiled(fc), k3)
            else:
                x3 = jnp.exp2(-jnp.abs(b3 - ref_low[l])) * jnp.where(upper, q3, k3)
            x = flat(x3).astype(BF16)
            a_low.append(lax.dot_general(x, x, nt, preferred_element_type=F32))

        a_up = {}
        for l in range(LOW_LEVELS, N_LEVELS):
            h = 1 << l
            pairs = CHUNK // (2 * h)
            bg = b.reshape(pairs, 2, h, HG_DIM)
            ref = bg[:, 0, h - 1:h, :]
            x_lo = kk.reshape(pairs, 2, h, HG_DIM)[:, 0] * jnp.exp2(jnp.minimum(ref - bg[:, 0], 0.0))
            x_up = q.reshape(pairs, 2, h, HG_DIM)[:, 1] * jnp.exp2(jnp.minimum(bg[:, 1] - ref, 0.0))
            x_all = jnp.stack([x_lo, x_up], axis=1).reshape(CHUNK, HG_DIM).astype(BF16)
            a_up[l] = lax.dot_general(x_up.reshape(CHUNK // 2, HG_DIM).astype(BF16), x_all, nt,
                                      preferred_element_type=F32)

        dqk = jnp.sum(q * kk, axis=1, keepdims=True)
        tiles = []
        for j in range(n_tiles):
            r = slice(j * SUBLANES, (j + 1) * SUBLANES)
            blk = jnp.where(pat_diag, dqk[r], 0.0)
            for l in range(LOW_LEVELS):
                blk = jnp.where(pat_low[l], a_low[l][r], blk)
            blk = jnp.where((lane >> 3) == j, blk, 0.0)
            for l in range(LOW_LEVELS, N_LEVELS):
                h = 1 << l
                t0 = j * SUBLANES
                if (t0 // h) % 2 == 1:
                    pair = t0 // (2 * h)
                    r_up = pair * h + (t0 - pair * 2 * h - h)
                    blk = jnp.where(lane < pair * 2 * h + h, a_up[l][r_up:r_up + SUBLANES], blk)
            tiles.append(blk)
        att = jnp.concatenate(tiles, axis=0)

        o = jnp.dot(att.astype(BF16), v, preferred_element_type=F32)
        o += lax.dot_general(qd, st.astype(BF16), nt, preferred_element_type=F32)
        st = st * jnp.exp2(jnp.minimum(b_last, 0.0)) + lax.dot_general(
            v, kd, (((0,), (0,)), ((), ())), preferred_element_type=F32)

        o = o * lax.rsqrt(jnp.mean(o * o, axis=-1, keepdims=True) + EPS) * gn
        y_ref[rows, :] = (o * _silu(zo_ref[rows, :])).astype(y_ref.dtype)
        return st

    st = lax.fori_loop(0, n_chunks, chunk, jnp.zeros((HG_DIM, HG_DIM), F32), unroll=16)
    s_ref[0, 0] = st.T


def _hgrn_prompt(z, lb, gn, *, layer, batch, seq, cast_jobs=()):
    hw = HG_HEADS * HG_DIM
    plans = [_cast_job_plan(job, batch * HG_HEADS, lambda n, h: n * HG_HEADS + h)
             for job in cast_jobs]
    per_blk = z.shape[2] // HG_DIM
    col = lambda off: pl.BlockSpec(
        (None, seq, HG_DIM), lambda n, h, off=off: ((off + h) // per_blk, n, (off + h) % per_blk))
    per_head = pl.BlockSpec((None, 1, HG_DIM), lambda n, h: (layer, 0, h))
    tri = jnp.asarray(_prefix_sum_matrix(), BF16)
    return pl.pallas_call(
        functools.partial(_hgrn_prompt_kernel, n_jobs=len(plans)),
        grid=(batch, HG_HEADS),
        in_specs=[col(0), col(HG_HEADS), col(2 * HG_HEADS), col(3 * HG_HEADS), per_head, per_head,
                  pl.BlockSpec(tri.shape, lambda n, h: (0, 0))] + [p[0] for p in plans],
        out_specs=[pl.BlockSpec((seq, HG_DIM), lambda n, h: (n, h)),
                   pl.BlockSpec((1, 1, HG_DIM, HG_DIM), lambda n, h: (n, h, 0, 0))]
        + [p[1] for p in plans],
        out_shape=[jax.ShapeDtypeStruct((batch * seq, hw), BF16),
                   jax.ShapeDtypeStruct((batch, HG_HEADS, HG_DIM, HG_DIM), F32)]
        + [p[2] for p in plans],
        compiler_params=_params("arbitrary", "arbitrary"),
        name="hgrn_prompt",
    )(z, z, z, z, lb, gn, tri, *[job.w for job in cast_jobs])


def _conv_out_proj_kernel(x_ref, mh_ref, zb_ref, zc_ref, zh_ref, zv_ref, zg_ref, scw_ref, ccw_ref,
                          ccb_ref, lng_ref, lnb_ref, wh_ref, ws_ref, wc_ref,
                          o_ref, ns_ref, nc_ref,
                          u_ref, a_ref, ms_next, mc_next, ms_cur, mc_cur, *, tiles_per_seq):
    s = pl.program_id(0)
    t_blk = zb_ref.shape[0]
    u_halo = SUBLANES
    a_halo = CONV_ROWS

    @pl.when(s == 0)
    def _():
        ms_next[...] = jnp.zeros(ms_next.shape, ms_next.dtype)
        mc_next[...] = jnp.zeros(mc_next.shape, mc_next.dtype)
        u_ref[...] = jnp.zeros(u_ref.shape, u_ref.dtype)
        a_ref[...] = jnp.zeros(a_ref.shape, a_ref.dtype)

    ms_cur[...] = ms_next[...]
    mc_cur[...] = mc_next[...]
    acc = x_ref[...]
    acc += jnp.dot(mh_ref[...], wh_ref[...], preferred_element_type=F32)
    acc += jnp.dot(ms_cur[...], ws_ref[...], preferred_element_type=F32)
    acc += jnp.dot(mc_cur[...], wc_ref[...], preferred_element_type=F32)
    o_ref[...] = acc

    tile = jnp.minimum(s, pl.num_programs(0) - 2)
    seq_start = lax.rem(tile, tiles_per_seq) == 0
    u_ref[0:u_halo, :] = jnp.where(seq_start, 0.0, u_ref[t_blk:t_blk + u_halo, :])
    a_ref[0:a_halo, :] = jnp.where(seq_start, 0.0, a_ref[t_blk:t_blk + a_halo, :])
    u_ref[u_halo:, :] = zc_ref[...] * zh_ref[...]
    a_ref[a_halo:, :] = zv_ref[...] * _sigmoid(zg_ref[...])
    ccb = ccb_ref[...]
    lng = lng_ref[...]
    lnb = lnb_ref[...]
    for i in range(t_blk // CONV_ROWS):
        base = i * CONV_ROWS
        rows = slice(base, base + CONV_ROWS)
        uw = u_ref[base:base + u_halo + CONV_ROWS, :]
        acc = scw_ref[SC_TAPS - 1:SC_TAPS, :] * uw[u_halo:, :]
        for j in range(SC_TAPS - 1):
            off = u_halo - (SC_TAPS - 1) + j
            acc += scw_ref[j:j + 1, :] * uw[off:off + CONV_ROWS, :]
        ms_next[rows, :] = (zb_ref[rows, :] * acc).astype(ms_next.dtype)

        strips = []
        for st in range(a_ref.shape[1] // LANES):
            lanes = slice(st * LANES, (st + 1) * LANES)
            aw = a_ref[base:base + a_halo + CONV_ROWS, lanes]
            acc = ccb[:, lanes]
            for res in range(SUBLANES):
                shifted = aw if res == 0 else pltpu.roll(aw, a_halo + CONV_ROWS - res, axis=0)
                for j in range(CC_TAPS):
                    off = a_halo - (CC_TAPS - 1) + j
                    if off % SUBLANES == res:
                        acc += ccw_ref[j:j + 1, lanes] * shifted[off - res:off - res + CONV_ROWS, :]
            strips.append(acc)
        acc = jnp.concatenate(strips, axis=1)
        mu = jnp.mean(acc, axis=-1, keepdims=True)
        d = acc - mu
        var = jnp.mean(d * d, axis=-1, keepdims=True)
        mc_next[rows, :] = _silu(d * lax.rsqrt(var + EPS) * lng + lnb).astype(mc_next.dtype)

    ns_ref[0] = u_ref[u_halo + t_blk - (SC_TAPS - 1):u_halo + t_blk, :]
    nc_ref[0] = a_ref[a_halo + t_blk - (CC_TAPS - 1):a_halo + t_blk, :]


def _conv_out_proj(x, mh, z, scw, ccw, ccb, lng, lnb, w_out, *, layer, seq, sc_off, cc_off):
    m, d = x.shape
    hw = mh.shape[1]
    w = scw.shape[2]
    t_blk = CONV_TIME_BLOCK
    n_tiles = m // t_blk
    tiles_per_seq = seq // t_blk
    conv_tile = lambda s: jnp.minimum(s, n_tiles - 1)
    proj_tile = lambda s: jnp.maximum(s - 1, 0)
    assert z.shape[2] == w
    zcol = lambda off: pl.BlockSpec((None, t_blk, w), lambda s, off=off: (off, conv_tile(s), 0))
    prow = lambda width: pl.BlockSpec((t_blk, width), lambda s: (proj_tile(s), 0))
    whole = lambda a: _layer_block(a, layer)
    w_rows = lambda rows, blk: pl.BlockSpec((rows, d), lambda s: (blk, 0))
    state = lambda taps: pl.BlockSpec((1, taps - 1, w),
                                      lambda s: (conv_tile(s) // tiles_per_seq, 0, 0))
    return pl.pallas_call(
        functools.partial(_conv_out_proj_kernel, tiles_per_seq=tiles_per_seq),
        grid=(n_tiles + 1,),
        in_specs=[prow(d), prow(hw),
                  zcol(sc_off), zcol(sc_off + 1), zcol(sc_off + 2), zcol(cc_off), zcol(cc_off + 1),
                  whole(scw), whole(ccw), whole(ccb), whole(lng), whole(lnb),
                  w_rows(hw, 0), w_rows(w, hw // w), w_rows(w, hw // w + 1)],
        out_specs=[prow(d), state(SC_TAPS), state(CC_TAPS)],
        out_shape=[jax.ShapeDtypeStruct((m, d), F32),
                   jax.ShapeDtypeStruct((m // seq, SC_TAPS - 1, w), F32),
                   jax.ShapeDtypeStruct((m // seq, CC_TAPS - 1, w), F32)],
        scratch_shapes=[pltpu.VMEM((SUBLANES + t_blk, w), F32),
                        pltpu.VMEM((CONV_ROWS + t_blk, w), F32),
                        pltpu.VMEM((t_blk, w), BF16), pltpu.VMEM((t_blk, w), BF16),
                        pltpu.VMEM((t_blk, w), BF16), pltpu.VMEM((t_blk, w), BF16)],
        compiler_params=_params("arbitrary"),
        name="conv_out_proj",
    )(x, mh, z, z, z, z, z, scw, ccw, ccb, lng, lnb, w_out, w_out, w_out)


def _mix_sample_kernel(*refs):
    so_ref, nc_ref = refs[-5], refs[-3]

    @pl.when(pl.program_id(0) == 0)
    def _():
        _mix_sample_step(*refs[:11], *refs[-6:])

    @pl.when(pl.program_id(0) > 0)
    def _():
        so_ref[...] = jnp.zeros(so_ref.shape, so_ref.dtype)
        nc_ref[...] = jnp.zeros(nc_ref.shape, nc_ref.dtype)


def _mix_sample_step(z_ref, st_ref, ss_ref, sc_ref, lb_ref, gn_ref, scw_ref, ccw_ref, ccb_ref,
                     lng_ref, lnb_ref, mix_ref, so_ref, ns_ref, nc_ref, oh_ref, cv_ref):
    hw = HG_HEADS * HG_DIM
    w = scw_ref.shape[1]
    eye = (lax.broadcasted_iota(jnp.int32, (HG_DIM, HG_DIM), 0)
           == lax.broadcasted_iota(jnp.int32, (HG_DIM, HG_DIM), 1))

    def column(x_row):
        return jnp.sum(jnp.where(eye, x_row, 0.0), axis=1, keepdims=True)

    n_tok = z_ref.shape[1]
    zw = z_ref.shape[2]

    def zcols(lo, hi):
        parts = [z_ref[c, :, max(lo - c * zw, 0):min(hi - c * zw, zw)]
                 for c in range(lo // zw, (hi - 1) // zw + 1)]
        return parts[0] if len(parts) == 1 else jnp.concatenate(parts, axis=1)

    q_all = _silu(zcols(0, hw))
    f_all, k_all = _forget_and_key(zcols(hw, 2 * hw), *_gate_constants(lb_ref[...]))
    decay_all = jnp.exp(jnp.log(jnp.maximum(f_all, F_FLOOR)))
    v_all = zcols(2 * hw, 3 * hw)
    for t in range(n_tok):
        tr = slice(t, t + 1)
        for h in range(HG_HEADS):
            cols = slice(h * HG_DIM, (h + 1) * HG_DIM)
            s_new = (column(decay_all[tr, cols]) * st_ref[t, h]
                     + column(k_all[tr, cols]) * v_all[tr, cols])
            so_ref[t, h] = s_new
            oh_ref[tr, cols] = jnp.dot(q_all[tr, cols].astype(BF16), s_new.astype(BF16),
                                       preferred_element_type=F32)
    gate_all = _silu(zcols(3 * hw, 4 * hw))
    for h in range(HG_HEADS):
        cols = slice(h * HG_DIM, (h + 1) * HG_DIM)
        o = oh_ref[:, cols]
        o = o * lax.rsqrt(jnp.mean(o * o, axis=-1, keepdims=True) + EPS) * gn_ref[:, cols]
        oh_ref[:, cols] = o * gate_all[:, cols]
    mix_ref[:, 0:hw] = oh_ref[...].astype(mix_ref.dtype)

    off = 4 * hw
    zb = zcols(off, off + w)
    u = zcols(off + w, off + 2 * w) * zcols(off + 2 * w, off + 3 * w)
    conv = scw_ref[SC_TAPS - 1:SC_TAPS, :] * u
    for j in range(SC_TAPS - 1):
        conv += scw_ref[j:j + 1, :] * ss_ref[:, j * w:(j + 1) * w]
    mix_ref[:, hw:hw + w] = (zb * conv).astype(mix_ref.dtype)
    for j in range(SC_TAPS - 2):
        ns_ref[:, j * w:(j + 1) * w] = ss_ref[:, (j + 1) * w:(j + 2) * w]
    ns_ref[:, (SC_TAPS - 2) * w:] = u

    off = 4 * hw + 3 * w
    a = zcols(off, off + w) * _sigmoid(zcols(off + w, off + 2 * w))
    past_taps = ccw_ref[0:CC_TAPS - 1, :]
    for t in range(n_tok):
        cv_ref[t:t + 1, :] = jnp.sum(past_taps * sc_ref[t], axis=0, keepdims=True)
        nc_ref[t, 0:CC_TAPS - 2, :] = sc_ref[t, 1:CC_TAPS - 1, :]
        nc_ref[t, CC_TAPS - 2:CC_TAPS - 1, :] = a[t:t + 1, :]
    conv = ccb_ref[...] + ccw_ref[CC_TAPS - 1:CC_TAPS, :] * a + cv_ref[...]
    mu = jnp.mean(conv, axis=-1, keepdims=True)
    d = conv - mu
    var = jnp.mean(d * d, axis=-1, keepdims=True)
    y = d * lax.rsqrt(var + EPS) * lng_ref[...] + lnb_ref[...]
    mix_ref[:, hw + w:] = _silu(y).astype(mix_ref.dtype)


def _mix_sample(z, st_h, st_s, st_c, lb, gn, scw, ccw, ccb, lng, lnb, *, layer, prev):
    nb = z.shape[1]
    tb = SAMPLE_TOKENS_PER_STEP
    hw = HG_HEADS * HG_DIM
    w = scw.shape[2]
    nblk = nb // tb
    n_pass = st_h.shape[0] - layer if prev is None else 1
    tok = lambda p, i: jnp.where(p == 0, i, nblk - 1)
    rows = lambda width: pl.BlockSpec((tb, width), lambda p, i: (tok(p, i), 0))
    layer_rows = lambda width: pl.BlockSpec((None, tb, width), lambda p, i: (layer, tok(p, i), 0))
    whole = lambda a: _layer_block(a, layer)
    state_tile = (None, tb, HG_HEADS, HG_DIM, HG_DIM)
    conv_tile = (None, tb) + st_c.shape[2:]
    args = [z, st_h, st_s, st_c, lb, gn, scw, ccw, ccb, lng, lnb]
    in_specs = [pl.BlockSpec((z.shape[0], tb, z.shape[2]), lambda p, i: (0, tok(p, i), 0)),
                pl.BlockSpec(state_tile, lambda p, i: (layer, tok(p, i), 0, 0, 0)),
                layer_rows(st_s.shape[2]),
                pl.BlockSpec(conv_tile, lambda p, i: (layer, tok(p, i), 0, 0)),
                whole(lb), whole(gn), whole(scw), whole(ccw), whole(ccb), whole(lng), whole(lnb)]
    aliases = {}
    if prev is not None:
        aliases = {len(args): 1, len(args) + 1: 3}
        args += list(prev)
        in_specs += [pl.BlockSpec(memory_space=pl.ANY)] * 2
    return pl.pallas_call(
        _mix_sample_kernel,
        grid=(n_pass, nblk),
        in_specs=in_specs,
        out_specs=[rows(hw + 2 * w),
                   pl.BlockSpec(state_tile, lambda p, i: (layer + p, i, 0, 0, 0)),
                   rows(st_s.shape[2]),
                   pl.BlockSpec(conv_tile, lambda p, i: (layer + p, i, 0, 0))],
        out_shape=[jax.ShapeDtypeStruct((nb, hw + 2 * w), BF16),
                   jax.ShapeDtypeStruct(st_h.shape, F32),
                   jax.ShapeDtypeStruct((nb, st_s.shape[2]), F32),
                   jax.ShapeDtypeStruct(st_c.shape, F32)],
        scratch_shapes=[pltpu.VMEM((tb, hw), F32), pltpu.VMEM((tb, w), F32)],
        input_output_aliases=aliases,
        compiler_params=_params("arbitrary", "arbitrary"),
        name="mix_sample",
    )(*args)


def kernel(x_prompt, x_sample, state_hgrn, state_sconv, state_cconv, g_mix, w_in, hgrn_lb,
           hgrn_norm_g, sconv_w, cconv_w, cconv_b, cconv_ln_g, cconv_ln_b, w_out, g_mlp,
           w_up, w_down, g_final):
    batch, seq, d = x_prompt.shape
    nb = x_sample.shape[0]
    depth = w_in.shape[0]
    hw = HG_HEADS * HG_DIM
    w = sconv_w.shape[2]
    assert state_hgrn.shape[2:] == (HG_HEADS, HG_DIM, HG_DIM)
    assert w_in.shape[2] == 4 * hw + 5 * w and w_out.shape[1] == hw + 2 * w
    assert sconv_w.shape[1] == SC_TAPS and cconv_w.shape[1] == CC_TAPS
    assert seq % CONV_TIME_BLOCK == 0 and seq % CHUNK == 0 and nb % SAMPLE_TOKENS_PER_STEP == 0

    p = jax.nn.softmax(hgrn_lb.astype(F32), axis=0)
    lb_all = jnp.cumsum(p, axis=0) - p[0:1]

    rows = lambda a: a.reshape(depth, 1, -1)
    g1, g2, gf = rows(g_mix), rows(g_mlp), g_final.reshape(1, -1)
    lb, gn = rows(lb_all), rows(hgrn_norm_g)
    ccb, lng, lnb = rows(cconv_b), rows(cconv_ln_g), rows(cconv_ln_b)

    xp = x_prompt.reshape(batch * seq, d)
    xs = x_sample.reshape(nb, d)
    st_s = state_sconv.reshape(depth, nb, (SC_TAPS - 1) * w)
    sc_off = 4 * hw // w
    cc_off = sc_off + 3

    ph, ps, pc, ss = [], [], [], []
    sample_states = None
    for li in range(depth):
        last = li == depth - 1

        if li == 0:
            zs, w_in_b = _norm_matmul_cast(xs, g1, w_in, layer=li, tn=WEIGHT_BLOCK)
        else:
            zs = _norm_matmul(xs, g1, w_in_b, layer=li, tm=nb)
        mix, new_h, new_s, new_c = _mix_sample(
            zs, state_hgrn, st_s, state_cconv, lb, gn, sconv_w, cconv_w, ccb, lng, lnb, layer=li,
            prev=sample_states)
        sample_states = (new_h, new_c)
        ss.append(new_s.reshape(nb, SC_TAPS - 1, w))

        z = _norm_matmul(xp, g1, w_in_b, layer=li, tm=2048)
        jobs = [_CastJob(w_out, li, None), _CastJob(w_up, li, WEIGHT_BLOCK),
                _CastJob(w_down, li, None)] if li == 0 else []
        mh, new_h, *cast = _hgrn_prompt(z, lb, gn, layer=li, batch=batch, seq=seq, cast_jobs=jobs)
        if li == 0:
            w_out_b, w_up_b, w_down_b = cast

        xs = _out_proj_sample(xs, mix, w_out_b, tk=WEIGHT_BLOCK)

        xp, new_s, new_c = _conv_out_proj(xp, mh, z, sconv_w, cconv_w, ccb, lng, lnb, w_out_b,
                                          layer=li, seq=seq, sc_off=sc_off, cc_off=cc_off)
        jobs = [] if last else [_CastJob(w_in, li + 1, WEIGHT_BLOCK), _CastJob(w_out, li + 1, None),
                                _CastJob(w_up, li + 1, WEIGHT_BLOCK), _CastJob(w_down, li + 1, None)]
        xp, xs, *cast = _ffn(xp, xs, g2, w_up_b, w_down_b, gf, layer=li, tm=1024, final_norm=last,
                             cast_jobs=jobs)
        if not last:
            w_in_b, w_out_b, w_up_b, w_down_b = cast
        ph.append(new_h)
        ps.append(new_s)
        pc.append(new_c)

    sh, sc = sample_states
    return (xp.reshape(batch, seq, d), xs.reshape(nb, 1, d), jnp.stack(ph), jnp.stack(ps),
            jnp.stack(pc), sh, jnp.stack(ss), sc)
```

```python
import functools
from typing import NamedTuple, Optional

import numpy as np
import jax
import jax.numpy as jnp
from jax import lax
from jax.experimental import pallas as pl
from jax.experimental.pallas import tpu as pltpu

F32 = jnp.float32
BF16 = jnp.bfloat16

EPS = 1e-6
F_FLOOR = 1e-30

LANES = 128
SUBLANES = 8
BF16_SUBLANES = 16
VMEM_LIMIT_BYTES = 56 * 1024 * 1024
IN_PROJ_VMEM_LIMIT_BYTES = 58 * 1024 * 1024

HG_HEADS = 8
HG_DIM = 128
SC_TAPS = 3
CC_TAPS = 31
CHUNK = 128
N_LEVELS = 7
SAMPLE_TOKENS_PER_STEP = 8
WEIGHT_BLOCK = 512
CONV_TIME_BLOCK = 512
CONV_ROWS = 32


def _params(*sem, vmem_limit_bytes=VMEM_LIMIT_BYTES):
    return pltpu.CompilerParams(dimension_semantics=sem, vmem_limit_bytes=vmem_limit_bytes)


def _layer_block(a, layer):
    zeros = (0,) * (a.ndim - 1)
    return pl.BlockSpec((None,) + a.shape[1:], lambda *_: (layer,) + zeros)


def _rmsnorm(x, g):
    return x * lax.rsqrt(jnp.mean(x * x, axis=-1, keepdims=True) + EPS) * g


def _sigmoid(x):
    return 0.5 * jnp.tanh(0.5 * x) + 0.5


def _silu(x):
    h = 0.5 * x
    return h + h * jnp.tanh(h)


def _gate_constants(lb):
    f_scale = 0.5 * (1.0 - lb)
    return f_scale, lb + f_scale


def _forget_and_key(z, f_scale, f_shift):
    w = f_scale * jnp.tanh(0.5 * z)
    return f_shift + w, f_scale - w


class _CastJob(NamedTuple):
    w: jax.Array
    layer: int
    col_block: Optional[int]


def _cast_job_plan(job, n_steps, step_of):
    _, rows, cols = job.w.shape
    chunk_rows = BF16_SUBLANES
    while rows % chunk_rows or rows // chunk_rows > n_steps:
        chunk_rows += BF16_SUBLANES
    n_chunks = rows // chunk_rows
    chunk = lambda *idx: jnp.minimum(step_of(*idx), n_chunks - 1)
    in_spec = pl.BlockSpec((None, chunk_rows, cols), lambda *idx: (job.layer, chunk(*idx), 0))
    if job.col_block is None:
        out_spec = pl.BlockSpec((chunk_rows, cols), lambda *idx: (chunk(*idx), 0))
        return in_spec, out_spec, jax.ShapeDtypeStruct((rows, cols), BF16)
    nblk = cols // job.col_block
    out_spec = pl.BlockSpec((nblk, chunk_rows, job.col_block), lambda *idx: (0, chunk(*idx), 0))
    return in_spec, out_spec, jax.ShapeDtypeStruct((nblk, rows, job.col_block), BF16)


def _run_cast_jobs(src_refs, dst_refs):
    for src, dst in zip(src_refs, dst_refs):
        if len(dst.shape) == 2:
            dst[...] = src[...].astype(dst.dtype)
        else:
            width = dst.shape[2]
            for c in range(dst.shape[0]):
                dst[c] = src[:, c * width:(c + 1) * width].astype(dst.dtype)


def _norm_matmul_kernel(x_ref, g_ref, w_ref, o_ref, h_ref):
    @pl.when(pl.program_id(1) == 0)
    def _():
        h_ref[...] = _rmsnorm(x_ref[...], g_ref[...]).astype(BF16)

    o_ref[...] = jnp.dot(h_ref[...], w_ref[...], preferred_element_type=F32)


def _norm_matmul(x, g, wb, *, layer, tm):
    m, k = x.shape
    nblk, _, tn = wb.shape
    return pl.pallas_call(
        _norm_matmul_kernel,
        grid=(m // tm, nblk),
        in_specs=[pl.BlockSpec((tm, k), lambda i, j: (i, 0)),
                  _layer_block(g, layer),
                  pl.BlockSpec((None, k, tn), lambda i, j: (j, 0, 0))],
        out_specs=pl.BlockSpec((None, tm, tn), lambda i, j: (j, i, 0)),
        out_shape=jax.ShapeDtypeStruct((nblk, m, tn), F32),
        scratch_shapes=[pltpu.VMEM((tm, k), BF16)],
        compiler_params=_params("arbitrary", "arbitrary",
                                vmem_limit_bytes=IN_PROJ_VMEM_LIMIT_BYTES),
        name="in_proj",
    )(x, g, wb)


def _norm_matmul_cast_kernel(x_ref, g_ref, w_ref, o_ref, wb_ref, h_ref):
    @pl.when(pl.program_id(0) == 0)
    def _():
        h_ref[...] = _rmsnorm(x_ref[...], g_ref[...]).astype(BF16)

    wb = w_ref[...].astype(BF16)
    wb_ref[...] = wb
    o_ref[...] = jnp.dot(h_ref[...], wb, preferred_element_type=F32)


def _norm_matmul_cast(x, g, w, *, layer, tn):
    m, k = x.shape
    n = w.shape[2]
    return pl.pallas_call(
        _norm_matmul_cast_kernel,
        grid=(n // tn,),
        in_specs=[pl.BlockSpec((m, k), lambda j: (0, 0)),
                  _layer_block(g, layer),
                  pl.BlockSpec((None, k, tn), lambda j: (layer, 0, j))],
        out_specs=[pl.BlockSpec((None, m, tn), lambda j: (j, 0, 0)),
                   pl.BlockSpec((None, k, tn), lambda j: (j, 0, 0))],
        out_shape=[jax.ShapeDtypeStruct((n // tn, m, tn), F32),
                   jax.ShapeDtypeStruct((n // tn, k, tn), BF16)],
        scratch_shapes=[pltpu.VMEM((m, k), BF16)],
        compiler_params=_params("arbitrary"),
        name="in_proj_cast",
    )(x, g, w)


def _out_proj_sample_kernel(x_ref, m_ref, w_ref, o_ref):
    @pl.when(pl.program_id(0) == 0)
    def _():
        o_ref[...] = x_ref[...]

    o_ref[...] += jnp.dot(m_ref[...], w_ref[...], preferred_element_type=F32)


def _out_proj_sample(x, mix, w_out_b, *, tk):
    m, d = x.shape
    kdim = mix.shape[1]
    return pl.pallas_call(
        _out_proj_sample_kernel,
        grid=(kdim // tk,),
        in_specs=[pl.BlockSpec((m, d), lambda r: (0, 0)),
                  pl.BlockSpec((m, tk), lambda r: (0, r)),
                  pl.BlockSpec((tk, d), lambda r: (r, 0))],
        out_specs=pl.BlockSpec((m, d), lambda r: (0, 0)),
        out_shape=jax.ShapeDtypeStruct((m, d), F32),
        compiler_params=_params("arbitrary"),
        name="out_proj_sample",
    )(x, mix, w_out_b)


def _ffn_kernel(x_ref, g_ref, wu_ref, wd_ref, gf_ref, xs_ref, *rest, final_norm, n_jobs):
    cast_src, (o_ref, ys_ref), cast_dst = rest[:n_jobs], rest[n_jobs:n_jobs + 2], rest[n_jobs + 2:-2]
    h_ref, hs_ref = rest[-2:]
    j = pl.program_id(1)

    def mlp_step(src_ref, acc_ref, hid_ref):
        @pl.when(j == 0)
        def _():
            x = src_ref[...]
            hid_ref[...] = _rmsnorm(x, g_ref[...]).astype(BF16)
            acc_ref[...] = x

        u = jnp.dot(hid_ref[...], wu_ref[...], preferred_element_type=F32)
        r = jnp.maximum(u, 0.0)
        acc_ref[...] += jnp.dot((r * r).astype(BF16), wd_ref[...], preferred_element_type=F32)

        if final_norm:
            @pl.when(j == pl.num_programs(1) - 1)
            def _():
                acc_ref[...] = _rmsnorm(acc_ref[...], gf_ref[...])

    mlp_step(x_ref, o_ref, h_ref)

    @pl.when(pl.program_id(0) == 0)
    def _():
        mlp_step(xs_ref, ys_ref, hs_ref)

    _run_cast_jobs(cast_src, cast_dst)


def _ffn(x, xs, g, wub, wdb, gf, *, layer, tm, final_norm, cast_jobs=()):
    m, d = x.shape
    ms = xs.shape[0]
    nblk, _, tf = wub.shape
    grid = (m // tm, nblk)
    plans = [_cast_job_plan(job, grid[0] * grid[1], lambda i, j: i * nblk + j)
             for job in cast_jobs]
    resident = pl.BlockSpec((ms, d), lambda i, j: (0, 0))
    return pl.pallas_call(
        functools.partial(_ffn_kernel, final_norm=final_norm, n_jobs=len(plans)),
        grid=grid,
        in_specs=[pl.BlockSpec((tm, d), lambda i, j: (i, 0)),
                  _layer_block(g, layer),
                  pl.BlockSpec((None, d, tf), lambda i, j: (j, 0, 0)),
                  pl.BlockSpec((tf, d), lambda i, j: (j, 0)),
                  pl.BlockSpec((1, d), lambda i, j: (0, 0)),
                  resident] + [p[0] for p in plans],
        out_specs=[pl.BlockSpec((tm, d), lambda i, j: (i, 0)), resident] + [p[1] for p in plans],
        out_shape=[jax.ShapeDtypeStruct((m, d), F32), jax.ShapeDtypeStruct((ms, d), F32)]
        + [p[2] for p in plans],
        scratch_shapes=[pltpu.VMEM((tm, d), BF16), pltpu.VMEM((ms, d), BF16)],
        compiler_params=_params("arbitrary", "arbitrary"),
        name="ffn",
    )(x, g, wub, wdb, gf, xs, *[job.w for job in cast_jobs])


LOG2E = 1.4426950408889634
LOW_LEVELS = 3


def _prefix_sum_matrix():
    t = np.arange(CHUNK)[:, None]
    r = np.arange(CHUNK)[None, :]
    return (r <= t).astype(np.float32)


def _hgrn_prompt_kernel(zq_ref, zf_ref, zi_ref, zo_ref, lb_ref, gn_ref, tri_ref, *rest, n_jobs):
    cast_src, (y_ref, s_ref), cast_dst = rest[:n_jobs], rest[n_jobs:n_jobs + 2], rest[n_jobs + 2:]
    _run_cast_jobs(cast_src, cast_dst)
    n_chunks = zq_ref.shape[0] // CHUNK
    n_tiles = CHUNK // SUBLANES
    lb = lb_ref[...]
    gn = gn_ref[...]
    tri = tri_ref[...]
    f_scale, f_shift = _gate_constants(lb)
    sub = lax.broadcasted_iota(jnp.int32, (SUBLANES, HG_DIM), 0)
    lane = lax.broadcasted_iota(jnp.int32, (SUBLANES, HG_DIM), 1)
    col = lane & (SUBLANES - 1)
    below = col < sub
    pat_low = [below & (((sub ^ col) >> l) == 1) for l in range(LOW_LEVELS)]
    pat_diag = col == sub
    upper_low = [((sub >> l) & 1) == 1 for l in range(LOW_LEVELS)]
    nt = (((1,), (1,)), ((), ()))
    tiled = lambda x: x.reshape(n_tiles, SUBLANES, HG_DIM)
    flat = lambda x: x.reshape(CHUNK, HG_DIM)

    def tile_row(x3, s):
        return jnp.broadcast_to(x3[:, s:s + 1, :], x3.shape)

    def chunk(c, st):
        rows = pl.ds(pl.multiple_of(c * CHUNK, CHUNK), CHUNK)
        q = _silu(zq_ref[rows, :])
        f, kk = _forget_and_key(zf_ref[rows, :], f_scale, f_shift)
        fc = jnp.maximum(f, F_FLOOR)
        g = jnp.log(fc) * LOG2E
        v = zi_ref[rows, :].astype(BF16)

        g_hi = g.astype(BF16)
        g_lo = (g - g_hi.astype(F32)).astype(BF16)
        b2 = jnp.dot(tri, jnp.concatenate([g_hi, g_lo], axis=1), preferred_element_type=F32)
        b = b2[:, :HG_DIM] + b2[:, HG_DIM:]
        b_last = b[CHUNK - 1:CHUNK, :]
        qd = (q * jnp.exp2(jnp.minimum(b, 0.0))).astype(BF16)
        kd = (kk * jnp.exp2(jnp.minimum(b_last - b, 0.0))).astype(BF16)

        half = SUBLANES // 2
        b3, q3, k3 = tiled(b), tiled(q), tiled(kk)
        ref_low = [None,
                   jnp.where(sub < half, tile_row(b3, 1), tile_row(b3, half + 1)),
                   tile_row(b3, half - 1)]
        a_low = []
        for l in range(LOW_LEVELS):
            upper = upper_low[l]
            if l == 0:
                x3 = jnp.where(upper, q3 * tiled(fc), k3)
            else:
                x3 = jnp.exp2(-jnp.abs(b3 - ref_low[l])) * jnp.where(upper, q3, k3)
            x = flat(x3).astype(BF16)
            a_low.append(lax.dot_general(x, x, nt, preferred_element_type=F32))

        a_up = {}
        for l in range(LOW_LEVELS, N_LEVELS):
            h = 1 << l
            pairs = CHUNK // (2 * h)
            bg = b.reshape(pairs, 2, h, HG_DIM)
            ref = bg[:, 0, h - 1:h, :]
            x_lo = kk.reshape(pairs, 2, h, HG_DIM)[:, 0] * jnp.exp2(jnp.minimum(ref - bg[:, 0], 0.0))
            x_up = q.reshape(pairs, 2, h, HG_DIM)[:, 1] * jnp.exp2(jnp.minimum(bg[:, 1] - ref, 0.0))
            x_all = jnp.stack([x_lo, x_up], axis=1).reshape(CHUNK, HG_DIM).astype(BF16)
            a_up[l] = lax.dot_general(x_up.reshape(CHUNK // 2, HG_DIM).astype(BF16), x_all, nt,
                                      preferred_element_type=F32)

        dqk = jnp.sum(q * kk, axis=1, keepdims=True)
        tiles = []
        for j in range(n_tiles):
            r = slice(j * SUBLANES, (j + 1) * SUBLANES)
            blk = jnp.where(pat_diag, dqk[r], 0.0)
            for l in range(LOW_LEVELS):
                blk = jnp.where(pat_low[l], a_low[l][r], blk)
            blk = jnp.where((lane >> 3) == j, blk, 0.0)
            for l in range(LOW_LEVELS, N_LEVELS):
                h = 1 << l
                t0 = j * SUBLANES
                if (t0 // h) % 2 == 1:
                    pair = t0 // (2 * h)
                    r_up = pair * h + (t0 - pair * 2 * h - h)
                    blk = jnp.where(lane < pair * 2 * h + h, a_up[l][r_up:r_up + SUBLANES], blk)
            tiles.append(blk)
        att = jnp.concatenate(tiles, axis=0)

        o = jnp.dot(att.astype(BF16), v, preferred_element_type=F32)
        o += lax.dot_general(qd, st.astype(BF16), nt, preferred_element_type=F32)
        st = st * jnp.exp2(jnp.minimum(b_last, 0.0)) + lax.dot_general(
            v, kd, (((0,), (0,)), ((), ())), preferred_element_type=F32)

        o = o * lax.rsqrt(jnp.mean(o * o, axis=-1, keepdims=True) + EPS) * gn
        y_ref[rows, :] = (o * _silu(zo_ref[rows, :])).astype(y_ref.dtype)
        return st

    st = lax.fori_loop(0, n_chunks, chunk, jnp.zeros((HG_DIM, HG_DIM), F32), unroll=16)
    s_ref[0, 0] = st.T


def _hgrn_prompt(z, lb, gn, *, layer, batch, seq, cast_jobs=()):
    hw = HG_HEADS * HG_DIM
    plans = [_cast_job_plan(job, batch * HG_HEADS, lambda n, h: n * HG_HEADS + h)
             for job in cast_jobs]
    per_blk = z.shape[2] // HG_DIM
    col = lambda off: pl.BlockSpec(
        (None, seq, HG_DIM), lambda n, h, off=off: ((off + h) // per_blk, n, (off + h) % per_blk))
    per_head = pl.BlockSpec((None, 1, HG_DIM), lambda n, h: (layer, 0, h))
    tri = jnp.asarray(_prefix_sum_matrix(), BF16)
    return pl.pallas_call(
        functools.partial(_hgrn_prompt_kernel, n_jobs=len(plans)),
        grid=(batch, HG_HEADS),
        in_specs=[col(0), col(HG_HEADS), col(2 * HG_HEADS), col(3 * HG_HEADS), per_head, per_head,
                  pl.BlockSpec(tri.shape, lambda n, h: (0, 0))] + [p[0] for p in plans],
        out_specs=[pl.BlockSpec((seq, HG_DIM), lambda n, h: (n, h)),
                   pl.BlockSpec((1, 1, HG_DIM, HG_DIM), lambda n, h: (n, h, 0, 0))]
        + [p[1] for p in plans],
        out_shape=[jax.ShapeDtypeStruct((batch * seq, hw), BF16),
                   jax.ShapeDtypeStruct((batch, HG_HEADS, HG_DIM, HG_DIM), F32)]
        + [p[2] for p in plans],
        compiler_params=_params("arbitrary", "arbitrary"),
        name="hgrn_prompt",
    )(z, z, z, z, lb, gn, tri, *[job.w for job in cast_jobs])


def _conv_out_proj_kernel(x_ref, mh_ref, zb_ref, zc_ref, zh_ref, zv_ref, zg_ref, scw_ref, ccw_ref,
                          ccb_ref, lng_ref, lnb_ref, wh_ref, ws_ref, wc_ref,
                          o_ref, ns_ref, nc_ref,
                          u_ref, a_ref, ms_next, mc_next, ms_cur, mc_cur, *, tiles_per_seq):
    s = pl.program_id(0)
    t_blk = zb_ref.shape[0]
    u_halo = SUBLANES
    a_halo = CONV_ROWS

    @pl.when(s == 0)
    def _():
        ms_next[...] = jnp.zeros(ms_next.shape, ms_next.dtype)
        mc_next[...] = jnp.zeros(mc_next.shape, mc_next.dtype)
        u_ref[...] = jnp.zeros(u_ref.shape, u_ref.dtype)
        a_ref[...] = jnp.zeros(a_ref.shape, a_ref.dtype)

    ms_cur[...] = ms_next[...]
    mc_cur[...] = mc_next[...]
    acc = x_ref[...]
    acc += jnp.dot(mh_ref[...], wh_ref[...], preferred_element_type=F32)
    acc += jnp.dot(ms_cur[...], ws_ref[...], preferred_element_type=F32)
    acc += jnp.dot(mc_cur[...], wc_ref[...], preferred_element_type=F32)
    o_ref[...] = acc

    tile = jnp.minimum(s, pl.num_programs(0) - 2)
    seq_start = lax.rem(tile, tiles_per_seq) == 0
    u_ref[0:u_halo, :] = jnp.where(seq_start, 0.0, u_ref[t_blk:t_blk + u_halo, :])
    a_ref[0:a_halo, :] = jnp.where(seq_start, 0.0, a_ref[t_blk:t_blk + a_halo, :])
    u_ref[u_halo:, :] = zc_ref[...] * zh_ref[...]
    a_ref[a_halo:, :] = zv_ref[...] * _sigmoid(zg_ref[...])
    ccb = ccb_ref[...]
    lng = lng_ref[...]
    lnb = lnb_ref[...]
    for i in range(t_blk // CONV_ROWS):
        base = i * CONV_ROWS
        rows = slice(base, base + CONV_ROWS)
        uw = u_ref[base:base + u_halo + CONV_ROWS, :]
        acc = scw_ref[SC_TAPS - 1:SC_TAPS, :] * uw[u_halo:, :]
        for j in range(SC_TAPS - 1):
            off = u_halo - (SC_TAPS - 1) + j
            acc += scw_ref[j:j + 1, :] * uw[off:off + CONV_ROWS, :]
        ms_next[rows, :] = (zb_ref[rows, :] * acc).astype(ms_next.dtype)

        strips = []
        for st in range(a_ref.shape[1] // LANES):
            lanes = slice(st * LANES, (st + 1) * LANES)
            aw = a_ref[base:base + a_halo + CONV_ROWS, lanes]
            acc = ccb[:, lanes]
            for res in range(SUBLANES):
                shifted = aw if res == 0 else pltpu.roll(aw, a_halo + CONV_ROWS - res, axis=0)
                for j in range(CC_TAPS):
                    off = a_halo - (CC_TAPS - 1) + j
                    if off % SUBLANES == res:
                        acc += ccw_ref[j:j + 1, lanes] * shifted[off - res:off - res + CONV_ROWS, :]
            strips.append(acc)
        acc = jnp.concatenate(strips, axis=1)
        mu = jnp.mean(acc, axis=-1, keepdims=True)
        d = acc - mu
        var = jnp.mean(d * d, axis=-1, keepdims=True)
        mc_next[rows, :] = _silu(d * lax.rsqrt(var + EPS) * lng + lnb).astype(mc_next.dtype)

    ns_ref[0] = u_ref[u_halo + t_blk - (SC_TAPS - 1):u_halo + t_blk, :]
    nc_ref[0] = a_ref[a_halo + t_blk - (CC_TAPS - 1):a_halo + t_blk, :]


def _conv_out_proj(x, mh, z, scw, ccw, ccb, lng, lnb, w_out, *, layer, seq, sc_off, cc_off):
    m, d = x.shape
    hw = mh.shape[1]
    w = scw.shape[2]
    t_blk = CONV_TIME_BLOCK
    n_tiles = m // t_blk
    tiles_per_seq = seq // t_blk
    conv_tile = lambda s: jnp.minimum(s, n_tiles - 1)
    proj_tile = lambda s: jnp.maximum(s - 1, 0)
    assert z.shape[2] == w
    zcol = lambda off: pl.BlockSpec((None, t_blk, w), lambda s, off=off: (off, conv_tile(s), 0))
    prow = lambda width: pl.BlockSpec((t_blk, width), lambda s: (proj_tile(s), 0))
    whole = lambda a: _layer_block(a, layer)
    w_rows = lambda rows, blk: pl.BlockSpec((rows, d), lambda s: (blk, 0))
    state = lambda taps: pl.BlockSpec((1, taps - 1, w),
                                      lambda s: (conv_tile(s) // tiles_per_seq, 0, 0))
    return pl.pallas_call(
        functools.partial(_conv_out_proj_kernel, tiles_per_seq=tiles_per_seq),
        grid=(n_tiles + 1,),
        in_specs=[prow(d), prow(hw),
                  zcol(sc_off), zcol(sc_off + 1), zcol(sc_off + 2), zcol(cc_off), zcol(cc_off + 1),
                  whole(scw), whole(ccw), whole(ccb), whole(lng), whole(lnb),
                  w_rows(hw, 0), w_rows(w, hw // w), w_rows(w, hw // w + 1)],
        out_specs=[prow(d), state(SC_TAPS), state(CC_TAPS)],
        out_shape=[jax.ShapeDtypeStruct((m, d), F32),
                   jax.ShapeDtypeStruct((m // seq, SC_TAPS - 1, w), F32),
                   jax.ShapeDtypeStruct((m // seq, CC_TAPS - 1, w), F32)],
        scratch_shapes=[pltpu.VMEM((SUBLANES + t_blk, w), F32),
                        pltpu.VMEM((CONV_ROWS + t_blk, w), F32),
                        pltpu.VMEM((t_blk, w), BF16), pltpu.VMEM((t_blk, w), BF16),
                        pltpu.VMEM((t_blk, w), BF16), pltpu.VMEM((t_blk, w), BF16)],
        compiler_params=_params("arbitrary"),
        name="conv_out_proj",
    )(x, mh, z, z, z, z, z, scw, ccw, ccb, lng, lnb, w_out, w_out, w_out)


def _mix_sample_kernel(*refs):
    so_ref, nc_ref = refs[-5], refs[-3]

    @pl.when(pl.program_id(0) == 0)
    def _():
        _mix_sample_step(*refs[:11], *refs[-6:])

    @pl.when(pl.program_id(0) > 0)
    def _():
        so_ref[...] = jnp.zeros(so_ref.shape, so_ref.dtype)
        nc_ref[...] = jnp.zeros(nc_ref.shape, nc_ref.dtype)


def _mix_sample_step(z_ref, st_ref, ss_ref, sc_ref, lb_ref, gn_ref, scw_ref, ccw_ref, ccb_ref,
                     lng_ref, lnb_ref, mix_ref, so_ref, ns_ref, nc_ref, oh_ref, cv_ref):
    hw = HG_HEADS * HG_DIM
    w = scw_ref.shape[1]
    eye = (lax.broadcasted_iota(jnp.int32, (HG_DIM, HG_DIM), 0)
           == lax.broadcasted_iota(jnp.int32, (HG_DIM, HG_DIM), 1))

    def column(x_row):
        return jnp.sum(jnp.where(eye, x_row, 0.0), axis=1, keepdims=True)

    n_tok = z_ref.shape[1]
    zw = z_ref.shape[2]

    def zcols(lo, hi):
        parts = [z_ref[c, :, max(lo - c * zw, 0):min(hi - c * zw, zw)]
                 for c in range(lo // zw, (hi - 1) // zw + 1)]
        return parts[0] if len(parts) == 1 else jnp.concatenate(parts, axis=1)

    q_all = _silu(zcols(0, hw))
    f_all, k_all = _forget_and_key(zcols(hw, 2 * hw), *_gate_constants(lb_ref[...]))
    decay_all = jnp.exp(jnp.log(jnp.maximum(f_all, F_FLOOR)))
    v_all = zcols(2 * hw, 3 * hw)
    token = lax.broadcasted_iota(jnp.int32, (n_tok, HG_DIM), 0)
    for h in range(HG_HEADS):
        cols = slice(h * HG_DIM, (h + 1) * HG_DIM)
        keys_t = k_all[:, cols].T.astype(BF16)
        for t in range(n_tok):
            tr = slice(t, t + 1)
            v_t = jnp.where(token == t, v_all[:, cols], 0.0).astype(BF16)
            s_new = (column(decay_all[tr, cols]) * st_ref[t, h]
                     + jnp.dot(keys_t, v_t, preferred_element_type=F32))
            so_ref[t, h] = s_new
            oh_ref[tr, cols] = jnp.dot(q_all[tr, cols].astype(BF16), s_new.astype(BF16),
                                       preferred_element_type=F32)
    gate_all = _silu(zcols(3 * hw, 4 * hw))
    for h in range(HG_HEADS):
        cols = slice(h * HG_DIM, (h + 1) * HG_DIM)
        o = oh_ref[:, cols]
        o = o * lax.rsqrt(jnp.mean(o * o, axis=-1, keepdims=True) + EPS) * gn_ref[:, cols]
        oh_ref[:, cols] = o * gate_all[:, cols]
    mix_ref[:, 0:hw] = oh_ref[...].astype(mix_ref.dtype)

    off = 4 * hw
    zb = zcols(off, off + w)
    u = zcols(off + w, off + 2 * w) * zcols(off + 2 * w, off + 3 * w)
    conv = scw_ref[SC_TAPS - 1:SC_TAPS, :] * u
    for j in range(SC_TAPS - 1):
        conv += scw_ref[j:j + 1, :] * ss_ref[:, j * w:(j + 1) * w]
    mix_ref[:, hw:hw + w] = (zb * conv).astype(mix_ref.dtype)
    for j in range(SC_TAPS - 2):
        ns_ref[:, j * w:(j + 1) * w] = ss_ref[:, (j + 1) * w:(j + 2) * w]
    ns_ref[:, (SC_TAPS - 2) * w:] = u

    off = 4 * hw + 3 * w
    a = zcols(off, off + w) * _sigmoid(zcols(off + w, off + 2 * w))
    past_taps = ccw_ref[0:CC_TAPS - 1, :]
    for t in range(n_tok):
        cv_ref[t:t + 1, :] = jnp.sum(past_taps * sc_ref[t], axis=0, keepdims=True)
        nc_ref[t, 0:CC_TAPS - 2, :] = sc_ref[t, 1:CC_TAPS - 1, :]
        nc_ref[t, CC_TAPS - 2:CC_TAPS - 1, :] = a[t:t + 1, :]
    conv = ccb_ref[...] + ccw_ref[CC_TAPS - 1:CC_TAPS, :] * a + cv_ref[...]
    mu = jnp.mean(conv, axis=-1, keepdims=True)
    d = conv - mu
    var = jnp.mean(d * d, axis=-1, keepdims=True)
    y = d * lax.rsqrt(var + EPS) * lng_ref[...] + lnb_ref[...]
    mix_ref[:, hw + w:] = _silu(y).astype(mix_ref.dtype)


def _mix_sample(z, st_h, st_s, st_c, lb, gn, scw, ccw, ccb, lng, lnb, *, layer, prev):
    nb = z.shape[1]
    tb = SAMPLE_TOKENS_PER_STEP
    hw = HG_HEADS * HG_DIM
    w = scw.shape[2]
    nblk = nb // tb
    n_pass = st_h.shape[0] - layer if prev is None else 1
    tok = lambda p, i: jnp.where(p == 0, i, nblk - 1)
    rows = lambda width: pl.BlockSpec((tb, width), lambda p, i: (tok(p, i), 0))
    layer_rows = lambda width: pl.BlockSpec((None, tb, width), lambda p, i: (layer, tok(p, i), 0))
    whole = lambda a: _layer_block(a, layer)
    state_tile = (None, tb, HG_HEADS, HG_DIM, HG_DIM)
    conv_tile = (None, tb) + st_c.shape[2:]
    args = [z, st_h, st_s, st_c, lb, gn, scw, ccw, ccb, lng, lnb]
    in_specs = [pl.BlockSpec((z.shape[0], tb, z.shape[2]), lambda p, i: (0, tok(p, i), 0)),
                pl.BlockSpec(state_tile, lambda p, i: (layer, tok(p, i), 0, 0, 0)),
                layer_rows(st_s.shape[2]),
                pl.BlockSpec(conv_tile, lambda p, i: (layer, tok(p, i), 0, 0)),
                whole(lb), whole(gn), whole(scw), whole(ccw), whole(ccb), whole(lng), whole(lnb)]
    aliases = {}
    if prev is not None:
        aliases = {len(args): 1, len(args) + 1: 3}
        args += list(prev)
        in_specs += [pl.BlockSpec(memory_space=pl.ANY)] * 2
    return pl.pallas_call(
        _mix_sample_kernel,
        grid=(n_pass, nblk),
        in_specs=in_specs,
        out_specs=[rows(hw + 2 * w),
                   pl.BlockSpec(state_tile, lambda p, i: (layer + p, i, 0, 0, 0)),
                   rows(st_s.shape[2]),
                   pl.BlockSpec(conv_tile, lambda p, i: (layer + p, i, 0, 0))],
        out_shape=[jax.ShapeDtypeStruct((nb, hw + 2 * w), BF16),
                   jax.ShapeDtypeStruct(st_h.shape, F32),
                   jax.ShapeDtypeStruct((nb, st_s.shape[2]), F32),
                   jax.ShapeDtypeStruct(st_c.shape, F32)],
        scratch_shapes=[pltpu.VMEM((tb, hw), F32), pltpu.VMEM((tb, w), F32)],
        input_output_aliases=aliases,
        compiler_params=_params("arbitrary", "arbitrary"),
        name="mix_sample",
    )(*args)


def kernel(x_prompt, x_sample, state_hgrn, state_sconv, state_cconv, g_mix, w_in, hgrn_lb,
           hgrn_norm_g, sconv_w, cconv_w, cconv_b, cconv_ln_g, cconv_ln_b, w_out, g_mlp,
           w_up, w_down, g_final):
    batch, seq, d = x_prompt.shape
    nb = x_sample.shape[0]
    depth = w_in.shape[0]
    hw = HG_HEADS * HG_DIM
    w = sconv_w.shape[2]
    assert state_hgrn.shape[2:] == (HG_HEADS, HG_DIM, HG_DIM)
    assert w_in.shape[2] == 4 * hw + 5 * w and w_out.shape[1] == hw + 2 * w
    assert sconv_w.shape[1] == SC_TAPS and cconv_w.shape[1] == CC_TAPS
    assert seq % CONV_TIME_BLOCK == 0 and seq % CHUNK == 0 and nb % SAMPLE_TOKENS_PER_STEP == 0

    p = jax.nn.softmax(hgrn_lb.astype(F32), axis=0)
    lb_all = jnp.cumsum(p, axis=0) - p[0:1]

    rows = lambda a: a.reshape(depth, 1, -1)
    g1, g2, gf = rows(g_mix), rows(g_mlp), g_final.reshape(1, -1)
    lb, gn = rows(lb_all), rows(hgrn_norm_g)
    ccb, lng, lnb = rows(cconv_b), rows(cconv_ln_g), rows(cconv_ln_b)

    xp = x_prompt.reshape(batch * seq, d)
    xs = x_sample.reshape(nb, d)
    st_s = state_sconv.reshape(depth, nb, (SC_TAPS - 1) * w)
    sc_off = 4 * hw // w
    cc_off = sc_off + 3

    ph, ps, pc, ss = [], [], [], []
    sample_states = None
    for li in range(depth):
        last = li == depth - 1

        if li == 0:
            zs, w_in_b = _norm_matmul_cast(xs, g1, w_in, layer=li, tn=WEIGHT_BLOCK)
        else:
            zs = _norm_matmul(xs, g1, w_in_b, layer=li, tm=nb)
        mix, new_h, new_s, new_c = _mix_sample(
            zs, state_hgrn, st_s, state_cconv, lb, gn, sconv_w, cconv_w, ccb, lng, lnb, layer=li,
            prev=sample_states)
        sample_states = (new_h, new_c)
        ss.append(new_s.reshape(nb, SC_TAPS - 1, w))

        z = _norm_matmul(xp, g1, w_in_b, layer=li, tm=2048)
        jobs = [_CastJob(w_out, li, None), _CastJob(w_up, li, WEIGHT_BLOCK),
                _CastJob(w_down, li, None)] if li == 0 else []
        mh, new_h, *cast = _hgrn_prompt(z, lb, gn, layer=li, batch=batch, seq=seq, cast_jobs=jobs)
        if li == 0:
            w_out_b, w_up_b, w_down_b = cast

        xs = _out_proj_sample(xs, mix, w_out_b, tk=WEIGHT_BLOCK)

        xp, new_s, new_c = _conv_out_proj(xp, mh, z, sconv_w, cconv_w, ccb, lng, lnb, w_out_b,
                                          layer=li, seq=seq, sc_off=sc_off, cc_off=cc_off)
        jobs = [] if last else [_CastJob(w_in, li + 1, WEIGHT_BLOCK), _CastJob(w_out, li + 1, None),
                                _CastJob(w_up, li + 1, WEIGHT_BLOCK), _CastJob(w_down, li + 1, None)]
        xp, xs, *cast = _ffn(xp, xs, g2, w_up_b, w_down_b, gf, layer=li, tm=1024, final_norm=last,
                             cast_jobs=jobs)
        if not last:
            w_in_b, w_out_b, w_up_b, w_down_b = cast
        ph.append(new_h)
        ps.append(new_s)
        pc.append(new_c)

    sh, sc = sample_states
    return (xp.reshape(batch, seq, d), xs.reshape(nb, 1, d), jnp.stack(ph), jnp.stack(ps),
            jnp.stack(pc), sh, jnp.stack(ss), sc)
```

```python
import functools
from typing import NamedTuple, Optional

import numpy as np
import jax
import jax.numpy as jnp
from jax import lax
from jax.experimental import pallas as pl
from jax.experimental.pallas import tpu as pltpu

F32 = jnp.float32
BF16 = jnp.bfloat16

EPS = 1e-6
F_FLOOR = 1e-30

LANES = 128
SUBLANES = 8
BF16_SUBLANES = 16
VMEM_LIMIT_BYTES = 56 * 1024 * 1024
IN_PROJ_VMEM_LIMIT_BYTES = 58 * 1024 * 1024

HG_HEADS = 8
HG_DIM = 128
SC_TAPS = 3
CC_TAPS = 31
CHUNK = 128
N_LEVELS = 7
SAMPLE_TOKENS_PER_STEP = 8
WEIGHT_BLOCK = 512
CONV_TIME_BLOCK = 512
CONV_ROWS = 32


def _params(*sem, vmem_limit_bytes=VMEM_LIMIT_BYTES):
    return pltpu.CompilerParams(dimension_semantics=sem, vmem_limit_bytes=vmem_limit_bytes)


def _layer_block(a, layer):
    zeros = (0,) * (a.ndim - 1)
    return pl.BlockSpec((None,) + a.shape[1:], lambda *_: (layer,) + zeros)


def _rmsnorm(x, g):
    return x * lax.rsqrt(jnp.mean(x * x, axis=-1, keepdims=True) + EPS) * g


def _sigmoid(x):
    return 0.5 * jnp.tanh(0.5 * x) + 0.5


def _silu(x):
    h = 0.5 * x
    return h + h * jnp.tanh(h)


def _gate_constants(lb):
    f_scale = 0.5 * (1.0 - lb)
    return f_scale, lb + f_scale


def _forget_and_key(z, f_scale, f_shift):
    w = f_scale * jnp.tanh(0.5 * z)
    return f_shift + w, f_scale - w


class _CastJob(NamedTuple):
    w: jax.Array
    layer: int
    col_block: Optional[int]


def _cast_job_plan(job, n_steps, step_of):
    _, rows, cols = job.w.shape
    chunk_rows = BF16_SUBLANES
    while rows % chunk_rows or rows // chunk_rows > n_steps:
        chunk_rows += BF16_SUBLANES
    n_chunks = rows // chunk_rows
    chunk = lambda *idx: jnp.minimum(step_of(*idx), n_chunks - 1)
    in_spec = pl.BlockSpec((None, chunk_rows, cols), lambda *idx: (job.layer, chunk(*idx), 0))
    if job.col_block is None:
        out_spec = pl.BlockSpec((chunk_rows, cols), lambda *idx: (chunk(*idx), 0))
        return in_spec, out_spec, jax.ShapeDtypeStruct((rows, cols), BF16)
    nblk = cols // job.col_block
    out_spec = pl.BlockSpec((nblk, chunk_rows, job.col_block), lambda *idx: (0, chunk(*idx), 0))
    return in_spec, out_spec, jax.ShapeDtypeStruct((nblk, rows, job.col_block), BF16)


def _run_cast_jobs(src_refs, dst_refs):
    for src, dst in zip(src_refs, dst_refs):
        if len(dst.shape) == 2:
            dst[...] = src[...].astype(dst.dtype)
        else:
            width = dst.shape[2]
            for c in range(dst.shape[0]):
                dst[c] = src[:, c * width:(c + 1) * width].astype(dst.dtype)


def _norm_matmul_kernel(x_ref, g_ref, w_ref, o_ref, h_ref):
    @pl.when(pl.program_id(1) == 0)
    def _():
        h_ref[...] = _rmsnorm(x_ref[...], g_ref[...]).astype(BF16)

    o_ref[...] = jnp.dot(h_ref[...], w_ref[...], preferred_element_type=F32)


def _norm_matmul(x, g, wb, *, layer, tm):
    m, k = x.shape
    nblk, _, tn = wb.shape
    return pl.pallas_call(
        _norm_matmul_kernel,
        grid=(m // tm, nblk),
        in_specs=[pl.BlockSpec((tm, k), lambda i, j: (i, 0)),
                  _layer_block(g, layer),
                  pl.BlockSpec((None, k, tn), lambda i, j: (j, 0, 0))],
        out_specs=pl.BlockSpec((None, tm, tn), lambda i, j: (j, i, 0)),
        out_shape=jax.ShapeDtypeStruct((nblk, m, tn), F32),
        scratch_shapes=[pltpu.VMEM((tm, k), BF16)],
        compiler_params=_params("arbitrary", "arbitrary",
                                vmem_limit_bytes=IN_PROJ_VMEM_LIMIT_BYTES),
        name="in_proj",
    )(x, g, wb)


def _norm_matmul_cast_kernel(x_ref, g_ref, w_ref, o_ref, wb_ref, h_ref):
    @pl.when(pl.program_id(0) == 0)
    def _():
        h_ref[...] = _rmsnorm(x_ref[...], g_ref[...]).astype(BF16)

    wb = w_ref[...].astype(BF16)
    wb_ref[...] = wb
    o_ref[...] = jnp.dot(h_ref[...], wb, preferred_element_type=F32)


def _norm_matmul_cast(x, g, w, *, layer, tn):
    m, k = x.shape
    n = w.shape[2]
    return pl.pallas_call(
        _norm_matmul_cast_kernel,
        grid=(n // tn,),
        in_specs=[pl.BlockSpec((m, k), lambda j: (0, 0)),
                  _layer_block(g, layer),
                  pl.BlockSpec((None, k, tn), lambda j: (layer, 0, j))],
        out_specs=[pl.BlockSpec((None, m, tn), lambda j: (j, 0, 0)),
                   pl.BlockSpec((None, k, tn), lambda j: (j, 0, 0))],
        out_shape=[jax.ShapeDtypeStruct((n // tn, m, tn), F32),
                   jax.ShapeDtypeStruct((n // tn, k, tn), BF16)],
        scratch_shapes=[pltpu.VMEM((m, k), BF16)],
        compiler_params=_params("arbitrary"),
        name="in_proj_cast",
    )(x, g, w)


def _out_proj_sample_kernel(x_ref, m_ref, w_ref, o_ref):
    @pl.when(pl.program_id(0) == 0)
    def _():
        o_ref[...] = x_ref[...]

    o_ref[...] += jnp.dot(m_ref[...], w_ref[...], preferred_element_type=F32)


def _out_proj_sample(x, mix, w_out_b, *, tk):
    m, d = x.shape
    kdim = mix.shape[1]
    return pl.pallas_call(
        _out_proj_sample_kernel,
        grid=(kdim // tk,),
        in_specs=[pl.BlockSpec((m, d), lambda r: (0, 0)),
                  pl.BlockSpec((m, tk), lambda r: (0, r)),
                  pl.BlockSpec((tk, d), lambda r: (r, 0))],
        out_specs=pl.BlockSpec((m, d), lambda r: (0, 0)),
        out_shape=jax.ShapeDtypeStruct((m, d), F32),
        compiler_params=_params("arbitrary"),
        name="out_proj_sample",
    )(x, mix, w_out_b)


def _ffn_kernel(x_ref, g_ref, wu_ref, wd_ref, gf_ref, xs_ref, *rest, final_norm, n_jobs):
    cast_src, (o_ref, ys_ref), cast_dst = rest[:n_jobs], rest[n_jobs:n_jobs + 2], rest[n_jobs + 2:-2]
    h_ref, hs_ref = rest[-2:]
    j = pl.program_id(1)

    def mlp_step(src_ref, acc_ref, hid_ref):
        @pl.when(j == 0)
        def _():
            x = src_ref[...]
            hid_ref[...] = _rmsnorm(x, g_ref[...]).astype(BF16)
            acc_ref[...] = x

        u = jnp.dot(hid_ref[...], wu_ref[...], preferred_element_type=F32)
        r = jnp.maximum(u, 0.0)
        acc_ref[...] += jnp.dot((r * r).astype(BF16), wd_ref[...], preferred_element_type=F32)

        if final_norm:
            @pl.when(j == pl.num_programs(1) - 1)
            def _():
                acc_ref[...] = _rmsnorm(acc_ref[...], gf_ref[...])

    mlp_step(x_ref, o_ref, h_ref)

    @pl.when(pl.program_id(0) == 0)
    def _():
        mlp_step(xs_ref, ys_ref, hs_ref)

    _run_cast_jobs(cast_src, cast_dst)


def _ffn(x, xs, g, wub, wdb, gf, *, layer, tm, final_norm, cast_jobs=()):
    m, d = x.shape
    ms = xs.shape[0]
    nblk, _, tf = wub.shape
    grid = (m // tm, nblk)
    plans = [_cast_job_plan(job, grid[0] * grid[1], lambda i, j: i * nblk + j)
             for job in cast_jobs]
    resident = pl.BlockSpec((ms, d), lambda i, j: (0, 0))
    return pl.pallas_call(
        functools.partial(_ffn_kernel, final_norm=final_norm, n_jobs=len(plans)),
        grid=grid,
        in_specs=[pl.BlockSpec((tm, d), lambda i, j: (i, 0)),
                  _layer_block(g, layer),
                  pl.BlockSpec((None, d, tf), lambda i, j: (j, 0, 0)),
                  pl.BlockSpec((tf, d), lambda i, j: (j, 0)),
                  pl.BlockSpec((1, d), lambda i, j: (0, 0)),
                  resident] + [p[0] for p in plans],
        out_specs=[pl.BlockSpec((tm, d), lambda i, j: (i, 0)), resident] + [p[1] for p in plans],
        out_shape=[jax.ShapeDtypeStruct((m, d), F32), jax.ShapeDtypeStruct((ms, d), F32)]
        + [p[2] for p in plans],
        scratch_shapes=[pltpu.VMEM((tm, d), BF16), pltpu.VMEM((ms, d), BF16)],
        compiler_params=_params("arbitrary", "arbitrary"),
        name="ffn",
    )(x, g, wub, wdb, gf, xs, *[job.w for job in cast_jobs])


LOG2E = 1.4426950408889634
LOW_LEVELS = 3


def _prefix_sum_matrix():
    t = np.arange(CHUNK)[:, None]
    r = np.arange(CHUNK)[None, :]
    return (r <= t).astype(np.float32)


def _hgrn_prompt_kernel(zq_ref, zf_ref, zi_ref, zo_ref, lb_ref, gn_ref, tri_ref, *rest, n_jobs):
    cast_src, (y_ref, s_ref), cast_dst = rest[:n_jobs], rest[n_jobs:n_jobs + 2], rest[n_jobs + 2:]
    _run_cast_jobs(cast_src, cast_dst)
    n_chunks = zq_ref.shape[0] // CHUNK
    n_tiles = CHUNK // SUBLANES
    lb = lb_ref[...]
    gn = gn_ref[...]
    tri = tri_ref[...]
    f_scale, f_shift = _gate_constants(lb)
    sub = lax.broadcasted_iota(jnp.int32, (SUBLANES, HG_DIM), 0)
    lane = lax.broadcasted_iota(jnp.int32, (SUBLANES, HG_DIM), 1)
    col = lane & (SUBLANES - 1)
    below = col < sub
    pat_low = [below & (((sub ^ col) >> l) == 1) for l in range(LOW_LEVELS)]
    pat_diag = col == sub
    upper_low = [((sub >> l) & 1) == 1 for l in range(LOW_LEVELS)]
    nt = (((1,), (1,)), ((), ()))
    tiled = lambda x: x.reshape(n_tiles, SUBLANES, HG_DIM)
    flat = lambda x: x.reshape(CHUNK, HG_DIM)

    def tile_row(x3, s):
        return jnp.broadcast_to(x3[:, s:s + 1, :], x3.shape)

    def chunk(c, st):
        rows = pl.ds(pl.multiple_of(c * CHUNK, CHUNK), CHUNK)
        q = _silu(zq_ref[rows, :])
        f, kk = _forget_and_key(zf_ref[rows, :], f_scale, f_shift)
        fc = jnp.maximum(f, F_FLOOR)
        g = jnp.log(fc) * LOG2E
        v = zi_ref[rows, :].astype(BF16)

        g_hi = g.astype(BF16)
        g_lo = (g - g_hi.astype(F32)).astype(BF16)
        b2 = jnp.dot(tri, jnp.concatenate([g_hi, g_lo], axis=1), preferred_element_type=F32)
        b = b2[:, :HG_DIM] + b2[:, HG_DIM:]
        b_last = b[CHUNK - 1:CHUNK, :]
        qd = (q * jnp.exp2(jnp.minimum(b, 0.0))).astype(BF16)
        kd = (kk * jnp.exp2(jnp.minimum(b_last - b, 0.0))).astype(BF16)

        half = SUBLANES // 2
        b3, q3, k3 = tiled(b), tiled(q), tiled(kk)
        ref_low = [None,
                   jnp.where(sub < half, tile_row(b3, 1), tile_row(b3, half + 1)),
                   tile_row(b3, half - 1)]
        a_low = []
        for l in range(LOW_LEVELS):
            upper = upper_low[l]
            if l == 0:
                x3 = jnp.where(upper, q3 * tiled(fc), k3)
            else:
                x3 = jnp.exp2(-jnp.abs(b3 - ref_low[l])) * jnp.where(upper, q3, k3)
            x = flat(x3).astype(BF16)
            a_low.append(lax.dot_general(x, x, nt, preferred_element_type=F32))

        a_up = {}
        for l in range(LOW_LEVELS, N_LEVELS):
            h = 1 << l
            pairs = CHUNK // (2 * h)
            bg = b.reshape(pairs, 2, h, HG_DIM)
            ref = bg[:, 0, h - 1:h, :]
            x_lo = kk.reshape(pairs, 2, h, HG_DIM)[:, 0] * jnp.exp2(jnp.minimum(ref - bg[:, 0], 0.0))
            x_up = q.reshape(pairs, 2, h, HG_DIM)[:, 1] * jnp.exp2(jnp.minimum(bg[:, 1] - ref, 0.0))
            x_all = jnp.stack([x_lo, x_up], axis=1).reshape(CHUNK, HG_DIM).astype(BF16)
            a_up[l] = lax.dot_general(x_up.reshape(CHUNK // 2, HG_DIM).astype(BF16), x_all, nt,
                                      preferred_element_type=F32)

        dqk = jnp.sum(q * kk, axis=1, keepdims=True)
        tiles = []
        for j in range(n_tiles):
            r = slice(j * SUBLANES, (j + 1) * SUBLANES)
            blk = jnp.where(pat_diag, dqk[r], 0.0)
            for l in range(LOW_LEVELS):
                blk = jnp.where(pat_low[l], a_low[l][r], blk)
            blk = jnp.where((lane >> 3) == j, blk, 0.0)
            for l in range(LOW_LEVELS, N_LEVELS):
                h = 1 << l
                t0 = j * SUBLANES
                if (t0 // h) % 2 == 1:
                    pair = t0 // (2 * h)
                    r_up = pair * h + (t0 - pair * 2 * h - h)
                    blk = jnp.where(lane < pair * 2 * h + h, a_up[l][r_up:r_up + SUBLANES], blk)
            tiles.append(blk)
        att = jnp.concatenate(tiles, axis=0)

        o = jnp.dot(att.astype(BF16), v, preferred_element_type=F32)
        o += lax.dot_general(qd, st.astype(BF16), nt, preferred_element_type=F32)
        st = st * jnp.exp2(jnp.minimum(b_last, 0.0)) + lax.dot_general(
            v, kd, (((0,), (0,)), ((), ())), preferred_element_type=F32)

        o = o * lax.rsqrt(jnp.mean(o * o, axis=-1, keepdims=True) + EPS) * gn
        y_ref[rows, :] = (o * _silu(zo_ref[rows, :])).astype(y_ref.dtype)
        return st

    st = lax.fori_loop(0, n_chunks, chunk, jnp.zeros((HG_DIM, HG_DIM), F32), unroll=16)
    s_ref[0, 0] = st.T


def _hgrn_prompt(z, lb, gn, *, layer, batch, seq, cast_jobs=()):
    hw = HG_HEADS * HG_DIM
    plans = [_cast_job_plan(job, batch * HG_HEADS, lambda n, h: n * HG_HEADS + h)
             for job in cast_jobs]
    per_blk = z.shape[2] // HG_DIM
    col = lambda off: pl.BlockSpec(
        (None, seq, HG_DIM), lambda n, h, off=off: ((off + h) // per_blk, n, (off + h) % per_blk))
    per_head = pl.BlockSpec((None, 1, HG_DIM), lambda n, h: (layer, 0, h))
    tri = jnp.asarray(_prefix_sum_matrix(), BF16)
    return pl.pallas_call(
        functools.partial(_hgrn_prompt_kernel, n_jobs=len(plans)),
        grid=(batch, HG_HEADS),
        in_specs=[col(0), col(HG_HEADS), col(2 * HG_HEADS), col(3 * HG_HEADS), per_head, per_head,
                  pl.BlockSpec(tri.shape, lambda n, h: (0, 0))] + [p[0] for p in plans],
        out_specs=[pl.BlockSpec((seq, HG_DIM), lambda n, h: (n, h)),
                   pl.BlockSpec((1, 1, HG_DIM, HG_DIM), lambda n, h: (n, h, 0, 0))]
        + [p[1] for p in plans],
        out_shape=[jax.ShapeDtypeStruct((batch * seq, hw), BF16),
                   jax.ShapeDtypeStruct((batch, HG_HEADS, HG_DIM, HG_DIM), F32)]
        + [p[2] for p in plans],
        compiler_params=_params("arbitrary", "arbitrary"),
        name="hgrn_prompt",
    )(z, z, z, z, lb, gn, tri, *[job.w for job in cast_jobs])


def _conv_out_proj_kernel(x_ref, mh_ref, zb_ref, zc_ref, zh_ref, zv_ref, zg_ref, scw_ref, ccw_ref,
                          ccb_ref, lng_ref, lnb_ref, wh_ref, ws_ref, wc_ref,
                          o_ref, ns_ref, nc_ref,
                          u_ref, a_ref, ms_ref, mc_ref, *, tiles_per_seq):
    _prompt_convs(pl.program_id(0), zb_ref, zc_ref, zh_ref, zv_ref, zg_ref, scw_ref, ccw_ref,
                  ccb_ref, lng_ref, lnb_ref, ns_ref, nc_ref, u_ref, a_ref, ms_ref, mc_ref,
                  tiles_per_seq=tiles_per_seq)
    acc = x_ref[...]
    acc += jnp.dot(mh_ref[...], wh_ref[...], preferred_element_type=F32)
    acc += jnp.dot(ms_ref[...], ws_ref[...], preferred_element_type=F32)
    acc += jnp.dot(mc_ref[...], wc_ref[...], preferred_element_type=F32)
    o_ref[...] = acc


def _prompt_convs(tile, zb_ref, zc_ref, zh_ref, zv_ref, zg_ref, scw_ref, ccw_ref, ccb_ref, lng_ref,
                  lnb_ref, ns_ref, nc_ref, u_ref, a_ref, ms_next, mc_next, *, tiles_per_seq):
    t_blk = zb_ref.shape[0]
    u_halo = SUBLANES
    a_halo = CONV_ROWS

    @pl.when(tile == 0)
    def _():
        u_ref[...] = jnp.zeros(u_ref.shape, u_ref.dtype)
        a_ref[...] = jnp.zeros(a_ref.shape, a_ref.dtype)

    seq_start = lax.rem(tile, tiles_per_seq) == 0
    u_ref[0:u_halo, :] = jnp.where(seq_start, 0.0, u_ref[t_blk:t_blk + u_halo, :])
    a_ref[0:a_halo, :] = jnp.where(seq_start, 0.0, a_ref[t_blk:t_blk + a_halo, :])
    u_ref[u_halo:, :] = zc_ref[...] * zh_ref[...]
    a_ref[a_halo:, :] = zv_ref[...] * _sigmoid(zg_ref[...])
    ccb = ccb_ref[...]
    lng = lng_ref[...]
    lnb = lnb_ref[...]
    for i in range(t_blk // CONV_ROWS):
        base = i * CONV_ROWS
        rows = slice(base, base + CONV_ROWS)
        uw = u_ref[base:base + u_halo + CONV_ROWS, :]
        acc = scw_ref[SC_TAPS - 1:SC_TAPS, :] * uw[u_halo:, :]
        for j in range(SC_TAPS - 1):
            off = u_halo - (SC_TAPS - 1) + j
            acc += scw_ref[j:j + 1, :] * uw[off:off + CONV_ROWS, :]
        ms_next[rows, :] = (zb_ref[rows, :] * acc).astype(ms_next.dtype)

        strips = []
        for st in range(a_ref.shape[1] // LANES):
            lanes = slice(st * LANES, (st + 1) * LANES)
            aw = a_ref[base:base + a_halo + CONV_ROWS, lanes]
            acc = ccb[:, lanes]
            for res in range(SUBLANES):
                shifted = aw if res == 0 else pltpu.roll(aw, a_halo + CONV_ROWS - res, axis=0)
                for j in range(CC_TAPS):
                    off = a_halo - (CC_TAPS - 1) + j
                    if off % SUBLANES == res:
                        acc += ccw_ref[j:j + 1, lanes] * shifted[off - res:off - res + CONV_ROWS, :]
            strips.append(acc)
        acc = jnp.concatenate(strips, axis=1)
        mu = jnp.mean(acc, axis=-1, keepdims=True)
        d = acc - mu
        var = jnp.mean(d * d, axis=-1, keepdims=True)
        mc_next[rows, :] = _silu(d * lax.rsqrt(var + EPS) * lng + lnb).astype(mc_next.dtype)

    ns_ref[0] = u_ref[u_halo + t_blk - (SC_TAPS - 1):u_halo + t_blk, :]
    nc_ref[0] = a_ref[a_halo + t_blk - (CC_TAPS - 1):a_halo + t_blk, :]


def _conv_out_proj(x, mh, z, scw, ccw, ccb, lng, lnb, w_out, *, layer, seq, sc_off, cc_off):
    m, d = x.shape
    hw = mh.shape[1]
    w = scw.shape[2]
    t_blk = CONV_TIME_BLOCK
    n_tiles = m // t_blk
    tiles_per_seq = seq // t_blk
    assert z.shape[2] == w
    zcol = lambda off: pl.BlockSpec((None, t_blk, w), lambda s, off=off: (off, s, 0))
    prow = lambda width: pl.BlockSpec((t_blk, width), lambda s: (s, 0))
    whole = lambda a: _layer_block(a, layer)
    w_rows = lambda rows, blk: pl.BlockSpec((rows, d), lambda s: (blk, 0))
    state = lambda taps: pl.BlockSpec((1, taps - 1, w), lambda s: (s // tiles_per_seq, 0, 0))
    return pl.pallas_call(
        functools.partial(_conv_out_proj_kernel, tiles_per_seq=tiles_per_seq),
        grid=(n_tiles,),
        in_specs=[prow(d), prow(hw),
                  zcol(sc_off), zcol(sc_off + 1), zcol(sc_off + 2), zcol(cc_off), zcol(cc_off + 1),
                  whole(scw), whole(ccw), whole(ccb), whole(lng), whole(lnb),
                  w_rows(hw, 0), w_rows(w, hw // w), w_rows(w, hw // w + 1)],
        out_specs=[prow(d), state(SC_TAPS), state(CC_TAPS)],
        out_shape=[jax.ShapeDtypeStruct((m, d), F32),
                   jax.ShapeDtypeStruct((m // seq, SC_TAPS - 1, w), F32),
                   jax.ShapeDtypeStruct((m // seq, CC_TAPS - 1, w), F32)],
        scratch_shapes=[pltpu.VMEM((SUBLANES + t_blk, w), F32),
                        pltpu.VMEM((CONV_ROWS + t_blk, w), F32),
                        pltpu.VMEM((t_blk, w), BF16), pltpu.VMEM((t_blk, w), BF16)],
        compiler_params=_params("arbitrary"),
        name="conv_out_proj",
    )(x, mh, z, z, z, z, z, scw, ccw, ccb, lng, lnb, w_out, w_out, w_out)


def _mix_sample_kernel(*refs):
    so_ref, nc_ref = refs[-5], refs[-3]

    @pl.when(pl.program_id(0) == 0)
    def _():
        _mix_sample_step(*refs[:11], *refs[-6:])

    @pl.when(pl.program_id(0) > 0)
    def _():
        so_ref[...] = jnp.zeros(so_ref.shape, so_ref.dtype)
        nc_ref[...] = jnp.zeros(nc_ref.shape, nc_ref.dtype)


def _mix_sample_step(z_ref, st_ref, ss_ref, sc_ref, lb_ref, gn_ref, scw_ref, ccw_ref, ccb_ref,
                     lng_ref, lnb_ref, mix_ref, so_ref, ns_ref, nc_ref, oh_ref, cv_ref):
    hw = HG_HEADS * HG_DIM
    w = scw_ref.shape[1]
    eye = (lax.broadcasted_iota(jnp.int32, (HG_DIM, HG_DIM), 0)
           == lax.broadcasted_iota(jnp.int32, (HG_DIM, HG_DIM), 1))

    def column(x_row):
        return jnp.sum(jnp.where(eye, x_row, 0.0), axis=1, keepdims=True)

    n_tok = z_ref.shape[1]
    zw = z_ref.shape[2]

    def zcols(lo, hi):
        parts = [z_ref[c, :, max(lo - c * zw, 0):min(hi - c * zw, zw)]
                 for c in range(lo // zw, (hi - 1) // zw + 1)]
        return parts[0] if len(parts) == 1 else jnp.concatenate(parts, axis=1)

    q_all = _silu(zcols(0, hw))
    f_all, k_all = _forget_and_key(zcols(hw, 2 * hw), *_gate_constants(lb_ref[...]))
    decay_all = jnp.exp(jnp.log(jnp.maximum(f_all, F_FLOOR)))
    v_all = zcols(2 * hw, 3 * hw)
    for t in range(n_tok):
        tr = slice(t, t + 1)
        for h in range(HG_HEADS):
            cols = slice(h * HG_DIM, (h + 1) * HG_DIM)
            s_new = (column(decay_all[tr, cols]) * st_ref[t, h]
                     + column(k_all[tr, cols]) * v_all[tr, cols])
            so_ref[t, h] = s_new
            oh_ref[tr, cols] = jnp.dot(q_all[tr, cols].astype(BF16), s_new.astype(BF16),
                                       preferred_element_type=F32)
    gate_all = _silu(zcols(3 * hw, 4 * hw))
    for h in range(HG_HEADS):
        cols = slice(h * HG_DIM, (h + 1) * HG_DIM)
        o = oh_ref[:, cols]
        o = o * lax.rsqrt(jnp.mean(o * o, axis=-1, keepdims=True) + EPS) * gn_ref[:, cols]
        oh_ref[:, cols] = o * gate_all[:, cols]
    mix_ref[:, 0:hw] = oh_ref[...].astype(mix_ref.dtype)

    off = 4 * hw
    zb = zcols(off, off + w)
    u = zcols(off + w, off + 2 * w) * zcols(off + 2 * w, off + 3 * w)
    conv = scw_ref[SC_TAPS - 1:SC_TAPS, :] * u
    for j in range(SC_TAPS - 1):
        conv += scw_ref[j:j + 1, :] * ss_ref[:, j * w:(j + 1) * w]
    mix_ref[:, hw:hw + w] = (zb * conv).astype(mix_ref.dtype)
    for j in range(SC_TAPS - 2):
        ns_ref[:, j * w:(j + 1) * w] = ss_ref[:, (j + 1) * w:(j + 2) * w]
    ns_ref[:, (SC_TAPS - 2) * w:] = u

    off = 4 * hw + 3 * w
    a = zcols(off, off + w) * _sigmoid(zcols(off + w, off + 2 * w))
    past_taps = ccw_ref[0:CC_TAPS - 1, :]
    for t in range(n_tok):
        cv_ref[t:t + 1, :] = jnp.sum(past_taps * sc_ref[t], axis=0, keepdims=True)
        nc_ref[t, 0:CC_TAPS - 2, :] = sc_ref[t, 1:CC_TAPS - 1, :]
        nc_ref[t, CC_TAPS - 2:CC_TAPS - 1, :] = a[t:t + 1, :]
    conv = ccb_ref[...] + ccw_ref[CC_TAPS - 1:CC_TAPS, :] * a + cv_ref[...]
    mu = jnp.mean(conv, axis=-1, keepdims=True)
    d = conv - mu
    var = jnp.mean(d * d, axis=-1, keepdims=True)
    y = d * lax.rsqrt(var + EPS) * lng_ref[...] + lnb_ref[...]
    mix_ref[:, hw + w:] = _silu(y).astype(mix_ref.dtype)


def _mix_sample(z, st_h, st_s, st_c, lb, gn, scw, ccw, ccb, lng, lnb, *, layer, prev):
    nb = z.shape[1]
    tb = SAMPLE_TOKENS_PER_STEP
    hw = HG_HEADS * HG_DIM
    w = scw.shape[2]
    nblk = nb // tb
    n_pass = st_h.shape[0] - layer if prev is None else 1
    tok = lambda p, i: jnp.where(p == 0, i, nblk - 1)
    rows = lambda width: pl.BlockSpec((tb, width), lambda p, i: (tok(p, i), 0))
    layer_rows = lambda width: pl.BlockSpec((None, tb, width), lambda p, i: (layer, tok(p, i), 0))
    whole = lambda a: _layer_block(a, layer)
    state_tile = (None, tb, HG_HEADS, HG_DIM, HG_DIM)
    conv_tile = (None, tb) + st_c.shape[2:]
    args = [z, st_h, st_s, st_c, lb, gn, scw, ccw, ccb, lng, lnb]
    in_specs = [pl.BlockSpec((z.shape[0], tb, z.shape[2]), lambda p, i: (0, tok(p, i), 0)),
                pl.BlockSpec(state_tile, lambda p, i: (layer, tok(p, i), 0, 0, 0)),
                layer_rows(st_s.shape[2]),
                pl.BlockSpec(conv_tile, lambda p, i: (layer, tok(p, i), 0, 0)),
                whole(lb), whole(gn), whole(scw), whole(ccw), whole(ccb), whole(lng), whole(lnb)]
    aliases = {}
    if prev is not None:
        aliases = {len(args): 1, len(args) + 1: 3}
        args += list(prev)
        in_specs += [pl.BlockSpec(memory_space=pl.ANY)] * 2
    return pl.pallas_call(
        _mix_sample_kernel,
        grid=(n_pass, nblk),
        in_specs=in_specs,
        out_specs=[rows(hw + 2 * w),
                   pl.BlockSpec(state_tile, lambda p, i: (layer + p, i, 0, 0, 0)),
                   rows(st_s.shape[2]),
                   pl.BlockSpec(conv_tile, lambda p, i: (layer + p, i, 0, 0))],
        out_shape=[jax.ShapeDtypeStruct((nb, hw + 2 * w), BF16),
                   jax.ShapeDtypeStruct(st_h.shape, F32),
                   jax.ShapeDtypeStruct((nb, st_s.shape[2]), F32),
                   jax.ShapeDtypeStruct(st_c.shape, F32)],
        scratch_shapes=[pltpu.VMEM((tb, hw), F32), pltpu.VMEM((tb, w), F32)],
        input_output_aliases=aliases,
        compiler_params=_params("arbitrary", "arbitrary"),
        name="mix_sample",
    )(*args)


def kernel(x_prompt, x_sample, state_hgrn, state_sconv, state_cconv, g_mix, w_in, hgrn_lb,
           hgrn_norm_g, sconv_w, cconv_w, cconv_b, cconv_ln_g, cconv_ln_b, w_out, g_mlp,
           w_up, w_down, g_final):
    batch, seq, d = x_prompt.shape
    nb = x_sample.shape[0]
    depth = w_in.shape[0]
    hw = HG_HEADS * HG_DIM
    w = sconv_w.shape[2]
    assert state_hgrn.shape[2:] == (HG_HEADS, HG_DIM, HG_DIM)
    assert w_in.shape[2] == 4 * hw + 5 * w and w_out.shape[1] == hw + 2 * w
    assert sconv_w.shape[1] == SC_TAPS and cconv_w.shape[1] == CC_TAPS
    assert seq % CONV_TIME_BLOCK == 0 and seq % CHUNK == 0 and nb % SAMPLE_TOKENS_PER_STEP == 0

    p = jax.nn.softmax(hgrn_lb.astype(F32), axis=0)
    lb_all = jnp.cumsum(p, axis=0) - p[0:1]

    rows = lambda a: a.reshape(depth, 1, -1)
    g1, g2, gf = rows(g_mix), rows(g_mlp), g_final.reshape(1, -1)
    lb, gn = rows(lb_all), rows(hgrn_norm_g)
    ccb, lng, lnb = rows(cconv_b), rows(cconv_ln_g), rows(cconv_ln_b)

    xp = x_prompt.reshape(batch * seq, d)
    xs = x_sample.reshape(nb, d)
    st_s = state_sconv.reshape(depth, nb, (SC_TAPS - 1) * w)
    sc_off = 4 * hw // w
    cc_off = sc_off + 3

    ph, ps, pc, ss = [], [], [], []
    sample_states = None
    for li in range(depth):
        last = li == depth - 1

        if li == 0:
            zs, w_in_b = _norm_matmul_cast(xs, g1, w_in, layer=li, tn=WEIGHT_BLOCK)
        else:
            zs = _norm_matmul(xs, g1, w_in_b, layer=li, tm=nb)
        mix, new_h, new_s, new_c = _mix_sample(
            zs, state_hgrn, st_s, state_cconv, lb, gn, sconv_w, cconv_w, ccb, lng, lnb, layer=li,
            prev=sample_states)
        sample_states = (new_h, new_c)
        ss.append(new_s.reshape(nb, SC_TAPS - 1, w))

        z = _norm_matmul(xp, g1, w_in_b, layer=li, tm=2048)
        jobs = [_CastJob(w_out, li, None), _CastJob(w_up, li, WEIGHT_BLOCK),
                _CastJob(w_down, li, None)] if li == 0 else []
        mh, new_h, *cast = _hgrn_prompt(z, lb, gn, layer=li, batch=batch, seq=seq, cast_jobs=jobs)
        if li == 0:
            w_out_b, w_up_b, w_down_b = cast

        xs = _out_proj_sample(xs, mix, w_out_b, tk=WEIGHT_BLOCK)

        xp, new_s, new_c = _conv_out_proj(xp, mh, z, sconv_w, cconv_w, ccb, lng, lnb, w_out_b,
                                          layer=li, seq=seq, sc_off=sc_off, cc_off=cc_off)
        jobs = [] if last else [_CastJob(w_in, li + 1, WEIGHT_BLOCK), _CastJob(w_out, li + 1, None),
                                _CastJob(w_up, li + 1, WEIGHT_BLOCK), _CastJob(w_down, li + 1, None)]
        xp, xs, *cast = _ffn(xp, xs, g2, w_up_b, w_down_b, gf, layer=li, tm=1024, final_norm=last,
                             cast_jobs=jobs)
        if not last:
            w_in_b, w_out_b, w_up_b, w_down_b = cast
        ph.append(new_h)
        ps.append(new_s)
        pc.append(new_c)

    sh, sc = sample_states
    return (xp.reshape(batch, seq, d), xs.reshape(nb, 1, d), jnp.stack(ph), jnp.stack(ps),
            jnp.stack(pc), sh, jnp.stack(ss), sc)
```

```python
import functools
from typing import NamedTuple, Optional

import numpy as np
import jax
import jax.numpy as jnp
from jax import lax
from jax.experimental import pallas as pl
from jax.experimental.pallas import tpu as pltpu

F32 = jnp.float32
BF16 = jnp.bfloat16

EPS = 1e-6
F_FLOOR = 1e-30

LANES = 128
SUBLANES = 8
BF16_SUBLANES = 16
VMEM_LIMIT_BYTES = 56 * 1024 * 1024
IN_PROJ_VMEM_LIMIT_BYTES = 58 * 1024 * 1024

HG_HEADS = 8
HG_DIM = 128
SC_TAPS = 3
CC_TAPS = 31
CHUNK = 128
N_LEVELS = 7
SAMPLE_TOKENS_PER_STEP = 8
WEIGHT_BLOCK = 512
CONV_TIME_BLOCK = 512
CONV_ROWS = 32


def _params(*sem, vmem_limit_bytes=VMEM_LIMIT_BYTES):
    return pltpu.CompilerParams(dimension_semantics=sem, vmem_limit_bytes=vmem_limit_bytes)


def _layer_block(a, layer):
    zeros = (0,) * (a.ndim - 1)
    return pl.BlockSpec((None,) + a.shape[1:], lambda *_: (layer,) + zeros)


def _rmsnorm(x, g):
    return x * lax.rsqrt(jnp.mean(x * x, axis=-1, keepdims=True) + EPS) * g


def _sigmoid(x):
    return 0.5 * jnp.tanh(0.5 * x) + 0.5


def _silu(x):
    h = 0.5 * x
    return h + h * jnp.tanh(h)


def _gate_constants(lb):
    f_scale = 0.5 * (1.0 - lb)
    return f_scale, lb + f_scale


def _forget_and_key(z, f_scale, f_shift):
    w = f_scale * jnp.tanh(0.5 * z)
    return f_shift + w, f_scale - w


class _CastJob(NamedTuple):
    w: jax.Array
    layer: int
    col_block: Optional[int]


def _cast_job_plan(job, n_steps, step_of):
    _, rows, cols = job.w.shape
    chunk_rows = BF16_SUBLANES
    while rows % chunk_rows or rows // chunk_rows > n_steps:
        chunk_rows += BF16_SUBLANES
    n_chunks = rows // chunk_rows
    chunk = lambda *idx: jnp.minimum(step_of(*idx), n_chunks - 1)
    in_spec = pl.BlockSpec((None, chunk_rows, cols), lambda *idx: (job.layer, chunk(*idx), 0))
    if job.col_block is None:
        out_spec = pl.BlockSpec((chunk_rows, cols), lambda *idx: (chunk(*idx), 0))
        return in_spec, out_spec, jax.ShapeDtypeStruct((rows, cols), BF16)
    nblk = cols // job.col_block
    out_spec = pl.BlockSpec((nblk, chunk_rows, job.col_block), lambda *idx: (0, chunk(*idx), 0))
    return in_spec, out_spec, jax.ShapeDtypeStruct((nblk, rows, job.col_block), BF16)


def _run_cast_jobs(src_refs, dst_refs):
    for src, dst in zip(src_refs, dst_refs):
        if len(dst.shape) == 2:
            dst[...] = src[...].astype(dst.dtype)
        else:
            width = dst.shape[2]
            for c in range(dst.shape[0]):
                dst[c] = src[:, c * width:(c + 1) * width].astype(dst.dtype)


def _norm_matmul_kernel(x_ref, g_ref, w_ref, o_ref, h_ref):
    @pl.when(pl.program_id(1) == 0)
    def _():
        h_ref[...] = _rmsnorm(x_ref[...], g_ref[...]).astype(BF16)

    o_ref[...] = jnp.dot(h_ref[...], w_ref[...], preferred_element_type=F32)


def _norm_matmul(x, g, wb, *, layer, tm):
    m, k = x.shape
    nblk, _, tn = wb.shape
    return pl.pallas_call(
        _norm_matmul_kernel,
        grid=(m // tm, nblk),
        in_specs=[pl.BlockSpec((tm, k), lambda i, j: (i, 0)),
                  _layer_block(g, layer),
                  pl.BlockSpec((None, k, tn), lambda i, j: (j, 0, 0))],
        out_specs=pl.BlockSpec((None, tm, tn), lambda i, j: (j, i, 0)),
        out_shape=jax.ShapeDtypeStruct((nblk, m, tn), F32),
        scratch_shapes=[pltpu.VMEM((tm, k), BF16)],
        compiler_params=_params("arbitrary", "arbitrary",
                                vmem_limit_bytes=IN_PROJ_VMEM_LIMIT_BYTES),
        name="in_proj",
    )(x, g, wb)


def _norm_matmul_cast_kernel(x_ref, g_ref, w_ref, o_ref, wb_ref, h_ref):
    @pl.when(pl.program_id(0) == 0)
    def _():
        h_ref[...] = _rmsnorm(x_ref[...], g_ref[...]).astype(BF16)

    wb = w_ref[...].astype(BF16)
    wb_ref[...] = wb
    o_ref[...] = jnp.dot(h_ref[...], wb, preferred_element_type=F32)


def _norm_matmul_cast(x, g, w, *, layer, tn):
    m, k = x.shape
    n = w.shape[2]
    return pl.pallas_call(
        _norm_matmul_cast_kernel,
        grid=(n // tn,),
        in_specs=[pl.BlockSpec((m, k), lambda j: (0, 0)),
                  _layer_block(g, layer),
                  pl.BlockSpec((None, k, tn), lambda j: (layer, 0, j))],
        out_specs=[pl.BlockSpec((None, m, tn), lambda j: (j, 0, 0)),
                   pl.BlockSpec((None, k, tn), lambda j: (j, 0, 0))],
        out_shape=[jax.ShapeDtypeStruct((n // tn, m, tn), F32),
                   jax.ShapeDtypeStruct((n // tn, k, tn), BF16)],
        scratch_shapes=[pltpu.VMEM((m, k), BF16)],
        compiler_params=_params("arbitrary"),
        name="in_proj_cast",
    )(x, g, w)


def _out_proj_sample_kernel(x_ref, m_ref, w_ref, o_ref):
    @pl.when(pl.program_id(0) == 0)
    def _():
        o_ref[...] = x_ref[...]

    o_ref[...] += jnp.dot(m_ref[...], w_ref[...], preferred_element_type=F32)


def _out_proj_sample(x, mix, w_out_b, *, tk):
    m, d = x.shape
    kdim = mix.shape[1]
    return pl.pallas_call(
        _out_proj_sample_kernel,
        grid=(kdim // tk,),
        in_specs=[pl.BlockSpec((m, d), lambda r: (0, 0)),
                  pl.BlockSpec((m, tk), lambda r: (0, r)),
                  pl.BlockSpec((tk, d), lambda r: (r, 0))],
        out_specs=pl.BlockSpec((m, d), lambda r: (0, 0)),
        out_shape=jax.ShapeDtypeStruct((m, d), F32),
        compiler_params=_params("arbitrary"),
        name="out_proj_sample",
    )(x, mix, w_out_b)


def _ffn_kernel(x_ref, g_ref, wu_ref, wd_ref, gf_ref, xs_ref, *rest, final_norm, n_jobs):
    cast_src, (o_ref, ys_ref), cast_dst = rest[:n_jobs], rest[n_jobs:n_jobs + 2], rest[n_jobs + 2:-2]
    h_ref, hs_ref = rest[-2:]
    j = pl.program_id(1)

    def mlp_step(src_ref, acc_ref, hid_ref):
        @pl.when(j == 0)
        def _():
            x = src_ref[...]
            hid_ref[...] = _rmsnorm(x, g_ref[...]).astype(BF16)
            acc_ref[...] = x

        u = jnp.dot(hid_ref[...], wu_ref[...], preferred_element_type=F32)
        r = jnp.maximum(u, 0.0)
        acc_ref[...] += jnp.dot((r * r).astype(BF16), wd_ref[...], preferred_element_type=F32)

        if final_norm:
            @pl.when(j == pl.num_programs(1) - 1)
            def _():
                acc_ref[...] = _rmsnorm(acc_ref[...], gf_ref[...])

    mlp_step(x_ref, o_ref, h_ref)

    @pl.when(pl.program_id(0) == 0)
    def _():
        mlp_step(xs_ref, ys_ref, hs_ref)

    _run_cast_jobs(cast_src, cast_dst)


def _ffn(x, xs, g, wub, wdb, gf, *, layer, tm, final_norm, cast_jobs=()):
    m, d = x.shape
    ms = xs.shape[0]
    nblk, _, tf = wub.shape
    grid = (m // tm, nblk)
    plans = [_cast_job_plan(job, grid[0] * grid[1], lambda i, j: i * nblk + j)
             for job in cast_jobs]
    resident = pl.BlockSpec((ms, d), lambda i, j: (0, 0))
    return pl.pallas_call(
        functools.partial(_ffn_kernel, final_norm=final_norm, n_jobs=len(plans)),
        grid=grid,
        in_specs=[pl.BlockSpec((tm, d), lambda i, j: (i, 0)),
                  _layer_block(g, layer),
                  pl.BlockSpec((None, d, tf), lambda i, j: (j, 0, 0)),
                  pl.BlockSpec((tf, d), lambda i, j: (j, 0)),
                  pl.BlockSpec((1, d), lambda i, j: (0, 0)),
                  resident] + [p[0] for p in plans],
        out_specs=[pl.BlockSpec((tm, d), lambda i, j: (i, 0)), resident] + [p[1] for p in plans],
        out_shape=[jax.ShapeDtypeStruct((m, d), F32), jax.ShapeDtypeStruct((ms, d), F32)]
        + [p[2] for p in plans],
        scratch_shapes=[pltpu.VMEM((tm, d), BF16), pltpu.VMEM((ms, d), BF16)],
        compiler_params=_params("arbitrary", "arbitrary"),
        name="ffn",
    )(x, g, wub, wdb, gf, xs, *[job.w for job in cast_jobs])


LOG2E = 1.4426950408889634
LOW_LEVELS = 3


def _prefix_sum_matrix():
    t = np.arange(CHUNK)[:, None]
    r = np.arange(CHUNK)[None, :]
    return (r <= t).astype(np.float32)


def _hgrn_prompt_kernel(zq_ref, zf_ref, zi_ref, zo_ref, lb_ref, gn_ref, tri_ref, *rest, n_jobs):
    cast_src, (y_ref, s_ref), cast_dst = rest[:n_jobs], rest[n_jobs:n_jobs + 2], rest[n_jobs + 2:]
    _run_cast_jobs(cast_src, cast_dst)
    n_chunks = zq_ref.shape[0] // CHUNK
    n_tiles = CHUNK // SUBLANES
    lb = lb_ref[...]
    gn = gn_ref[...]
    tri = tri_ref[...]
    f_scale, f_shift = _gate_constants(lb)
    sub = lax.broadcasted_iota(jnp.int32, (SUBLANES, HG_DIM), 0)
    lane = lax.broadcasted_iota(jnp.int32, (SUBLANES, HG_DIM), 1)
    col = lane & (SUBLANES - 1)
    below = col < sub
    pat_low = [below & (((sub ^ col) >> l) == 1) for l in range(LOW_LEVELS)]
    pat_diag = col == sub
    upper_low = [((sub >> l) & 1) == 1 for l in range(LOW_LEVELS)]
    nt = (((1,), (1,)), ((), ()))
    tiled = lambda x: x.reshape(n_tiles, SUBLANES, HG_DIM)
    flat = lambda x: x.reshape(CHUNK, HG_DIM)

    def tile_row(x3, s):
        return jnp.broadcast_to(x3[:, s:s + 1, :], x3.shape)

    def chunk(c, st):
        rows = pl.ds(pl.multiple_of(c * CHUNK, CHUNK), CHUNK)
        q = _silu(zq_ref[rows, :])
        f, kk = _forget_and_key(zf_ref[rows, :], f_scale, f_shift)
        fc = jnp.maximum(f, F_FLOOR)
        g = jnp.log(fc) * LOG2E
        v = zi_ref[rows, :].astype(BF16)

        g_hi = g.astype(BF16)
        g_lo = (g - g_hi.astype(F32)).astype(BF16)
        b2 = jnp.dot(tri, jnp.concatenate([g_hi, g_lo], axis=1), preferred_element_type=F32)
        b = b2[:, :HG_DIM] + b2[:, HG_DIM:]
        b_last = b[CHUNK - 1:CHUNK, :]
        qd = (q * jnp.exp2(jnp.minimum(b, 0.0))).astype(BF16)
        kd = (kk * jnp.exp2(jnp.minimum(b_last - b, 0.0))).astype(BF16)

        half = SUBLANES // 2
        b3, q3, k3 = tiled(b), tiled(q), tiled(kk)
        ref_low = [None,
                   jnp.where(sub < half, tile_row(b3, 1), tile_row(b3, half + 1)),
                   tile_row(b3, half - 1)]
        a_low = []
        for l in range(LOW_LEVELS):
            upper = upper_low[l]
            if l == 0:
                x3 = jnp.where(upper, q3 * tiled(fc), k3)
            else:
                x3 = jnp.exp2(-jnp.abs(b3 - ref_low[l])) * jnp.where(upper, q3, k3)
            x = flat(x3).astype(BF16)
            a_low.append(lax.dot_general(x, x, nt, preferred_element_type=F32))

        a_up = {}
        for l in range(LOW_LEVELS, N_LEVELS):
            h = 1 << l
            pairs = CHUNK // (2 * h)
            bg = b.reshape(pairs, 2, h, HG_DIM)
            ref = bg[:, 0, h - 1:h, :]
            x_lo = kk.reshape(pairs, 2, h, HG_DIM)[:, 0] * jnp.exp2(jnp.minimum(ref - bg[:, 0], 0.0))
            x_up = q.reshape(pairs, 2, h, HG_DIM)[:, 1] * jnp.exp2(jnp.minimum(bg[:, 1] - ref, 0.0))
            x_all = jnp.stack([x_lo, x_up], axis=1).reshape(CHUNK, HG_DIM).astype(BF16)
            a_up[l] = lax.dot_general(x_up.reshape(CHUNK // 2, HG_DIM).astype(BF16), x_all, nt,
                                      preferred_element_type=F32)

        dqk = jnp.sum(q * kk, axis=1, keepdims=True)
        tiles = []
        for j in range(n_tiles):
            r = slice(j * SUBLANES, (j + 1) * SUBLANES)
            blk = jnp.where(pat_diag, dqk[r], 0.0)
            for l in range(LOW_LEVELS):
                blk = jnp.where(pat_low[l], a_low[l][r], blk)
            blk = jnp.where((lane >> 3) == j, blk, 0.0)
            for l in range(LOW_LEVELS, N_LEVELS):
                h = 1 << l
                t0 = j * SUBLANES
                if (t0 // h) % 2 == 1:
                    pair = t0 // (2 * h)
                    r_up = pair * h + (t0 - pair * 2 * h - h)
                    blk = jnp.where(lane < pair * 2 * h + h, a_up[l][r_up:r_up + SUBLANES], blk)
            tiles.append(blk)
        att = jnp.concatenate(tiles, axis=0)

        o = jnp.dot(att.astype(BF16), v, preferred_element_type=F32)
        o += lax.dot_general(qd, st.astype(BF16), nt, preferred_element_type=F32)
        st = st * jnp.exp2(jnp.minimum(b_last, 0.0)) + lax.dot_general(
            v, kd, (((0,), (0,)), ((), ())), preferred_element_type=F32)

        o = o * lax.rsqrt(jnp.mean(o * o, axis=-1, keepdims=True) + EPS) * gn
        y_ref[rows, :] = (o * _silu(zo_ref[rows, :])).astype(y_ref.dtype)
        return st

    st = lax.fori_loop(0, n_chunks, chunk, jnp.zeros((HG_DIM, HG_DIM), F32), unroll=16)
    s_ref[0, 0] = st.T


def _hgrn_prompt(z, lb, gn, *, layer, batch, seq, cast_jobs=()):
    hw = HG_HEADS * HG_DIM
    plans = [_cast_job_plan(job, batch * HG_HEADS, lambda n, h: n * HG_HEADS + h)
             for job in cast_jobs]
    per_blk = z.shape[2] // HG_DIM
    col = lambda off: pl.BlockSpec(
        (None, seq, HG_DIM), lambda n, h, off=off: ((off + h) // per_blk, n, (off + h) % per_blk))
    per_head = pl.BlockSpec((None, 1, HG_DIM), lambda n, h: (layer, 0, h))
    tri = jnp.asarray(_prefix_sum_matrix(), BF16)
    return pl.pallas_call(
        functools.partial(_hgrn_prompt_kernel, n_jobs=len(plans)),
        grid=(batch, HG_HEADS),
        in_specs=[col(0), col(HG_HEADS), col(2 * HG_HEADS), col(3 * HG_HEADS), per_head, per_head,
                  pl.BlockSpec(tri.shape, lambda n, h: (0, 0))] + [p[0] for p in plans],
        out_specs=[pl.BlockSpec((seq, HG_DIM), lambda n, h: (n, h)),
                   pl.BlockSpec((1, 1, HG_DIM, HG_DIM), lambda n, h: (n, h, 0, 0))]
        + [p[1] for p in plans],
        out_shape=[jax.ShapeDtypeStruct((batch * seq, hw), BF16),
                   jax.ShapeDtypeStruct((batch, HG_HEADS, HG_DIM, HG_DIM), F32)]
        + [p[2] for p in plans],
        compiler_params=_params("arbitrary", "arbitrary"),
        name="hgrn_prompt",
    )(z, z, z, z, lb, gn, tri, *[job.w for job in cast_jobs])


def _conv_out_proj_kernel(x_ref, mh_ref, zb_ref, zc_ref, zh_ref, zv_ref, zg_ref, scw_ref, ccw_ref,
                          ccb_ref, lng_ref, lnb_ref, wh_ref, ws_ref, wc_ref,
                          o_ref, ns_ref, nc_ref,
                          u_ref, a_ref, ms_ref, mc_ref, *, tiles_per_seq):
    _prompt_convs(pl.program_id(0), zb_ref, zc_ref, zh_ref, zv_ref, zg_ref, scw_ref, ccw_ref,
                  ccb_ref, lng_ref, lnb_ref, ns_ref, nc_ref, u_ref, a_ref, ms_ref, mc_ref,
                  tiles_per_seq=tiles_per_seq)
    acc = x_ref[...]
    acc += jnp.dot(mh_ref[...], wh_ref[...], preferred_element_type=F32)
    acc += jnp.dot(ms_ref[...], ws_ref[...], preferred_element_type=F32)
    acc += jnp.dot(mc_ref[...], wc_ref[...], preferred_element_type=F32)
    o_ref[...] = acc


def _prompt_convs(tile, zb_ref, zc_ref, zh_ref, zv_ref, zg_ref, scw_ref, ccw_ref, ccb_ref, lng_ref,
                  lnb_ref, ns_ref, nc_ref, u_ref, a_ref, ms_next, mc_next, *, tiles_per_seq):
    t_blk = zb_ref.shape[0]
    u_halo = SUBLANES
    a_halo = CONV_ROWS

    @pl.when(tile == 0)
    def _():
        u_ref[...] = jnp.zeros(u_ref.shape, u_ref.dtype)
        a_ref[...] = jnp.zeros(a_ref.shape, a_ref.dtype)

    seq_start = lax.rem(tile, tiles_per_seq) == 0
    u_ref[0:u_halo, :] = jnp.where(seq_start, 0.0, u_ref[t_blk:t_blk + u_halo, :])
    a_ref[0:a_halo, :] = jnp.where(seq_start, 0.0, a_ref[t_blk:t_blk + a_halo, :])
    u_ref[u_halo:, :] = zc_ref[...] * zh_ref[...]
    a_ref[a_halo:, :] = zv_ref[...] * _sigmoid(zg_ref[...])
    ccb = ccb_ref[...]
    lng = lng_ref[...]
    lnb = lnb_ref[...]
    for i in range(t_blk // CONV_ROWS):
        base = i * CONV_ROWS
        rows = slice(base, base + CONV_ROWS)
        uw = u_ref[base:base + u_halo + CONV_ROWS, :]
        acc = scw_ref[SC_TAPS - 1:SC_TAPS, :] * uw[u_halo:, :]
        for j in range(SC_TAPS - 1):
            off = u_halo - (SC_TAPS - 1) + j
            acc += scw_ref[j:j + 1, :] * uw[off:off + CONV_ROWS, :]
        ms_next[rows, :] = (zb_ref[rows, :] * acc).astype(ms_next.dtype)

        strips = []
        for st in range(a_ref.shape[1] // LANES):
            lanes = slice(st * LANES, (st + 1) * LANES)
            aw = a_ref[base:base + a_halo + CONV_ROWS, lanes]
            acc = ccb[:, lanes]
            for res in range(SUBLANES):
                shifted = aw if res == 0 else pltpu.roll(aw, a_halo + CONV_ROWS - res, axis=0)
                for j in range(CC_TAPS):
                    off = a_halo - (CC_TAPS - 1) + j
                    if off % SUBLANES == res:
                        acc += ccw_ref[j:j + 1, lanes] * shifted[off - res:off - res + CONV_ROWS, :]
            strips.append(acc)
        acc = jnp.concatenate(strips, axis=1)
        mu = jnp.mean(acc, axis=-1, keepdims=True)
        d = acc - mu
        var = jnp.mean(d * d, axis=-1, keepdims=True)
        mc_next[rows, :] = _silu(d * lax.rsqrt(var + EPS) * lng + lnb).astype(mc_next.dtype)

    ns_ref[0] = u_ref[u_halo + t_blk - (SC_TAPS - 1):u_halo + t_blk, :]
    nc_ref[0] = a_ref[a_halo + t_blk - (CC_TAPS - 1):a_halo + t_blk, :]


def _conv_out_proj(x, mh, z, scw, ccw, ccb, lng, lnb, w_out, *, layer, seq, sc_off, cc_off):
    m, d = x.shape
    hw = mh.shape[1]
    w = scw.shape[2]
    t_blk = CONV_TIME_BLOCK
    n_tiles = m // t_blk
    tiles_per_seq = seq // t_blk
    assert z.shape[2] == w
    zcol = lambda off: pl.BlockSpec((None, t_blk, w), lambda s, off=off: (off, s, 0))
    prow = lambda width: pl.BlockSpec((t_blk, width), lambda s: (s, 0))
    whole = lambda a: _layer_block(a, layer)
    w_rows = lambda rows, blk: pl.BlockSpec((rows, d), lambda s: (blk, 0))
    state = lambda taps: pl.BlockSpec((1, taps - 1, w), lambda s: (s // tiles_per_seq, 0, 0))
    return pl.pallas_call(
        functools.partial(_conv_out_proj_kernel, tiles_per_seq=tiles_per_seq),
        grid=(n_tiles,),
        in_specs=[prow(d), prow(hw),
                  zcol(sc_off), zcol(sc_off + 1), zcol(sc_off + 2), zcol(cc_off), zcol(cc_off + 1),
                  whole(scw), whole(ccw), whole(ccb), whole(lng), whole(lnb),
                  w_rows(hw, 0), w_rows(w, hw // w), w_rows(w, hw // w + 1)],
        out_specs=[prow(d), state(SC_TAPS), state(CC_TAPS)],
        out_shape=[jax.ShapeDtypeStruct((m, d), F32),
                   jax.ShapeDtypeStruct((m // seq, SC_TAPS - 1, w), F32),
                   jax.ShapeDtypeStruct((m // seq, CC_TAPS - 1, w), F32)],
        scratch_shapes=[pltpu.VMEM((SUBLANES + t_blk, w), F32),
                        pltpu.VMEM((CONV_ROWS + t_blk, w), F32),
                        pltpu.VMEM((t_blk, w), BF16), pltpu.VMEM((t_blk, w), BF16)],
        compiler_params=_params("arbitrary"),
        name="conv_out_proj",
    )(x, mh, z, z, z, z, z, scw, ccw, ccb, lng, lnb, w_out, w_out, w_out)


STATE_RING = 3


def _mix_sample_kernel(*refs, layer):
    so_ref, nc_ref = refs[-7], refs[-5]
    st_hbm, ring, sem = refs[1], refs[-2], refs[-1]
    i = pl.program_id(1)
    n_blocks = pl.num_programs(1)
    n_tok = ring.shape[1]

    def state_copy(blk):
        slot = lax.rem(blk, STATE_RING)
        return pltpu.make_async_copy(st_hbm.at[layer, pl.ds(blk * n_tok, n_tok)], ring.at[slot],
                                     sem.at[slot])

    @pl.when(pl.program_id(0) == 0)
    def _():
        @pl.when(i == 0)
        def _():
            for blk in range(STATE_RING - 1):
                @pl.when(blk < n_blocks)
                def _():
                    state_copy(blk).start()

        @pl.when(i + STATE_RING - 1 < n_blocks)
        def _():
            state_copy(i + STATE_RING - 1).start()

        state_copy(i).wait()
        _mix_sample_step(refs[0], ring.at[lax.rem(i, STATE_RING)], *refs[2:11], *refs[-8:-2])

    @pl.when(pl.program_id(0) > 0)
    def _():
        so_ref[...] = jnp.zeros(so_ref.shape, so_ref.dtype)
        nc_ref[...] = jnp.zeros(nc_ref.shape, nc_ref.dtype)


def _mix_sample_step(z_ref, st_ref, ss_ref, sc_ref, lb_ref, gn_ref, scw_ref, ccw_ref, ccb_ref,
                     lng_ref, lnb_ref, mix_ref, so_ref, ns_ref, nc_ref, oh_ref, cv_ref):
    hw = HG_HEADS * HG_DIM
    w = scw_ref.shape[1]
    eye = (lax.broadcasted_iota(jnp.int32, (HG_DIM, HG_DIM), 0)
           == lax.broadcasted_iota(jnp.int32, (HG_DIM, HG_DIM), 1))

    def column(x_row):
        return jnp.sum(jnp.where(eye, x_row, 0.0), axis=1, keepdims=True)

    n_tok = z_ref.shape[1]
    zw = z_ref.shape[2]

    def zcols(lo, hi):
        parts = [z_ref[c, :, max(lo - c * zw, 0):min(hi - c * zw, zw)]
                 for c in range(lo // zw, (hi - 1) // zw + 1)]
        return parts[0] if len(parts) == 1 else jnp.concatenate(parts, axis=1)

    q_all = _silu(zcols(0, hw))
    f_all, k_all = _forget_and_key(zcols(hw, 2 * hw), *_gate_constants(lb_ref[...]))
    decay_all = jnp.exp(jnp.log(jnp.maximum(f_all, F_FLOOR)))
    v_all = zcols(2 * hw, 3 * hw)
    for t in range(n_tok):
        tr = slice(t, t + 1)
        for h in range(HG_HEADS):
            cols = slice(h * HG_DIM, (h + 1) * HG_DIM)
            s_new = (column(decay_all[tr, cols]) * st_ref[t, h]
                     + column(k_all[tr, cols]) * v_all[tr, cols])
            so_ref[t, h] = s_new
            oh_ref[tr, cols] = jnp.dot(q_all[tr, cols].astype(BF16), s_new.astype(BF16),
                                       preferred_element_type=F32)
    gate_all = _silu(zcols(3 * hw, 4 * hw))
    for h in range(HG_HEADS):
        cols = slice(h * HG_DIM, (h + 1) * HG_DIM)
        o = oh_ref[:, cols]
        o = o * lax.rsqrt(jnp.mean(o * o, axis=-1, keepdims=True) + EPS) * gn_ref[:, cols]
        oh_ref[:, cols] = o * gate_all[:, cols]
    mix_ref[:, 0:hw] = oh_ref[...].astype(mix_ref.dtype)

    off = 4 * hw
    zb = zcols(off, off + w)
    u = zcols(off + w, off + 2 * w) * zcols(off + 2 * w, off + 3 * w)
    conv = scw_ref[SC_TAPS - 1:SC_TAPS, :] * u
    for j in range(SC_TAPS - 1):
        conv += scw_ref[j:j + 1, :] * ss_ref[:, j * w:(j + 1) * w]
    mix_ref[:, hw:hw + w] = (zb * conv).astype(mix_ref.dtype)
    for j in range(SC_TAPS - 2):
        ns_ref[:, j * w:(j + 1) * w] = ss_ref[:, (j + 1) * w:(j + 2) * w]
    ns_ref[:, (SC_TAPS - 2) * w:] = u

    off = 4 * hw + 3 * w
    a = zcols(off, off + w) * _sigmoid(zcols(off + w, off + 2 * w))
    past_taps = ccw_ref[0:CC_TAPS - 1, :]
    for t in range(n_tok):
        cv_ref[t:t + 1, :] = jnp.sum(past_taps * sc_ref[t], axis=0, keepdims=True)
        nc_ref[t, 0:CC_TAPS - 2, :] = sc_ref[t, 1:CC_TAPS - 1, :]
        nc_ref[t, CC_TAPS - 2:CC_TAPS - 1, :] = a[t:t + 1, :]
    conv = ccb_ref[...] + ccw_ref[CC_TAPS - 1:CC_TAPS, :] * a + cv_ref[...]
    mu = jnp.mean(conv, axis=-1, keepdims=True)
    d = conv - mu
    var = jnp.mean(d * d, axis=-1, keepdims=True)
    y = d * lax.rsqrt(var + EPS) * lng_ref[...] + lnb_ref[...]
    mix_ref[:, hw + w:] = _silu(y).astype(mix_ref.dtype)


def _mix_sample(z, st_h, st_s, st_c, lb, gn, scw, ccw, ccb, lng, lnb, *, layer, prev):
    nb = z.shape[1]
    tb = SAMPLE_TOKENS_PER_STEP
    hw = HG_HEADS * HG_DIM
    w = scw.shape[2]
    nblk = nb // tb
    n_pass = st_h.shape[0] - layer if prev is None else 1
    tok = lambda p, i: jnp.where(p == 0, i, nblk - 1)
    rows = lambda width: pl.BlockSpec((tb, width), lambda p, i: (tok(p, i), 0))
    layer_rows = lambda width: pl.BlockSpec((None, tb, width), lambda p, i: (layer, tok(p, i), 0))
    whole = lambda a: _layer_block(a, layer)
    state_tile = (None, tb, HG_HEADS, HG_DIM, HG_DIM)
    conv_tile = (None, tb) + st_c.shape[2:]
    args = [z, st_h, st_s, st_c, lb, gn, scw, ccw, ccb, lng, lnb]
    in_specs = [pl.BlockSpec((z.shape[0], tb, z.shape[2]), lambda p, i: (0, tok(p, i), 0)),
                pl.BlockSpec(memory_space=pl.ANY),
                layer_rows(st_s.shape[2]),
                pl.BlockSpec(conv_tile, lambda p, i: (layer, tok(p, i), 0, 0)),
                whole(lb), whole(gn), whole(scw), whole(ccw), whole(ccb), whole(lng), whole(lnb)]
    aliases = {}
    if prev is not None:
        aliases = {len(args): 1, len(args) + 1: 3}
        args += list(prev)
        in_specs += [pl.BlockSpec(memory_space=pl.ANY)] * 2
    return pl.pallas_call(
        functools.partial(_mix_sample_kernel, layer=layer),
        grid=(n_pass, nblk),
        in_specs=in_specs,
        out_specs=[rows(hw + 2 * w),
                   pl.BlockSpec(state_tile, lambda p, i: (layer + p, i, 0, 0, 0)),
                   rows(st_s.shape[2]),
                   pl.BlockSpec(conv_tile, lambda p, i: (layer + p, i, 0, 0))],
        out_shape=[jax.ShapeDtypeStruct((nb, hw + 2 * w), BF16),
                   jax.ShapeDtypeStruct(st_h.shape, F32),
                   jax.ShapeDtypeStruct((nb, st_s.shape[2]), F32),
                   jax.ShapeDtypeStruct(st_c.shape, F32)],
        scratch_shapes=[pltpu.VMEM((tb, hw), F32), pltpu.VMEM((tb, w), F32),
                        pltpu.VMEM((STATE_RING, tb, HG_HEADS, HG_DIM, HG_DIM), F32),
                        pltpu.SemaphoreType.DMA((STATE_RING,))],
        input_output_aliases=aliases,
        compiler_params=_params("arbitrary", "arbitrary"),
        name="mix_sample",
    )(*args)


def kernel(x_prompt, x_sample, state_hgrn, state_sconv, state_cconv, g_mix, w_in, hgrn_lb,
           hgrn_norm_g, sconv_w, cconv_w, cconv_b, cconv_ln_g, cconv_ln_b, w_out, g_mlp,
           w_up, w_down, g_final):
    batch, seq, d = x_prompt.shape
    nb = x_sample.shape[0]
    depth = w_in.shape[0]
    hw = HG_HEADS * HG_DIM
    w = sconv_w.shape[2]
    assert state_hgrn.shape[2:] == (HG_HEADS, HG_DIM, HG_DIM)
    assert w_in.shape[2] == 4 * hw + 5 * w and w_out.shape[1] == hw + 2 * w
    assert sconv_w.shape[1] == SC_TAPS and cconv_w.shape[1] == CC_TAPS
    assert seq % CONV_TIME_BLOCK == 0 and seq % CHUNK == 0 and nb % SAMPLE_TOKENS_PER_STEP == 0

    p = jax.nn.softmax(hgrn_lb.astype(F32), axis=0)
    lb_all = jnp.cumsum(p, axis=0) - p[0:1]

    rows = lambda a: a.reshape(depth, 1, -1)
    g1, g2, gf = rows(g_mix), rows(g_mlp), g_final.reshape(1, -1)
    lb, gn = rows(lb_all), rows(hgrn_norm_g)
    ccb, lng, lnb = rows(cconv_b), rows(cconv_ln_g), rows(cconv_ln_b)

    xp = x_prompt.reshape(batch * seq, d)
    xs = x_sample.reshape(nb, d)
    st_s = state_sconv.reshape(depth, nb, (SC_TAPS - 1) * w)
    sc_off = 4 * hw // w
    cc_off = sc_off + 3

    ph, ps, pc, ss = [], [], [], []
    sample_states = None
    for li in range(depth):
        last = li == depth - 1

        if li == 0:
            zs, w_in_b = _norm_matmul_cast(xs, g1, w_in, layer=li, tn=WEIGHT_BLOCK)
        else:
            zs = _norm_matmul(xs, g1, w_in_b, layer=li, tm=nb)
        mix, new_h, new_s, new_c = _mix_sample(
            zs, state_hgrn, st_s, state_cconv, lb, gn, sconv_w, cconv_w, ccb, lng, lnb, layer=li,
            prev=sample_states)
        sample_states = (new_h, new_c)
        ss.append(new_s.reshape(nb, SC_TAPS - 1, w))

        z = _norm_matmul(xp, g1, w_in_b, layer=li, tm=2048)
        jobs = [_CastJob(w_out, li, None), _CastJob(w_up, li, WEIGHT_BLOCK),
                _CastJob(w_down, li, None)] if li == 0 else []
        mh, new_h, *cast = _hgrn_prompt(z, lb, gn, layer=li, batch=batch, seq=seq, cast_jobs=jobs)
        if li == 0:
            w_out_b, w_up_b, w_down_b = cast

        xs = _out_proj_sample(xs, mix, w_out_b, tk=WEIGHT_BLOCK)

        xp, new_s, new_c = _conv_out_proj(xp, mh, z, sconv_w, cconv_w, ccb, lng, lnb, w_out_b,
                                          layer=li, seq=seq, sc_off=sc_off, cc_off=cc_off)
        jobs = [] if last else [_CastJob(w_in, li + 1, WEIGHT_BLOCK), _CastJob(w_out, li + 1, None),
                                _CastJob(w_up, li + 1, WEIGHT_BLOCK), _CastJob(w_down, li + 1, None)]
        xp, xs, *cast = _ffn(xp, xs, g2, w_up_b, w_down_b, gf, layer=li, tm=1024, final_norm=last,
                             cast_jobs=jobs)
        if not last:
            w_in_b, w_out_b, w_up_b, w_down_b = cast
        ph.append(new_h)
        ps.append(new_s)
        pc.append(new_c)

    sh, sc = sample_states
    return (xp.reshape(batch, seq, d), xs.reshape(nb, 1, d), jnp.stack(ph), jnp.stack(ps),
            jnp.stack(pc), sh, jnp.stack(ss), sc)
```
